```python
import math
import jax, jax.numpy as jnp
from jax import lax
import numpy as np

D_MODEL = 4096
BATCH = 8
SEQ = 2048
DEPTH = 1
DEC_BATCH = 8
DEC_SEQ = 32
PAST_LEN = 4096

CHUNK = 64
QBLK = 128
HD = 128
H_A = D_MODEL // (2 * HD)
KV_A = 2
H_IDX = 16
D_IDX = 64
TOPK_MAX = 256
H_B = D_MODEL // (4 * HD)
D_A = H_A * HD
D_B = H_B * 2 * HD
D_MIX = D_A + D_B
COL_SIZES = (H_A * HD, KV_A * HD, KV_A * HD, D_A, H_IDX * D_IDX, D_IDX, H_IDX,
             H_B * 2 * HD, H_B * 2 * HD, H_B * 2 * HD, D_B)
SPLIT_POINTS = tuple(int(v) for v in np.cumsum(COL_SIZES)[:-1])
N_COLS = int(sum(COL_SIZES))
ALPHA = (2.0 * DEPTH) ** 0.25
BETA = (8.0 * DEPTH) ** -0.25
LN_EPS = 1e-5

kernel_name = 'hybrid_dsa_diffattn_stream_encoder_step'

F32 = jnp.float32


def alibi_slopes(n):
    return jnp.asarray(2.0 ** (-8.0 * np.arange(1, n + 1) / n), dtype=F32)


def chunk_visible(qpos, kpos):
    return (kpos // CHUNK) <= (qpos // CHUNK)


def to_blocks(a, qb):
    B, T = a.shape[:2]
    return jnp.moveaxis(a.reshape(B, T // qb, qb, *a.shape[2:]), 1, 0)


def from_blocks(a):
    nb, B, qb = a.shape[:3]
    return jnp.moveaxis(a, 0, 1).reshape(B, nb * qb, *a.shape[3:])


def dsa_attend(q, qi, wi, qpos, k_all, v_all, ki_all, kpos, topk):
    B, Tq = q.shape[:2]
    R = H_A // KV_A
    rel = jnp.einsum('bqhd,bsd->bqhs', qi, ki_all, preferred_element_type=F32)
    score = jnp.einsum('bqhs,bqh->bqs', jax.nn.relu(rel), wi.astype(F32) * (H_IDX ** -0.5))
    vis = chunk_visible(qpos[:, None], kpos[None, :])
    score = jnp.where(vis[None], score, -jnp.inf)
    _, idx = lax.top_k(score, topk)
    gather = jax.vmap(lambda a, i: a[i])
    k_sel = gather(k_all, idx)
    v_sel = gather(v_all, idx)
    pos_sel = kpos[idx]
    qg = q.reshape(B, Tq, KV_A, R, HD)
    logits = jnp.einsum('bqgrd,bqkgd->bqgrk', qg, k_sel, preferred_element_type=F32) * (HD ** -0.5)
    dist = jnp.abs(qpos[None, :, None] - pos_sel).astype(F32)
    slopes = alibi_slopes(H_A).reshape(KV_A, R)
    logits = logits - slopes[None, None, :, :, None] * dist[:, :, None, None, :]
    valid = chunk_visible(qpos[None, :, None], pos_sel)[:, :, None, None, :]
    p = jax.nn.softmax(jnp.where(valid, logits, -jnp.inf), axis=-1)
    o = jnp.einsum('bqgrk,bqkgd->bqgrd', p.astype(v_sel.dtype), v_sel)
    return o.reshape(B, Tq, D_A)


def diff_attend(q, qpos, k_all, v_all, kpos, lam, lam_init, subln_g):
    B, Tq = q.shape[:2]
    logits = jnp.einsum('bqhmd,bshmd->bhmqs', q, k_all, preferred_element_type=F32) * (HD ** -0.5)
    dist = jnp.abs(qpos[:, None] - kpos[None, :]).astype(F32)
    logits = logits - alibi_slopes(H_B)[None, :, None, None, None] * dist
    vis = chunk_visible(qpos[:, None], kpos[None, :])
    p = jax.nn.softmax(jnp.where(vis, logits, -jnp.inf), axis=-1)
    a = p[:, :, 0] - lam * p[:, :, 1]
    o = jnp.einsum('bhqs,bshe->bqhe', a.astype(v_all.dtype), v_all).astype(F32)
    o = o * lax.rsqrt(jnp.mean(o * o, axis=-1, keepdims=True) + LN_EPS)
    o = o * subln_g.astype(F32) * (1.0 - lam_init)
    return o.reshape(B, Tq, D_B).astype(v_all.dtype)


def encoder_layer(x, c, past, layer_idx, w_ada, b_ada, w_in, w_out,
                  lam_q1, lam_k1, lam_q2, lam_k2, subln_g, ln_g, ln_b):
    B, T, _ = x.shape
    mod = jnp.einsum('bd,de->be', jax.nn.silu(c), w_ada) + b_ada
    shift, scale, gate = jnp.split(mod, 3, axis=-1)
    h = x * (1.0 + scale[:, None, :]) + shift[:, None, :]
    proj = jnp.einsum('btd,de->bte', h, w_in)
    q_a, k_a, v_a, g_a, q_idx, k_idx, w_idx, q_b, k_b, v_b, g_b = jnp.split(proj, SPLIT_POINTS, axis=-1)
    q_a = q_a.reshape(B, T, H_A, HD)
    k_a = k_a.reshape(B, T, KV_A, HD)
    v_a = v_a.reshape(B, T, KV_A, HD)
    q_idx = q_idx.reshape(B, T, H_IDX, D_IDX)
    q_b = q_b.reshape(B, T, H_B, 2, HD)
    k_b = k_b.reshape(B, T, H_B, 2, HD)
    v_b = v_b.reshape(B, T, H_B, 2 * HD)
    new_rows = (k_a, v_a, k_idx, k_b, v_b)
    if past is None:
        past_len = 0
        ka_all, va_all, ki_all, kb_all, vb_all = new_rows
    else:
        past_len = past[0].shape[1]
        ka_all, va_all, ki_all, kb_all, vb_all = (
            jnp.concatenate([p, n], axis=1) for p, n in zip(past, new_rows))
    qpos = past_len + jnp.arange(T, dtype=jnp.int32)
    kpos = jnp.arange(past_len + T, dtype=jnp.int32)
    topk = min(TOPK_MAX, (past_len + T) // 4)
    lam_init = 0.8 - 0.6 * math.exp(-0.3 * layer_idx)
    lam = (jnp.exp(jnp.sum(lam_q1.astype(F32) * lam_k1.astype(F32)))
           - jnp.exp(jnp.sum(lam_q2.astype(F32) * lam_k2.astype(F32))) + lam_init)
    qb = min(QBLK, T)

    def block_fn(args):
        qa, qi, wi, qbk, qp = args
        oa = dsa_attend(qa, qi, wi, qp, ka_all, va_all, ki_all, kpos, topk)
        ob = diff_attend(qbk, qp, kb_all, vb_all, kpos, lam, lam_init, subln_g)
        return oa, ob

    oa, ob = lax.map(block_fn, (to_blocks(q_a, qb), to_blocks(q_idx, qb), to_blocks(w_idx, qb),
                                to_blocks(q_b, qb), qpos.reshape(T // qb, qb)))
    o_a = from_blocks(oa)
    o_b = from_blocks(ob)
    mixed = jnp.concatenate([o_a * jax.nn.silu(g_a), o_b * jax.nn.silu(g_b)], axis=-1)
    y = jnp.einsum('bte,ed->btd', mixed, w_out)
    r = (ALPHA * x + gate[:, None, :] * y).astype(F32)
    mu = jnp.mean(r, axis=-1, keepdims=True)
    var = jnp.mean(jnp.square(r - mu), axis=-1, keepdims=True)
    out = (r - mu) * lax.rsqrt(var + LN_EPS) * ln_g.astype(F32) + ln_b.astype(F32)
    return out.astype(x.dtype), new_rows


def setup_inputs(seed: int = 0) -> dict:
    key = jax.random.key(seed)
    ks = jax.random.split(key, 20)
    nrm = jax.random.normal
    return {
        'x_prompt': nrm(ks[0], (BATCH, SEQ, D_MODEL), F32),
        'x_sample': nrm(ks[1], (DEC_BATCH, DEC_SEQ, D_MODEL), F32),
        'cache_a_k': nrm(ks[2], (DEPTH, DEC_BATCH, PAST_LEN, KV_A, HD), F32),
        'cache_a_v': nrm(ks[3], (DEPTH, DEC_BATCH, PAST_LEN, KV_A, HD), F32),
        'cache_a_kidx': nrm(ks[4], (DEPTH, DEC_BATCH, PAST_LEN, D_IDX), F32),
        'cache_b_k': nrm(ks[5], (DEPTH, DEC_BATCH, PAST_LEN, H_B, 2, HD), F32),
        'cache_b_v': nrm(ks[6], (DEPTH, DEC_BATCH, PAST_LEN, H_B, 2 * HD), F32),
        'c_prompt': nrm(ks[7], (BATCH, D_MODEL), F32),
        'c_sample': nrm(ks[8], (DEC_BATCH, D_MODEL), F32),
        'w_ada': nrm(ks[9], (DEPTH, D_MODEL, 3 * D_MODEL), F32) * D_MODEL ** -0.5,
        'b_ada': nrm(ks[10], (DEPTH, 3 * D_MODEL), F32) * 0.02,
        'w_in': nrm(ks[11], (DEPTH, D_MODEL, N_COLS), F32) * D_MODEL ** -0.5,
        'w_out': nrm(ks[12], (DEPTH, D_MIX, D_MODEL), F32) * (D_MIX ** -0.5) * BETA,
        'lam_q1': nrm(ks[13], (DEPTH, HD), F32) * 0.1,
        'lam_k1': nrm(ks[14], (DEPTH, HD), F32) * 0.1,
        'lam_q2': nrm(ks[15], (DEPTH, HD), F32) * 0.1,
        'lam_k2': nrm(ks[16], (DEPTH, HD), F32) * 0.1,
        'subln_g': 1.0 + 0.02 * nrm(ks[17], (DEPTH, 2 * HD), F32),
        'ln_g': 1.0 + 0.02 * nrm(ks[18], (DEPTH, D_MODEL), F32),
        'ln_b': 0.02 * nrm(ks[19], (DEPTH, D_MODEL), F32),
    }


def reference(x_prompt, x_sample, cache_a_k, cache_a_v, cache_a_kidx, cache_b_k, cache_b_v,
              c_prompt, c_sample, w_ada, b_ada, w_in, w_out, lam_q1, lam_k1, lam_q2, lam_k2,
              subln_g, ln_g, ln_b):
    xp, xs = x_prompt, x_sample
    news_p, news_s = [], []
    for l in range(DEPTH):
        params = (w_ada[l], b_ada[l], w_in[l], w_out[l], lam_q1[l], lam_k1[l],
                  lam_q2[l], lam_k2[l], subln_g[l], ln_g[l], ln_b[l])
        xp, rows_p = encoder_layer(xp, c_prompt, None, l, *params)
        past = (cache_a_k[l], cache_a_v[l], cache_a_kidx[l], cache_b_k[l], cache_b_v[l])
        xs, rows_s = encoder_layer(xs, c_sample, past, l, *params)
        news_p.append(rows_p)
        news_s.append(rows_s)
    st = lambda rows, i: jnp.stack([r[i] for r in rows], axis=0)
    return (xp, xs,
            st(news_p, 0), st(news_p, 1), st(news_p, 2), st(news_p, 3), st(news_p, 4),
            st(news_s, 0), st(news_s, 1), st(news_s, 2), st(news_s, 3), st(news_s, 4))
```

```python
import functools
import math

import jax
import jax.numpy as jnp
import numpy as np
from jax import lax
from jax.experimental import pallas as pl
from jax.experimental.pallas import tpu as pltpu

F32 = jnp.float32
BF16 = jnp.bfloat16

HD = 128
CHUNK = 64
CHUNK_SHIFT = 6
KV_A = 2
H_IDX = 16
D_IDX = 64
TOPK_MAX = 256
LN_EPS = 1e-5
LAM_INIT = 0.8 - 0.6 * math.exp(-0.3 * 0)

NEG = -1e30
N_BISECT = 16

VMEM_LIMIT = 56 * 1024 * 1024


def _cparams(sem):
    return pltpu.CompilerParams(dimension_semantics=sem, vmem_limit_bytes=VMEM_LIMIT)


def _silu(x):
    return x * jax.nn.sigmoid(x)


def _dot_nt(a, b):
    return lax.dot_general(a, b, (((1,), (1,)), ((), ())), preferred_element_type=F32)


def _alibi_slopes(n):
    return jnp.asarray(2.0 ** (-8.0 * np.arange(1, n + 1) / n), dtype=F32)


def _ada_kernel(c_ref, w_ref, b_ref, o_ref):
    s = _silu(c_ref[...]).astype(BF16)
    o_ref[...] = jnp.dot(s, w_ref[...].astype(BF16), preferred_element_type=F32) + b_ref[...]


def _ada(c, w_ada, b_ada, tn=512):
    n, d = c.shape
    e = w_ada.shape[1]
    return pl.pallas_call(
        _ada_kernel,
        out_shape=jax.ShapeDtypeStruct((n, e), F32),
        grid=(e // tn,),
        in_specs=[pl.BlockSpec((n, d), lambda j: (0, 0)),
                  pl.BlockSpec((d, tn), lambda j: (0, j)),
                  pl.BlockSpec((1, tn), lambda j: (0, j))],
        out_specs=pl.BlockSpec((n, tn), lambda j: (0, j)),
        compiler_params=_cparams(("arbitrary",)),
        name="ada",
    )(c, w_ada, b_ada.reshape(1, e))


def _modulate_kernel(x_ref, shift_ref, scale_ref, o_ref):
    o_ref[0] = (x_ref[0] * (1.0 + scale_ref[0]) + shift_ref[0]).astype(o_ref.dtype)


def _modulate(x, shift, scale, tt):
    b, t, d = x.shape
    return pl.pallas_call(
        _modulate_kernel,
        out_shape=jax.ShapeDtypeStruct((b, t, d), BF16),
        grid=(b, t // tt),
        in_specs=[pl.BlockSpec((1, tt, d), lambda i, j: (i, j, 0)),
                  pl.BlockSpec((1, 1, d), lambda i, j: (i, 0, 0)),
                  pl.BlockSpec((1, 1, d), lambda i, j: (i, 0, 0))],
        out_specs=pl.BlockSpec((1, tt, d), lambda i, j: (i, j, 0)),
        compiler_params=_cparams(("arbitrary", "arbitrary")),
        name="modulate",
    )(x, shift, scale)


def _proj_kernel(h_ref, w_ref, *o_refs, outs):
    acc = jnp.dot(h_ref[...], w_ref[...], preferred_element_type=F32)
    for o_ref, (kind, lo, hi, width, scale) in zip(o_refs, outs):
        if kind == "flat":
            o_ref[...] = acc[:, lo:hi].astype(o_ref.dtype)
        else:
            for hh in range((hi - lo) // width):
                blk = acc[:, lo + hh * width: lo + (hh + 1) * width]
                if scale != 1.0:
                    blk = blk * scale
                o_ref[hh] = blk.astype(o_ref.dtype)


def _proj(h2d, w_pad, col_start, ncols, tn, tm, outs):
    m, k = h2d.shape
    nj = ncols // tn
    assert col_start % tn == 0 and ncols % tn == 0 and m % tm == 0
    j0 = col_start // tn
    out_shapes, out_specs, kouts = [], [], []
    for kind, lo, hi, width, scale, dtype in outs:
        if kind == "flat":
            out_shapes.append(jax.ShapeDtypeStruct((m, nj * (hi - lo)), dtype))
            out_specs.append(pl.BlockSpec((tm, hi - lo), lambda i, j: (i, j)))
        else:
            nh = (hi - lo) // width
            out_shapes.append(jax.ShapeDtypeStruct((nj * nh, m, width), dtype))
            out_specs.append(pl.BlockSpec((nh, tm, width), lambda i, j: (j, i, 0)))
        kouts.append((kind, lo, hi, width, scale))
    return pl.pallas_call(
        functools.partial(_proj_kernel, outs=tuple(kouts)),
        out_shape=out_shapes,
        grid=(m // tm, nj),
        in_specs=[pl.BlockSpec((tm, k), lambda i, j: (i, 0)),
                  pl.BlockSpec((k, tn), lambda i, j: (0, j0 + j))],
        out_specs=out_specs,
        compiler_params=_cparams(("arbitrary", "arbitrary")),
        name=f"proj_c{col_start}",
    )(h2d, w_pad)


def _kth_largest(load, ntiles, k, axis, shape):
    neg_inf = jnp.full(shape, -jnp.inf, F32)

    def reduce_tiles(fn, init):
        return lax.fori_loop(0, ntiles, lambda i, c: fn(load(i), c), init)

    def count_ge(t):
        return reduce_tiles(
            lambda x, c: c + jnp.sum(jnp.where(x >= t, 1.0, 0.0), axis=axis, keepdims=True),
            jnp.zeros(shape, F32))

    def max_below(t, strict):
        def fn(x, c):
            keep = (x < t) if strict else (x <= t)
            return jnp.maximum(c, jnp.max(jnp.where(keep, x, -jnp.inf), axis=axis, keepdims=True))
        return reduce_tiles(fn, neg_inf)

    def minmax(x, c):
        mn, mx = c
        mn = jnp.minimum(mn, jnp.min(jnp.where(x > -jnp.inf, x, jnp.inf), axis=axis, keepdims=True))
        mx = jnp.maximum(mx, jnp.max(x, axis=axis, keepdims=True))
        return mn, mx

    lo, hi = reduce_tiles(minmax, (jnp.full(shape, jnp.inf, F32), neg_inf))
    kf = float(k)

    def bisect(_, c):
        lo, hi = c
        mid = 0.5 * lo + 0.5 * hi
        ok = count_ge(mid) >= kf
        return jnp.where(ok, mid, lo), jnp.where(ok, hi, mid)

    lo, hi = lax.fori_loop(0, N_BISECT, bisect, (lo, hi))
    v = max_below(hi, strict=False)
    c = count_ge(v)

    def walk(vc):
        v, c = vc
        v = jnp.where(c < kf, max_below(v, strict=True), v)
        return v, count_ge(v)

    v, _ = lax.while_loop(lambda vc: jnp.min(vc[1]) < kf, walk, (v, c))
    return v


def _attn_a_kernel(slopes_ref, qa_ref, k_ref, v_ref, kw_ref, kwq_ref, qidx_ref, ga_ref, o_ref,
                   k16, v16, ki16, st_scr, mb_scr, o_scr, thr_scr, *, topk, n_heads):
    qi = pl.program_id(1)
    tq = qa_ref.shape[2]
    tk = tq
    rep = n_heads // KV_A
    nk = qi + 1

    @pl.when(qi == 0)
    def _():
        for g in range(KV_A):
            k16[g] = k_ref[0, :, g * HD:(g + 1) * HD].astype(BF16)
            v16[g] = v_ref[0, :, g * HD:(g + 1) * HD].astype(BF16)
        ki16[...] = kw_ref[0, :, :D_IDX].astype(BF16)

    w_t = jnp.transpose(kwq_ref[0])
    w_sc = w_t[D_IDX:D_IDX + H_IDX, :] * (H_IDX ** -0.5)

    def score_tile(kj, c):
        ki = ki16[pl.ds(pl.multiple_of(kj * tk, tk), tk), :]
        acc = jnp.zeros((tk, tq), F32)
        for h in range(H_IDX):
            acc = acc + jnp.maximum(_dot_nt(ki, qidx_ref[h, 0]), 0.0) * w_sc[h:h + 1, :]
        st_scr[kj] = acc
        return c

    lax.fori_loop(0, nk, score_tile, 0)
    krow = lax.broadcasted_iota(jnp.int32, (tk, tq), 0)
    qcol = lax.broadcasted_iota(jnp.int32, (tk, tq), 1)
    st_scr[qi] = jnp.where((krow >> CHUNK_SHIFT) <= (qcol >> CHUNK_SHIFT), st_scr[qi], -jnp.inf)

    thr_scr[...] = jnp.full((1, tq), -3e38, F32)

    @pl.when(qi * tq + CHUNK >= topk)
    def _():
        thr_scr[...] = _kth_largest(lambda i: st_scr[i], nk, topk, 0, (1, tq))

    thr = thr_scr[...]

    def mask_tile(kj, c):
        mb_scr[kj] = jnp.transpose(jnp.where(st_scr[kj] >= thr, 0.0, NEG))
        return c

    lax.fori_loop(0, nk, mask_tile, 0)

    rc = (lax.broadcasted_iota(jnp.int32, (tq, tk), 0)
          - lax.broadcasted_iota(jnp.int32, (tq, tk), 1)).astype(F32)

    def head(h, c):
        g = h // rep
        slope = slopes_ref[h]
        q = qa_ref[h, 0]

        def kv_step(kj, carry):
            m, l, acc = carry
            rows = pl.ds(pl.multiple_of(kj * tk, tk), tk)
            s = _dot_nt(q, k16[g, rows, :])
            off = ((qi - kj) * tk).astype(F32)
            s = s - slope * jnp.abs(rc + off) + mb_scr[kj]
            m_new = jnp.maximum(m, jnp.max(s, axis=1, keepdims=True))
            alpha = jnp.exp(m - m_new)
            p = jnp.exp(s - m_new)
            l = alpha * l + jnp.sum(p, axis=1, keepdims=True)
            acc = alpha * acc + jnp.dot(p.astype(BF16), v16[g, rows, :], preferred_element_type=F32)
            return m_new, l, acc

        m, l, acc = lax.fori_loop(
            0, nk, kv_step,
            (jnp.full((tq, 1), NEG, F32), jnp.zeros((tq, 1), F32), jnp.zeros((tq, HD), F32)))
        o_scr[h] = acc / l
        return c

    lax.fori_loop(0, n_heads, head, 0)

    for h in range(n_heads):
        cols = slice(h * HD, (h + 1) * HD)
        o_ref[0, :, cols] = (o_scr[h] * _silu(ga_ref[0, :, cols].astype(F32))).astype(o_ref.dtype)


def _attn_a_prompt(qa, k, v, kw, qidx, ga, topk):
    n_heads, b, t, _ = qa.shape
    tq = min(256, topk)
    assert t % tq == 0 and tq % CHUNK == 0 and tq <= topk <= tq + CHUNK
    nkt = t // tq
    d_a = n_heads * HD
    return pl.pallas_call(
        functools.partial(_attn_a_kernel, topk=topk, n_heads=n_heads),
        out_shape=jax.ShapeDtypeStruct((b, t, d_a), BF16),
        grid=(b, nkt),
        in_specs=[pl.BlockSpec(memory_space=pltpu.SMEM),
                  pl.BlockSpec((n_heads, 1, tq, HD), lambda i, j: (0, i, j, 0)),
                  pl.BlockSpec((1, t, KV_A * HD), lambda i, j: (i, 0, 0)),
                  pl.BlockSpec((1, t, KV_A * HD), lambda i, j: (i, 0, 0)),
                  pl.BlockSpec((1, t, 128), lambda i, j: (i, 0, 0)),
                  pl.BlockSpec((1, tq, 128), lambda i, j: (i, j, 0)),
                  pl.BlockSpec((H_IDX, 1, tq, D_IDX), lambda i, j: (0, i, j, 0)),
                  pl.BlockSpec((1, tq, d_a), lambda i, j: (i, j, 0))],
        out_specs=pl.BlockSpec((1, tq, d_a), lambda i, j: (i, j, 0)),
        scratch_shapes=[pltpu.VMEM((KV_A, t, HD), BF16),
                        pltpu.VMEM((KV_A, t, HD), BF16),
                        pltpu.VMEM((t, D_IDX), BF16),
                        pltpu.VMEM((nkt, tq, tq), F32),
                        pltpu.VMEM((nkt, tq, tq), F32),
                        pltpu.VMEM((n_heads, tq, HD), F32),
                        pltpu.VMEM((1, tq), F32)],
        compiler_params=_cparams(("arbitrary", "arbitrary")),
        name="attn_a_prompt",
    )(_alibi_slopes(n_heads), qa, k, v, kw, kw, qidx, ga)


def _lam(lamp_ref):
    s1 = jnp.sum(lamp_ref[0:1, :] * lamp_ref[1:2, :], axis=1, keepdims=True)
    s2 = jnp.sum(lamp_ref[2:3, :] * lamp_ref[3:4, :], axis=1, keepdims=True)
    return jnp.exp(s1) - jnp.exp(s2) + LAM_INIT


def _subln_gate(o, subln, g):
    o = o * lax.rsqrt(jnp.mean(o * o, axis=-1, keepdims=True) + LN_EPS)
    o = o * subln * (1.0 - LAM_INIT)
    return o * _silu(g.astype(F32))


def _attn_b_kernel(slopes_ref, lamp_ref, subln_ref, q_ref, k_ref, v_ref, g_ref, o_ref):
    h = pl.program_id(1)
    qi = pl.program_id(2)
    tq = q_ref.shape[2]
    tk = tq
    slope = slopes_ref[h]
    rc = (lax.broadcasted_iota(jnp.int32, (tq, tk), 0)
          - lax.broadcasted_iota(jnp.int32, (tq, tk), 1)).astype(F32)
    qrow = lax.broadcasted_iota(jnp.int32, (tq, tk), 0)
    kcol = lax.broadcasted_iota(jnp.int32, (tq, tk), 1)
    diag_bias = jnp.where((kcol >> CHUNK_SHIFT) <= (qrow >> CHUNK_SHIFT), -slope * jnp.abs(rc), NEG)

    def one_map(mi):
        q = q_ref[mi, 0]

        def update(carry, kj, bias):
            m, l, acc = carry
            rows = pl.ds(pl.multiple_of(kj * tk, tk), tk)
            s = _dot_nt(q, k_ref[mi, 0, rows, :]) + bias
            m_new = jnp.maximum(m, jnp.max(s, axis=1, keepdims=True))
            alpha = jnp.exp(m - m_new)
            p = jnp.exp(s - m_new)
            l = alpha * l + jnp.sum(p, axis=1, keepdims=True)
            acc = alpha * acc + jnp.dot(p.astype(BF16), v_ref[0, 0, rows, :], preferred_element_type=F32)
            return m_new, l, acc

        def off_diag(kj, carry):
            off = ((qi - kj) * tk).astype(F32)
            return update(carry, kj, -slope * (rc + off))

        carry = (jnp.full((tq, 1), NEG, F32), jnp.zeros((tq, 1), F32), jnp.zeros((tq, 2 * HD), F32))
        carry = lax.fori_loop(0, qi, off_diag, carry)
        m, l, acc = update(carry, qi, diag_bias)
        return acc / l

    o = one_map(0) - _lam(lamp_ref) * one_map(1)
    o_ref[0] = _subln_gate(o, subln_ref[...], g_ref[0]).astype(o_ref.dtype)


def _attn_b_prompt(qb, kb16, vb16, gb, lamp, subln, tq=256):
    _, b, t, _ = qb.shape
    n_heads = vb16.shape[0]
    assert t % tq == 0 and tq % CHUNK == 0
    return pl.pallas_call(
        _attn_b_kernel,
        out_shape=jax.ShapeDtypeStruct((b, t, n_heads * 2 * HD), BF16),
        grid=(b, n_heads, t // tq),
        in_specs=[pl.BlockSpec(memory_space=pltpu.SMEM),
                  pl.BlockSpec((4, HD), lambda i, h, j: (0, 0)),
                  pl.BlockSpec((1, 2 * HD), lambda i, h, j: (0, 0)),
                  pl.BlockSpec((2, 1, tq, HD), lambda i, h, j: (h, i, j, 0)),
                  pl.BlockSpec((2, 1, t, HD), lambda i, h, j: (h, i, 0, 0)),
                  pl.BlockSpec((1, 1, t, 2 * HD), lambda i, h, j: (h, i, 0, 0)),
                  pl.BlockSpec((1, tq, 2 * HD), lambda i, h, j: (i, j, h))],
        out_specs=pl.BlockSpec((1, tq, 2 * HD), lambda i, h, j: (i, j, h)),
        compiler_params=_cparams(("arbitrary", "arbitrary", "arbitrary")),
        name="attn_b_prompt",
    )(_alibi_slopes(n_heads), lamp, subln, qb, kb16, vb16, gb)


def _attn_a_sample_kernel(slopes_ref, qa_ref, kc_ref, vc_ref, kic_ref, kn_ref, vn_ref, kwn_ref, qidx_ref,
                          ga_ref, o_ref, k16, v16, sc_scr, s_scr, p_scr, *, topk, n_heads, cw):
    t = qa_ref.shape[2]
    past = kc_ref.shape[1]
    s_pad = sc_scr.shape[1]
    rep = n_heads // KV_A
    kcol = lax.broadcasted_iota(jnp.int32, (t, s_pad), 1)
    qrow = lax.broadcasted_iota(jnp.int32, (t, s_pad), 0)

    qi_all = qidx_ref[:, 0].reshape(H_IDX * t, D_IDX)
    w_q = kwn_ref[0, :, D_IDX:D_IDX + H_IDX] * (H_IDX ** -0.5)

    def scores(ki):
        r = jnp.maximum(_dot_nt(qi_all, ki), 0.0)
        acc = jnp.zeros((t, ki.shape[0]), F32)
        for h in range(H_IDX):
            acc = acc + r[h * t:(h + 1) * t, :] * w_q[:, h:h + 1]
        return acc

    for c0 in range(0, past, cw):
        sc_scr[:, c0:c0 + cw] = scores(kic_ref[0, c0:c0 + cw, :].astype(BF16))
    tail = s_pad - past
    ki_new = jnp.concatenate(
        [kwn_ref[0, :, :D_IDX], jnp.zeros((tail - t, D_IDX), F32)], axis=0).astype(BF16)
    tail_valid = lax.broadcasted_iota(jnp.int32, (t, tail), 1) < t
    sc_scr[:, past:] = jnp.where(tail_valid, scores(ki_new), -jnp.inf)

    thr = _kth_largest(lambda i: sc_scr[...], 1, topk, 1, (t, 1))
    mbias = jnp.where(sc_scr[...] >= thr, 0.0, NEG)
    dist = jnp.abs((past + qrow - kcol).astype(F32))

    for g in range(KV_A):
        cols = slice(g * HD, (g + 1) * HD)
        k16[:past] = kc_ref[0, :, cols].astype(BF16)
        v16[:past] = vc_ref[0, :, cols].astype(BF16)
        zpad = jnp.zeros((tail - t, HD), F32)
        k16[past:] = jnp.concatenate([kn_ref[0, :, cols], zpad], axis=0).astype(BF16)
        v16[past:] = jnp.concatenate([vn_ref[0, :, cols], zpad], axis=0).astype(BF16)
        q_g = qa_ref[g * rep:(g + 1) * rep, 0].reshape(rep * t, HD)
        s_scr[...] = _dot_nt(q_g, k16[...])
        inv_l = []
        for hh in range(rep):
            rows = slice(hh * t, (hh + 1) * t)
            s = s_scr[rows, :] - slopes_ref[g * rep + hh] * dist + mbias
            p = jnp.exp(s - jnp.max(s, axis=1, keepdims=True))
            inv_l.append(1.0 / jnp.sum(p, axis=1, keepdims=True))
            p_scr[rows, :] = p.astype(BF16)
        o_g = jnp.dot(p_scr[...], v16[...], preferred_element_type=F32)
        for hh in range(rep):
            hc = slice((g * rep + hh) * HD, (g * rep + hh + 1) * HD)
            o = o_g[hh * t:(hh + 1) * t, :] * inv_l[hh]
            o_ref[0, :, hc] = (o * _silu(ga_ref[0, :, hc].astype(F32))).astype(o_ref.dtype)


def _attn_a_sample(qa, kc, vc, kic, kn, vn, kwn, qidx, ga, topk, cw=512):
    n_heads, b, t, _ = qa.shape
    past = kc.shape[1]
    s_pad = past + 128
    assert t <= 128 and past % cw == 0
    d_a = n_heads * HD
    return pl.pallas_call(
        functools.partial(_attn_a_sample_kernel, topk=topk, n_heads=n_heads, cw=cw),
        out_shape=jax.ShapeDtypeStruct((b, t, d_a), BF16),
        grid=(b,),
        in_specs=[pl.BlockSpec(memory_space=pltpu.SMEM),
                  pl.BlockSpec((n_heads, 1, t, HD), lambda i: (0, i, 0, 0)),
                  pl.BlockSpec((1, past, KV_A * HD), lambda i: (i, 0, 0)),
                  pl.BlockSpec((1, past, KV_A * HD), lambda i: (i, 0, 0)),
                  pl.BlockSpec((1, past, D_IDX), lambda i: (i, 0, 0)),
                  pl.BlockSpec((1, t, KV_A * HD), lambda i: (i, 0, 0)),
                  pl.BlockSpec((1, t, KV_A * HD), lambda i: (i, 0, 0)),
                  pl.BlockSpec((1, t, 128), lambda i: (i, 0, 0)),
                  pl.BlockSpec((H_IDX, 1, t, D_IDX), lambda i: (0, i, 0, 0)),
                  pl.BlockSpec((1, t, d_a), lambda i: (i, 0, 0))],
        out_specs=pl.BlockSpec((1, t, d_a), lambda i: (i, 0, 0)),
        scratch_shapes=[pltpu.VMEM((s_pad, HD), BF16),
                        pltpu.VMEM((s_pad, HD), BF16),
                        pltpu.VMEM((t, s_pad), F32),
                        pltpu.VMEM((n_heads // KV_A * t, s_pad), F32),
                        pltpu.VMEM((n_heads // KV_A * t, s_pad), BF16)],
        compiler_params=_cparams(("arbitrary",)),
        name="attn_a_sample",
    )(_alibi_slopes(n_heads), qa, kc, vc, kic, kn, vn, kwn, qidx, ga)


def _attn_b_sample_kernel(slopes_ref, lamp_ref, subln_ref, q_ref, kc_ref, vc_ref, kn_ref, vn_ref, g_ref,
                          o_ref, k16, v16):
    h = pl.program_id(1)
    t = q_ref.shape[2]
    past = kc_ref.shape[1]
    s_pad = k16.shape[0]
    tail = s_pad - past
    k16[:past] = kc_ref[0].astype(BF16)
    v16[:past] = vc_ref[0].astype(BF16)
    zpad = jnp.zeros((tail - t, 2 * HD), F32)
    k16[past:] = jnp.concatenate([kn_ref[0], zpad], axis=0).astype(BF16)
    v16[past:] = jnp.concatenate([vn_ref[0], zpad], axis=0).astype(BF16)

    kcol = lax.broadcasted_iota(jnp.int32, (t, s_pad), 1)
    qrow = lax.broadcasted_iota(jnp.int32, (t, s_pad), 0)
    dist = jnp.abs((past + qrow - kcol).astype(F32))
    bias = jnp.where(kcol < past + t, -slopes_ref[h] * dist, NEG)

    z = jnp.zeros((t, HD), BF16)
    q_bd = jnp.concatenate([jnp.concatenate([q_ref[0, 0], z], axis=1),
                            jnp.concatenate([z, q_ref[1, 0]], axis=1)], axis=0)
    s = _dot_nt(q_bd, k16[...])

    def softmax(x):
        x = x + bias
        p = jnp.exp(x - jnp.max(x, axis=1, keepdims=True))
        return p / jnp.sum(p, axis=1, keepdims=True)

    a = softmax(s[:t]) - _lam(lamp_ref) * softmax(s[t:])
    o = jnp.dot(a.astype(BF16), v16[...], preferred_element_type=F32)
    o_ref[0] = _subln_gate(o, subln_ref[...], g_ref[0]).astype(o_ref.dtype)


def _attn_b_sample(qb, kc, vc, kn, vn, gb, lamp, subln):
    _, b, t, _ = qb.shape
    past = kc.shape[1]
    n_heads = kc.shape[2] // (2 * HD)
    s_pad = past + 128
    assert t <= 128
    return pl.pallas_call(
        _attn_b_sample_kernel,
        out_shape=jax.ShapeDtypeStruct((b, t, n_heads * 2 * HD), BF16),
        grid=(b, n_heads),
        in_specs=[pl.BlockSpec(memory_space=pltpu.SMEM),
                  pl.BlockSpec((4, HD), lambda i, h: (0, 0)),
                  pl.BlockSpec((1, 2 * HD), lambda i, h: (0, 0)),
                  pl.BlockSpec((2, 1, t, HD), lambda i, h: (h, i, 0, 0)),
                  pl.BlockSpec((1, past, 2 * HD), lambda i, h: (i, 0, h)),
                  pl.BlockSpec((1, past, 2 * HD), lambda i, h: (i, 0, h)),
                  pl.BlockSpec((1, t, 2 * HD), lambda i, h: (i, 0, h)),
                  pl.BlockSpec((1, t, 2 * HD), lambda i, h: (i, 0, h)),
                  pl.BlockSpec((1, t, 2 * HD), lambda i, h: (i, 0, h))],
        out_specs=pl.BlockSpec((1, t, 2 * HD), lambda i, h: (i, 0, h)),
        scratch_shapes=[pltpu.VMEM((s_pad, 2 * HD), BF16),
                        pltpu.VMEM((s_pad, 2 * HD), BF16)],
        compiler_params=_cparams(("arbitrary", "arbitrary")),
        name="attn_b_sample",
    )(_alibi_slopes(n_heads), lamp, subln, qb, kc, vc, kn, vn, gb)


def _out_kernel(ma_ref, mb_ref, wa_ref, wb_ref, x_ref, gate_ref, lng_ref, lnb_ref, o_ref, r_scr, *, alpha):
    j = pl.program_id(1)
    nj = pl.num_programs(1)
    tn = x_ref.shape[1]
    y = jnp.dot(ma_ref[...], wa_ref[...], preferred_element_type=F32)
    y = y + jnp.dot(mb_ref[...], wb_ref[...], preferred_element_type=F32)
    r_scr[j] = alpha * x_ref[...] + gate_ref[...] * y

    @pl.when(j == nj - 1)
    def _():
        n_tiles = r_scr.shape[0]
        d = n_tiles * tn
        tot = jnp.zeros((r_scr.shape[1], 1), F32)
        for c in range(n_tiles):
            tot = tot + jnp.sum(r_scr[c], axis=1, keepdims=True)
        mu = tot / d
        sq = jnp.zeros_like(tot)
        for c in range(n_tiles):
            dv = r_scr[c] - mu
            sq = sq + jnp.sum(dv * dv, axis=1, keepdims=True)
        inv = lax.rsqrt(sq / d + LN_EPS)
        for c in range(n_tiles):
            cols = slice(c * tn, (c + 1) * tn)
            o_ref[:, cols] = (r_scr[c] - mu) * inv * lng_ref[:, cols] + lnb_ref[:, cols]


def _out_proj(ma, mb, wa, wb, x2d, gate, ln_g, ln_b, alpha, tm, tn, rows_per_gate):
    m, da = ma.shape
    db = mb.shape[1]
    d = wa.shape[1]
    assert m % tm == 0 and d % tn == 0
    if rows_per_gate == 1:
        gate_spec = pl.BlockSpec((tm, tn), lambda i, j: (i, j))
    else:
        assert rows_per_gate % tm == 0
        gate = gate.reshape(-1, 1, d)
        gate_spec = pl.BlockSpec((None, 1, tn), lambda i, j: (i * tm // rows_per_gate, 0, j))
    return pl.pallas_call(
        functools.partial(_out_kernel, alpha=alpha),
        out_shape=jax.ShapeDtypeStruct((m, d), F32),
        grid=(m // tm, d // tn),
        in_specs=[pl.BlockSpec((tm, da), lambda i, j: (i, 0)),
                  pl.BlockSpec((tm, db), lambda i, j: (i, 0)),
                  pl.BlockSpec((da, tn), lambda i, j: (0, j)),
                  pl.BlockSpec((db, tn), lambda i, j: (0, j)),
                  pl.BlockSpec((tm, tn), lambda i, j: (i, j)),
                  gate_spec,
                  pl.BlockSpec((1, d), lambda i, j: (0, 0)),
                  pl.BlockSpec((1, d), lambda i, j: (0, 0))],
        out_specs=pl.BlockSpec((tm, d), lambda i, j: (i, 0)),
        scratch_shapes=[pltpu.VMEM((d // tn, tm, tn), F32)],
        compiler_params=_cparams(("arbitrary", "arbitrary")),
        name="out_proj",
    )(ma, mb, wa, wb, x2d, gate, ln_g.reshape(1, d), ln_b.reshape(1, d))


def _pad_w_in(w_in, d_a, d_b, n_b):
    sizes = (d_a, KV_A * HD, KV_A * HD, d_a, H_IDX * D_IDX, D_IDX, H_IDX, n_b, n_b, n_b, d_b)
    q_a, k_a, v_a, g_a, q_idx, k_idx, w_idx, q_b, k_b, v_b, g_b = jnp.split(
        w_in, np.cumsum(sizes)[:-1].tolist(), axis=1)
    small = 2 * KV_A * HD + D_IDX + H_IDX
    pad = jnp.zeros((w_in.shape[0], 1024 - small), w_in.dtype)
    return jnp.concatenate([k_a, v_a, k_idx, w_idx, pad, q_a, g_a, q_idx, q_b, k_b, v_b, g_b],
                           axis=1).astype(BF16)


def _project(h2d, w_pad, d_a, d_b, tm, want_b16):
    kvw = 2 * KV_A * HD + 128
    k_a, v_a, kw = _proj(h2d, w_pad, 0, kvw, kvw, tm, [
        ("flat", 0, KV_A * HD, 0, 1.0, F32),
        ("flat", KV_A * HD, 2 * KV_A * HD, 0, 1.0, F32),
        ("flat", 2 * KV_A * HD, kvw, 0, 1.0, F32)])
    c = 1024
    tn = 512
    (q_a,) = _proj(h2d, w_pad, c, d_a, tn, tm, [("heads", 0, tn, HD, HD ** -0.5, BF16)])
    c += d_a
    (g_a,) = _proj(h2d, w_pad, c, d_a, tn, tm, [("flat", 0, tn, 0, 1.0, BF16)])
    c += d_a
    (q_idx,) = _proj(h2d, w_pad, c, H_IDX * D_IDX, tn, tm, [("heads", 0, tn, D_IDX, 1.0, BF16)])
    c += H_IDX * D_IDX
    (q_b,) = _proj(h2d, w_pad, c, d_b, tn, tm, [("heads", 0, tn, HD, HD ** -0.5, BF16)])
    c += d_b
    kb_outs = [("flat", 0, tn, 0, 1.0, F32)]
    vb_outs = [("flat", 0, tn, 0, 1.0, F32)]
    if want_b16:
        kb_outs.append(("heads", 0, tn, HD, 1.0, BF16))
        vb_outs.append(("heads", 0, tn, 2 * HD, 1.0, BF16))
    k_b = _proj(h2d, w_pad, c, d_b, tn, tm, kb_outs)
    c += d_b
    v_b = _proj(h2d, w_pad, c, d_b, tn, tm, vb_outs)
    c += d_b
    (g_b,) = _proj(h2d, w_pad, c, d_b, tn, tm, [("flat", 0, tn, 0, 1.0, BF16)])
    return dict(k_a=k_a, v_a=v_a, kw=kw, q_a=q_a, g_a=g_a, q_idx=q_idx, q_b=q_b, k_b=k_b, v_b=v_b, g_b=g_b)


def _layer(x, mod, past, w_pad, w_out_a, w_out_b, lamp, subln, ln_g, ln_b, alpha):
    b, t, d = x.shape
    d_a = w_out_a.shape[0]
    d_b = w_out_b.shape[0]
    h_a = d_a // HD
    h_b = d_b // (2 * HD)
    m = b * t
    shift, scale, gate = (mod[:, i * d:(i + 1) * d] for i in range(3))
    prompt = past is None

    h = _modulate(x, shift.reshape(b, 1, d), scale.reshape(b, 1, d), tt=min(t, 256))
    p = _project(h.reshape(m, d), w_pad, d_a, d_b, tm=min(m, 1024), want_b16=prompt)

    r3 = lambda a: a.reshape(b, t, a.shape[-1])
    r4 = lambda a: a.reshape(a.shape[0], b, t, a.shape[-1])
    k_a, v_a, kw, g_a, g_b = r3(p["k_a"]), r3(p["v_a"]), r3(p["kw"]), r3(p["g_a"]), r3(p["g_b"])
    q_a, q_idx, q_b = r4(p["q_a"]), r4(p["q_idx"]), r4(p["q_b"])
    k_b, v_b = r3(p["k_b"][0]), r3(p["v_b"][0])

    if prompt:
        topk = min(TOPK_MAX, t // 4)
        mix_a = _attn_a_prompt(q_a, k_a, v_a, kw, q_idx, g_a, topk)
        mix_b = _attn_b_prompt(q_b, r4(p["k_b"][1]), r4(p["v_b"][1]), g_b, lamp, subln)
        tm_out, rows_per_gate, gate_rows = min(m, 512), t, gate
    else:
        kc_a, vc_a, kic, kc_b, vc_b = past
        plen = kc_a.shape[1]
        topk = min(TOPK_MAX, (plen + t) // 4)
        mix_a = _attn_a_sample(q_a, kc_a.reshape(b, plen, KV_A * HD), vc_a.reshape(b, plen, KV_A * HD), kic,
                               k_a, v_a, kw, q_idx, g_a, topk)
        mix_b = _attn_b_sample(q_b, kc_b.reshape(b, plen, d_b), vc_b.reshape(b, plen, d_b), k_b, v_b, g_b,
                               lamp, subln)
        tm_out, rows_per_gate = m, 1
        gate_rows = jnp.broadcast_to(gate[:, None, :], (b, t, d)).reshape(m, d)

    y = _out_proj(mix_a.reshape(m, d_a), mix_b.reshape(m, d_b), w_out_a, w_out_b, x.reshape(m, d),
                  gate_rows, ln_g, ln_b, alpha, tm=tm_out, tn=512, rows_per_gate=rows_per_gate)
    rows = (k_a.reshape(1, b, t, KV_A, HD), v_a.reshape(1, b, t, KV_A, HD), kw[None, :, :, :D_IDX],
            k_b.reshape(1, b, t, h_b, 2, HD), v_b.reshape(1, b, t, h_b, 2 * HD))
    return y.reshape(b, t, d), rows


def kernel(x_prompt, x_sample, cache_a_k, cache_a_v, cache_a_kidx, cache_b_k, cache_b_v, c_prompt, c_sample,
           w_ada, b_ada, w_in, w_out, lam_q1, lam_k1, lam_q2, lam_k2, subln_g, ln_g, ln_b):
    depth, d, _ = w_ada.shape
    assert depth == 1, "single-layer step"
    d_b = cache_b_v.shape[3] * cache_b_v.shape[4]
    d_a = w_out.shape[1] - d_b
    alpha = (2.0 * depth) ** 0.25
    bp = x_prompt.shape[0]

    mod = _ada(jnp.concatenate([c_prompt, c_sample], axis=0), w_ada[0], b_ada[0])
    w_pad = _pad_w_in(w_in[0], d_a, d_b, d_b)
    w_out16 = w_out[0].astype(BF16)
    lamp = jnp.concatenate([lam_q1, lam_k1, lam_q2, lam_k2], axis=0)
    common = (w_pad, w_out16[:d_a], w_out16[d_a:], lamp, subln_g, ln_g[0], ln_b[0], alpha)

    y_p, rows_p = _layer(x_prompt, mod[:bp], None, *common)
    past = (cache_a_k[0], cache_a_v[0], cache_a_kidx[0], cache_b_k[0], cache_b_v[0])
    y_s, rows_s = _layer(x_sample, mod[bp:], past, *common)
    return (y_p, y_s) + rows_p + rows_s
```

```python
import functools
import math

import jax
import jax.numpy as jnp
import numpy as np
from jax import lax
from jax.experimental import pallas as pl
from jax.experimental.pallas import tpu as pltpu

F32 = jnp.float32
BF16 = jnp.bfloat16

HD = 128
CHUNK = 64
CHUNK_SHIFT = 6
KV_A = 2
H_IDX = 16
D_IDX = 64
TOPK_MAX = 256
LN_EPS = 1e-5
LAM_INIT = 0.8 - 0.6 * math.exp(-0.3 * 0)

NEG = -1e30
N_BISECT = 16

VMEM_LIMIT = 56 * 1024 * 1024


def _cparams(sem):
    return pltpu.CompilerParams(dimension_semantics=sem, vmem_limit_bytes=VMEM_LIMIT)


def _silu(x):
    return x * jax.nn.sigmoid(x)


def _dot_nt(a, b):
    return lax.dot_general(a, b, (((1,), (1,)), ((), ())), preferred_element_type=F32)


def _alibi_slopes(n):
    return jnp.asarray(2.0 ** (-8.0 * np.arange(1, n + 1) / n), dtype=F32)


def _ada_kernel(c_ref, w_ref, b_ref, o_ref):
    s = _silu(c_ref[...]).astype(BF16)
    o_ref[...] = jnp.dot(s, w_ref[...].astype(BF16), preferred_element_type=F32) + b_ref[...]


def _ada(c, w_ada, b_ada, tn=512):
    n, d = c.shape
    e = w_ada.shape[1]
    return pl.pallas_call(
        _ada_kernel,
        out_shape=jax.ShapeDtypeStruct((n, e), F32),
        grid=(e // tn,),
        in_specs=[pl.BlockSpec((n, d), lambda j: (0, 0)),
                  pl.BlockSpec((d, tn), lambda j: (0, j)),
                  pl.BlockSpec((1, tn), lambda j: (0, j))],
        out_specs=pl.BlockSpec((n, tn), lambda j: (0, j)),
        compiler_params=_cparams(("arbitrary",)),
        name="ada",
    )(c, w_ada, b_ada.reshape(1, e))


def _modulate_kernel(x_ref, shift_ref, scale_ref, o_ref):
    o_ref[0] = (x_ref[0] * (1.0 + scale_ref[0]) + shift_ref[0]).astype(o_ref.dtype)


def _modulate(x, shift, scale, tt):
    b, t, d = x.shape
    return pl.pallas_call(
        _modulate_kernel,
        out_shape=jax.ShapeDtypeStruct((b, t, d), BF16),
        grid=(b, t // tt),
        in_specs=[pl.BlockSpec((1, tt, d), lambda i, j: (i, j, 0)),
                  pl.BlockSpec((1, 1, d), lambda i, j: (i, 0, 0)),
                  pl.BlockSpec((1, 1, d), lambda i, j: (i, 0, 0))],
        out_specs=pl.BlockSpec((1, tt, d), lambda i, j: (i, j, 0)),
        compiler_params=_cparams(("arbitrary", "arbitrary")),
        name="modulate",
    )(x, shift, scale)


def _proj_kernel(h_ref, w_ref, *o_refs, outs):
    acc = jnp.dot(h_ref[...], w_ref[...], preferred_element_type=F32)
    for o_ref, (kind, lo, hi, width, scale) in zip(o_refs, outs):
        if kind == "flat":
            o_ref[...] = acc[:, lo:hi].astype(o_ref.dtype)
        else:
            for hh in range((hi - lo) // width):
                blk = acc[:, lo + hh * width: lo + (hh + 1) * width]
                if scale != 1.0:
                    blk = blk * scale
                o_ref[hh] = blk.astype(o_ref.dtype)


def _proj(h2d, w_pad, col_start, ncols, tn, tm, outs):
    m, k = h2d.shape
    nj = ncols // tn
    assert col_start % tn == 0 and ncols % tn == 0 and m % tm == 0
    j0 = col_start // tn
    out_shapes, out_specs, kouts = [], [], []
    for kind, lo, hi, width, scale, dtype in outs:
        if kind == "flat":
            out_shapes.append(jax.ShapeDtypeStruct((m, nj * (hi - lo)), dtype))
            out_specs.append(pl.BlockSpec((tm, hi - lo), lambda i, j: (i, j)))
        else:
            nh = (hi - lo) // width
            out_shapes.append(jax.ShapeDtypeStruct((nj * nh, m, width), dtype))
            out_specs.append(pl.BlockSpec((nh, tm, width), lambda i, j: (j, i, 0)))
        kouts.append((kind, lo, hi, width, scale))
    return pl.pallas_call(
        functools.partial(_proj_kernel, outs=tuple(kouts)),
        out_shape=out_shapes,
        grid=(m // tm, nj),
        in_specs=[pl.BlockSpec((tm, k), lambda i, j: (i, 0)),
                  pl.BlockSpec((k, tn), lambda i, j: (0, j0 + j))],
        out_specs=out_specs,
        compiler_params=_cparams(("arbitrary", "arbitrary")),
        name=f"proj_c{col_start}",
    )(h2d, w_pad)


def _kth_largest(load, ntiles, k, axis, shape):
    neg_inf = jnp.full(shape, -jnp.inf, F32)

    def reduce_tiles(fn, init):
        return lax.fori_loop(0, ntiles, lambda i, c: fn(load(i), c), init)

    def count_ge(t):
        return reduce_tiles(
            lambda x, c: c + jnp.sum(jnp.where(x >= t, 1.0, 0.0), axis=axis, keepdims=True),
            jnp.zeros(shape, F32))

    def max_below(t, strict):
        def fn(x, c):
            keep = (x < t) if strict else (x <= t)
            return jnp.maximum(c, jnp.max(jnp.where(keep, x, -jnp.inf), axis=axis, keepdims=True))
        return reduce_tiles(fn, neg_inf)

    def minmax(x, c):
        mn, mx = c
        mn = jnp.minimum(mn, jnp.min(jnp.where(x > -jnp.inf, x, jnp.inf), axis=axis, keepdims=True))
        mx = jnp.maximum(mx, jnp.max(x, axis=axis, keepdims=True))
        return mn, mx

    lo, hi = reduce_tiles(minmax, (jnp.full(shape, jnp.inf, F32), neg_inf))
    kf = float(k)

    def bisect(_, c):
        lo, hi = c
        mid = 0.5 * lo + 0.5 * hi
        ok = count_ge(mid) >= kf
        return jnp.where(ok, mid, lo), jnp.where(ok, hi, mid)

    lo, hi = lax.fori_loop(0, N_BISECT, bisect, (lo, hi))
    v = max_below(hi, strict=False)
    c = count_ge(v)

    def walk(vc):
        v, c = vc
        v = jnp.where(c < kf, max_below(v, strict=True), v)
        return v, count_ge(v)

    v, _ = lax.while_loop(lambda vc: jnp.min(vc[1]) < kf, walk, (v, c))
    return v


def _attn_a_kernel(slopes_ref, qa_ref, k_ref, v_ref, kw_ref, kwq_ref, qidx_ref, ga_ref, o_ref,
                   k16, vt16, ki16, st_scr, mb_scr, acc_scr, thr_scr, *, topk, n_heads):
    qi = pl.program_id(1)
    tq = qa_ref.shape[2]
    tk = tq
    rep = n_heads // KV_A
    nk = qi + 1

    @pl.when(qi == 0)
    def _():
        for g in range(KV_A):
            cols = slice(g * HD, (g + 1) * HD)
            k16[g] = k_ref[0, :, cols].astype(BF16)
            for j in range(vt16.shape[1]):
                vt16[g, j] = jnp.transpose(v_ref[0, j * tk:(j + 1) * tk, cols]).astype(BF16)
        ki16[...] = kw_ref[0, :, :D_IDX].astype(BF16)

    w_t = jnp.transpose(kwq_ref[0])
    w_sc = w_t[D_IDX:D_IDX + H_IDX, :] * (H_IDX ** -0.5)

    def score_tile(kj, c):
        ki = ki16[pl.ds(pl.multiple_of(kj * tk, tk), tk), :]
        acc = jnp.zeros((tk, tq), F32)
        for h in range(H_IDX):
            acc = acc + jnp.maximum(_dot_nt(ki, qidx_ref[h, 0]), 0.0) * w_sc[h:h + 1, :]
        st_scr[kj] = acc
        return c

    lax.fori_loop(0, nk, score_tile, 0)
    krow = lax.broadcasted_iota(jnp.int32, (tk, tq), 0)
    qcol = lax.broadcasted_iota(jnp.int32, (tk, tq), 1)
    st_scr[qi] = jnp.where((krow >> CHUNK_SHIFT) <= (qcol >> CHUNK_SHIFT), st_scr[qi], -jnp.inf)

    thr_scr[...] = jnp.full((1, tq), -3e38, F32)

    @pl.when(qi * tq + CHUNK >= topk)
    def _():
        thr_scr[...] = _kth_largest(lambda i: st_scr[i], nk, topk, 0, (1, tq))

    thr = thr_scr[...]

    def mask_tile(kj, c):
        mb_scr[kj] = jnp.where(st_scr[kj] >= thr, 0.0, NEG)
        return c

    lax.fori_loop(0, nk, mask_tile, 0)

    qk_diff = (qcol - krow).astype(F32)

    for g in range(KV_A):
        q_g = qa_ref[g * rep:(g + 1) * rep, 0].reshape(rep * tq, HD)
        acc_scr[...] = jnp.zeros(acc_scr.shape, F32)

        def kv_step(kj, carry):
            m_old, l_old = carry
            rows = pl.ds(pl.multiple_of(kj * tk, tk), tk)
            s_all = _dot_nt(k16[g, rows, :], q_g)
            dist = jnp.abs(qk_diff + ((qi - kj) * tk).astype(F32))
            mb = mb_scr[kj]
            s = jnp.concatenate(
                [s_all[:, hh * tq:(hh + 1) * tq] - slopes_ref[g * rep + hh] * dist + mb for hh in range(rep)],
                axis=1)
            m_new = jnp.maximum(m_old, jnp.max(s, axis=0, keepdims=True))
            alpha = jnp.exp(m_old - m_new)
            p = jnp.exp(s - m_new)
            l_new = alpha * l_old + jnp.sum(p, axis=0, keepdims=True)
            acc_scr[...] = alpha * acc_scr[...] + jnp.dot(vt16[g, kj], p.astype(BF16),
                                                          preferred_element_type=F32)
            return m_new, l_new

        _, l = lax.fori_loop(0, nk, kv_step,
                             (jnp.full((1, rep * tq), NEG, F32), jnp.zeros((1, rep * tq), F32)))
        o_t = acc_scr[...] / l
        for hh in range(rep):
            cols = slice((g * rep + hh) * HD, (g * rep + hh + 1) * HD)
            o = jnp.transpose(o_t[:, hh * tq:(hh + 1) * tq])
            o_ref[0, :, cols] = (o * _silu(ga_ref[0, :, cols].astype(F32))).astype(o_ref.dtype)


def _attn_a_prompt(qa, k, v, kw, qidx, ga, topk):
    n_heads, b, t, _ = qa.shape
    tq = min(256, topk)
    assert t % tq == 0 and tq % CHUNK == 0 and tq <= topk <= tq + CHUNK
    nkt = t // tq
    d_a = n_heads * HD
    return pl.pallas_call(
        functools.partial(_attn_a_kernel, topk=topk, n_heads=n_heads),
        out_shape=jax.ShapeDtypeStruct((b, t, d_a), BF16),
        grid=(b, nkt),
        in_specs=[pl.BlockSpec(memory_space=pltpu.SMEM),
                  pl.BlockSpec((n_heads, 1, tq, HD), lambda i, j: (0, i, j, 0)),
                  pl.BlockSpec((1, t, KV_A * HD), lambda i, j: (i, 0, 0)),
                  pl.BlockSpec((1, t, KV_A * HD), lambda i, j: (i, 0, 0)),
                  pl.BlockSpec((1, t, 128), lambda i, j: (i, 0, 0)),
                  pl.BlockSpec((1, tq, 128), lambda i, j: (i, j, 0)),
                  pl.BlockSpec((H_IDX, 1, tq, D_IDX), lambda i, j: (0, i, j, 0)),
                  pl.BlockSpec((1, tq, d_a), lambda i, j: (i, j, 0))],
        out_specs=pl.BlockSpec((1, tq, d_a), lambda i, j: (i, j, 0)),
        scratch_shapes=[pltpu.VMEM((KV_A, t, HD), BF16),
                        pltpu.VMEM((KV_A, nkt, HD, tq), BF16),
                        pltpu.VMEM((t, D_IDX), BF16),
                        pltpu.VMEM((nkt, tq, tq), F32),
                        pltpu.VMEM((nkt, tq, tq), F32),
                        pltpu.VMEM((HD, n_heads // KV_A * tq), F32),
                        pltpu.VMEM((1, tq), F32)],
        compiler_params=_cparams(("arbitrary", "arbitrary")),
        name="attn_a_prompt",
    )(_alibi_slopes(n_heads), qa, k, v, kw, kw, qidx, ga)


def _lam(lamp_ref):
    s1 = jnp.sum(lamp_ref[0:1, :] * lamp_ref[1:2, :], axis=1, keepdims=True)
    s2 = jnp.sum(lamp_ref[2:3, :] * lamp_ref[3:4, :], axis=1, keepdims=True)
    return jnp.exp(s1) - jnp.exp(s2) + LAM_INIT


def _subln_gate(o, subln, g):
    o = o * lax.rsqrt(jnp.mean(o * o, axis=-1, keepdims=True) + LN_EPS)
    o = o * subln * (1.0 - LAM_INIT)
    return o * _silu(g.astype(F32))


def _attn_b_kernel(slopes_ref, lamp_ref, subln_ref, q_ref, k_ref, v_ref, g_ref, o_ref, vt_scr, acc_scr):
    h = pl.program_id(1)
    qi = pl.program_id(2)
    tq = q_ref.shape[2]
    tk = tq
    slope = slopes_ref[h]

    @pl.when(qi == 0)
    def _():
        for j in range(vt_scr.shape[0]):
            vt_scr[j] = jnp.transpose(v_ref[0, 0, j * tk:(j + 1) * tk, :].astype(F32)).astype(BF16)

    krow = lax.broadcasted_iota(jnp.int32, (tk, tq), 0)
    qcol = lax.broadcasted_iota(jnp.int32, (tk, tq), 1)
    nrc = -slope * (qcol - krow).astype(F32)
    diag_bias = jnp.where((krow >> CHUNK_SHIFT) <= (qcol >> CHUNK_SHIFT), -jnp.abs(nrc), NEG)
    acc_scr[...] = jnp.zeros(acc_scr.shape, F32)

    def block(j, bias, carry):
        rows = pl.ds(pl.multiple_of(j * tk, tk), tk)
        vt = vt_scr[j]
        out = []
        for mi in range(2):
            m_old, l_old = carry[mi]
            s = _dot_nt(k_ref[mi, 0, rows, :], q_ref[mi, 0]) + bias
            m_new = jnp.maximum(m_old, jnp.max(s, axis=0, keepdims=True))
            alpha = jnp.exp(m_old - m_new)
            p = jnp.exp(s - m_new)
            l_new = alpha * l_old + jnp.sum(p, axis=0, keepdims=True)
            acc_scr[mi] = alpha * acc_scr[mi] + jnp.dot(vt, p.astype(BF16), preferred_element_type=F32)
            out.append((m_new, l_new))
        return tuple(out)

    def full_block(j, carry):
        return block(j, nrc - slope * ((qi - j) * tk).astype(F32), carry)

    init = (jnp.full((1, tq), NEG, F32), jnp.zeros((1, tq), F32))
    carry = lax.fori_loop(0, qi, full_block, (init, init))
    (_, l0), (_, l1) = block(qi, diag_bias, carry)

    o_t = acc_scr[0] / l0 - _lam(lamp_ref) * (acc_scr[1] / l1)
    o_t = o_t * lax.rsqrt(jnp.mean(o_t * o_t, axis=0, keepdims=True) + LN_EPS)
    o = jnp.transpose(o_t) * subln_ref[...] * (1.0 - LAM_INIT)
    o_ref[0] = (o * _silu(g_ref[0].astype(F32))).astype(o_ref.dtype)


def _attn_b_prompt(qb, kb16, vb16, gb, lamp, subln):
    _, b, t, _ = qb.shape
    n_heads = vb16.shape[0]
    tq = min(512, t)
    assert t % tq == 0 and tq % CHUNK == 0
    return pl.pallas_call(
        _attn_b_kernel,
        out_shape=jax.ShapeDtypeStruct((b, t, n_heads * 2 * HD), BF16),
        grid=(b, n_heads, t // tq),
        in_specs=[pl.BlockSpec(memory_space=pltpu.SMEM),
                  pl.BlockSpec((4, HD), lambda i, h, j: (0, 0)),
                  pl.BlockSpec((1, 2 * HD), lambda i, h, j: (0, 0)),
                  pl.BlockSpec((2, 1, tq, HD), lambda i, h, j: (h, i, j, 0)),
                  pl.BlockSpec((2, 1, t, HD), lambda i, h, j: (h, i, 0, 0)),
                  pl.BlockSpec((1, 1, t, 2 * HD), lambda i, h, j: (h, i, 0, 0)),
                  pl.BlockSpec((1, tq, 2 * HD), lambda i, h, j: (i, j, h))],
        out_specs=pl.BlockSpec((1, tq, 2 * HD), lambda i, h, j: (i, j, h)),
        scratch_shapes=[pltpu.VMEM((t // tq, 2 * HD, tq), BF16),
                        pltpu.VMEM((2, 2 * HD, tq), F32)],
        compiler_params=_cparams(("arbitrary", "arbitrary", "arbitrary")),
        name="attn_b_prompt",
    )(_alibi_slopes(n_heads), lamp, subln, qb, kb16, vb16, gb)


def _attn_a_sample_kernel(slopes_ref, qa_ref, kc_ref, vc_ref, kic_ref, kn_ref, vn_ref, kwn_ref, qidx_ref,
                          ga_ref, o_ref, k16, v16, sc_scr, s_scr, p_scr, *, topk, n_heads, cw):
    t = qa_ref.shape[2]
    past = kc_ref.shape[1]
    s_pad = sc_scr.shape[1]
    rep = n_heads // KV_A
    kcol = lax.broadcasted_iota(jnp.int32, (t, s_pad), 1)
    qrow = lax.broadcasted_iota(jnp.int32, (t, s_pad), 0)

    qi_all = qidx_ref[:, 0].reshape(H_IDX * t, D_IDX)
    w_q = kwn_ref[0, :, D_IDX:D_IDX + H_IDX] * (H_IDX ** -0.5)

    def scores(ki):
        r = jnp.maximum(_dot_nt(qi_all, ki), 0.0)
        acc = jnp.zeros((t, ki.shape[0]), F32)
        for h in range(H_IDX):
            acc = acc + r[h * t:(h + 1) * t, :] * w_q[:, h:h + 1]
        return acc

    for c0 in range(0, past, cw):
        sc_scr[:, c0:c0 + cw] = scores(kic_ref[0, c0:c0 + cw, :].astype(BF16))
    tail = s_pad - past
    ki_new = jnp.concatenate(
        [kwn_ref[0, :, :D_IDX], jnp.zeros((tail - t, D_IDX), F32)], axis=0).astype(BF16)
    tail_valid = lax.broadcasted_iota(jnp.int32, (t, tail), 1) < t
    sc_scr[:, past:] = jnp.where(tail_valid, scores(ki_new), -jnp.inf)

    thr = _kth_largest(lambda i: sc_scr[...], 1, topk, 1, (t, 1))
    mbias = jnp.where(sc_scr[...] >= thr, 0.0, NEG)
    dist = jnp.abs((past + qrow - kcol).astype(F32))

    for g in range(KV_A):
        cols = slice(g * HD, (g + 1) * HD)
        k16[:past] = kc_ref[0, :, cols].astype(BF16)
        v16[:past] = vc_ref[0, :, cols].astype(BF16)
        zpad = jnp.zeros((tail - t, HD), F32)
        k16[past:] = jnp.concatenate([kn_ref[0, :, cols], zpad], axis=0).astype(BF16)
        v16[past:] = jnp.concatenate([vn_ref[0, :, cols], zpad], axis=0).astype(BF16)
        q_g = qa_ref[g * rep:(g + 1) * rep, 0].reshape(rep * t, HD)
        s_scr[...] = _dot_nt(q_g, k16[...])
        inv_l = []
        for hh in range(rep):
            rows = slice(hh * t, (hh + 1) * t)
            s = s_scr[rows, :] - slopes_ref[g * rep + hh] * dist + mbias
            p = jnp.exp(s - jnp.max(s, axis=1, keepdims=True))
            inv_l.append(1.0 / jnp.sum(p, axis=1, keepdims=True))
            p_scr[rows, :] = p.astype(BF16)
        o_g = jnp.dot(p_scr[...], v16[...], preferred_element_type=F32)
        for hh in range(rep):
            hc = slice((g * rep + hh) * HD, (g * rep + hh + 1) * HD)
            o = o_g[hh * t:(hh + 1) * t, :] * inv_l[hh]
            o_ref[0, :, hc] = (o * _silu(ga_ref[0, :, hc].astype(F32))).astype(o_ref.dtype)


def _attn_a_sample(qa, kc, vc, kic, kn, vn, kwn, qidx, ga, topk, cw=512):
    n_heads, b, t, _ = qa.shape
    past = kc.shape[1]
    s_pad = past + 128
    assert t <= 128 and past % cw == 0
    d_a = n_heads * HD
    return pl.pallas_call(
        functools.partial(_attn_a_sample_kernel, topk=topk, n_heads=n_heads, cw=cw),
        out_shape=jax.ShapeDtypeStruct((b, t, d_a), BF16),
        grid=(b,),
        in_specs=[pl.BlockSpec(memory_space=pltpu.SMEM),
                  pl.BlockSpec((n_heads, 1, t, HD), lambda i: (0, i, 0, 0)),
                  pl.BlockSpec((1, past, KV_A * HD), lambda i: (i, 0, 0)),
                  pl.BlockSpec((1, past, KV_A * HD), lambda i: (i, 0, 0)),
                  pl.BlockSpec((1, past, D_IDX), lambda i: (i, 0, 0)),
                  pl.BlockSpec((1, t, KV_A * HD), lambda i: (i, 0, 0)),
                  pl.BlockSpec((1, t, KV_A * HD), lambda i: (i, 0, 0)),
                  pl.BlockSpec((1, t, 128), lambda i: (i, 0, 0)),
                  pl.BlockSpec((H_IDX, 1, t, D_IDX), lambda i: (0, i, 0, 0)),
                  pl.BlockSpec((1, t, d_a), lambda i: (i, 0, 0))],
        out_specs=pl.BlockSpec((1, t, d_a), lambda i: (i, 0, 0)),
        scratch_shapes=[pltpu.VMEM((s_pad, HD), BF16),
                        pltpu.VMEM((s_pad, HD), BF16),
                        pltpu.VMEM((t, s_pad), F32),
                        pltpu.VMEM((n_heads // KV_A * t, s_pad), F32),
                        pltpu.VMEM((n_heads // KV_A * t, s_pad), BF16)],
        compiler_params=_cparams(("arbitrary",)),
        name="attn_a_sample",
    )(_alibi_slopes(n_heads), qa, kc, vc, kic, kn, vn, kwn, qidx, ga)


def _attn_b_sample_kernel(slopes_ref, lamp_ref, subln_ref, q_ref, kc_ref, vc_ref, kn_ref, vn_ref, g_ref,
                          o_ref, k16, v16):
    h = pl.program_id(1)
    t = q_ref.shape[2]
    past = kc_ref.shape[1]
    s_pad = k16.shape[0]
    tail = s_pad - past
    k16[:past] = kc_ref[0].astype(BF16)
    v16[:past] = vc_ref[0].astype(BF16)
    zpad = jnp.zeros((tail - t, 2 * HD), F32)
    k16[past:] = jnp.concatenate([kn_ref[0], zpad], axis=0).astype(BF16)
    v16[past:] = jnp.concatenate([vn_ref[0], zpad], axis=0).astype(BF16)

    kcol = lax.broadcasted_iota(jnp.int32, (t, s_pad), 1)
    qrow = lax.broadcasted_iota(jnp.int32, (t, s_pad), 0)
    dist = jnp.abs((past + qrow - kcol).astype(F32))
    bias = jnp.where(kcol < past + t, -slopes_ref[h] * dist, NEG)

    z = jnp.zeros((t, HD), BF16)
    q_bd = jnp.concatenate([jnp.concatenate([q_ref[0, 0], z], axis=1),
                            jnp.concatenate([z, q_ref[1, 0]], axis=1)], axis=0)
    s = _dot_nt(q_bd, k16[...])

    def softmax(x):
        x = x + bias
        p = jnp.exp(x - jnp.max(x, axis=1, keepdims=True))
        return p / jnp.sum(p, axis=1, keepdims=True)

    a = softmax(s[:t]) - _lam(lamp_ref) * softmax(s[t:])
    o = jnp.dot(a.astype(BF16), v16[...], preferred_element_type=F32)
    o_ref[0] = _subln_gate(o, subln_ref[...], g_ref[0]).astype(o_ref.dtype)


def _attn_b_sample(qb, kc, vc, kn, vn, gb, lamp, subln):
    _, b, t, _ = qb.shape
    past = kc.shape[1]
    n_heads = kc.shape[2] // (2 * HD)
    s_pad = past + 128
    assert t <= 128
    return pl.pallas_call(
        _attn_b_sample_kernel,
        out_shape=jax.ShapeDtypeStruct((b, t, n_heads * 2 * HD), BF16),
        grid=(b, n_heads),
        in_specs=[pl.BlockSpec(memory_space=pltpu.SMEM),
                  pl.BlockSpec((4, HD), lambda i, h: (0, 0)),
                  pl.BlockSpec((1, 2 * HD), lambda i, h: (0, 0)),
                  pl.BlockSpec((2, 1, t, HD), lambda i, h: (h, i, 0, 0)),
                  pl.BlockSpec((1, past, 2 * HD), lambda i, h: (i, 0, h)),
                  pl.BlockSpec((1, past, 2 * HD), lambda i, h: (i, 0, h)),
                  pl.BlockSpec((1, t, 2 * HD), lambda i, h: (i, 0, h)),
                  pl.BlockSpec((1, t, 2 * HD), lambda i, h: (i, 0, h)),
                  pl.BlockSpec((1, t, 2 * HD), lambda i, h: (i, 0, h))],
        out_specs=pl.BlockSpec((1, t, 2 * HD), lambda i, h: (i, 0, h)),
        scratch_shapes=[pltpu.VMEM((s_pad, 2 * HD), BF16),
                        pltpu.VMEM((s_pad, 2 * HD), BF16)],
        compiler_params=_cparams(("arbitrary", "arbitrary")),
        name="attn_b_sample",
    )(_alibi_slopes(n_heads), lamp, subln, qb, kc, vc, kn, vn, gb)


def _out_kernel(ma_ref, mb_ref, wa_ref, wb_ref, x_ref, gate_ref, lng_ref, lnb_ref, o_ref, r_scr, *, alpha):
    j = pl.program_id(1)
    nj = pl.num_programs(1)
    tn = x_ref.shape[1]
    y = jnp.dot(ma_ref[...], wa_ref[...], preferred_element_type=F32)
    y = y + jnp.dot(mb_ref[...], wb_ref[...], preferred_element_type=F32)
    r_scr[j] = alpha * x_ref[...] + gate_ref[...] * y

    @pl.when(j == nj - 1)
    def _():
        n_tiles = r_scr.shape[0]
        d = n_tiles * tn
        tot = jnp.zeros((r_scr.shape[1], 1), F32)
        for c in range(n_tiles):
            tot = tot + jnp.sum(r_scr[c], axis=1, keepdims=True)
        mu = tot / d
        sq = jnp.zeros_like(tot)
        for c in range(n_tiles):
            dv = r_scr[c] - mu
            sq = sq + jnp.sum(dv * dv, axis=1, keepdims=True)
        inv = lax.rsqrt(sq / d + LN_EPS)
        for c in range(n_tiles):
            cols = slice(c * tn, (c + 1) * tn)
            o_ref[:, cols] = (r_scr[c] - mu) * inv * lng_ref[:, cols] + lnb_ref[:, cols]


def _out_proj(ma, mb, wa, wb, x2d, gate, ln_g, ln_b, alpha, tm, tn, rows_per_gate):
    m, da = ma.shape
    db = mb.shape[1]
    d = wa.shape[1]
    assert m % tm == 0 and d % tn == 0
    if rows_per_gate == 1:
        gate_spec = pl.BlockSpec((tm, tn), lambda i, j: (i, j))
    else:
        assert rows_per_gate % tm == 0
        gate = gate.reshape(-1, 1, d)
        gate_spec = pl.BlockSpec((None, 1, tn), lambda i, j: (i * tm // rows_per_gate, 0, j))
    return pl.pallas_call(
        functools.partial(_out_kernel, alpha=alpha),
        out_shape=jax.ShapeDtypeStruct((m, d), F32),
        grid=(m // tm, d // tn),
        in_specs=[pl.BlockSpec((tm, da), lambda i, j: (i, 0)),
                  pl.BlockSpec((tm, db), lambda i, j: (i, 0)),
                  pl.BlockSpec((da, tn), lambda i, j: (0, j)),
                  pl.BlockSpec((db, tn), lambda i, j: (0, j)),
                  pl.BlockSpec((tm, tn), lambda i, j: (i, j)),
                  gate_spec,
                  pl.BlockSpec((1, d), lambda i, j: (0, 0)),
                  pl.BlockSpec((1, d), lambda i, j: (0, 0))],
        out_specs=pl.BlockSpec((tm, d), lambda i, j: (i, 0)),
        scratch_shapes=[pltpu.VMEM((d // tn, tm, tn), F32)],
        compiler_params=_cparams(("arbitrary", "arbitrary")),
        name="out_proj",
    )(ma, mb, wa, wb, x2d, gate, ln_g.reshape(1, d), ln_b.reshape(1, d))


def _pad_w_in(w_in, d_a, d_b, n_b):
    sizes = (d_a, KV_A * HD, KV_A * HD, d_a, H_IDX * D_IDX, D_IDX, H_IDX, n_b, n_b, n_b, d_b)
    q_a, k_a, v_a, g_a, q_idx, k_idx, w_idx, q_b, k_b, v_b, g_b = jnp.split(
        w_in, np.cumsum(sizes)[:-1].tolist(), axis=1)
    small = 2 * KV_A * HD + D_IDX + H_IDX
    pad = jnp.zeros((w_in.shape[0], 1024 - small), w_in.dtype)
    return jnp.concatenate([k_a, v_a, k_idx, w_idx, pad, q_a, g_a, q_idx, q_b, k_b, v_b, g_b],
                           axis=1).astype(BF16)


def _project(h2d, w_pad, d_a, d_b, tm, want_b16):
    kvw = 2 * KV_A * HD + 128
    k_a, v_a, kw = _proj(h2d, w_pad, 0, kvw, kvw, tm, [
        ("flat", 0, KV_A * HD, 0, 1.0, F32),
        ("flat", KV_A * HD, 2 * KV_A * HD, 0, 1.0, F32),
        ("flat", 2 * KV_A * HD, kvw, 0, 1.0, F32)])
    c = 1024
    tn = 512
    (q_a,) = _proj(h2d, w_pad, c, d_a, tn, tm, [("heads", 0, tn, HD, HD ** -0.5, BF16)])
    c += d_a
    (g_a,) = _proj(h2d, w_pad, c, d_a, tn, tm, [("flat", 0, tn, 0, 1.0, BF16)])
    c += d_a
    (q_idx,) = _proj(h2d, w_pad, c, H_IDX * D_IDX, tn, tm, [("heads", 0, tn, D_IDX, 1.0, BF16)])
    c += H_IDX * D_IDX
    (q_b,) = _proj(h2d, w_pad, c, d_b, tn, tm, [("heads", 0, tn, HD, HD ** -0.5, BF16)])
    c += d_b
    kb_outs = [("flat", 0, tn, 0, 1.0, F32)]
    vb_outs = [("flat", 0, tn, 0, 1.0, F32)]
    if want_b16:
        kb_outs.append(("heads", 0, tn, HD, 1.0, BF16))
        vb_outs.append(("heads", 0, tn, 2 * HD, 1.0, BF16))
    k_b = _proj(h2d, w_pad, c, d_b, tn, tm, kb_outs)
    c += d_b
    v_b = _proj(h2d, w_pad, c, d_b, tn, tm, vb_outs)
    c += d_b
    (g_b,) = _proj(h2d, w_pad, c, d_b, tn, tm, [("flat", 0, tn, 0, 1.0, BF16)])
    return dict(k_a=k_a, v_a=v_a, kw=kw, q_a=q_a, g_a=g_a, q_idx=q_idx, q_b=q_b, k_b=k_b, v_b=v_b, g_b=g_b)


def _layer(x, mod, past, w_pad, w_out_a, w_out_b, lamp, subln, ln_g, ln_b, alpha):
    b, t, d = x.shape
    d_a = w_out_a.shape[0]
    d_b = w_out_b.shape[0]
    h_a = d_a // HD
    h_b = d_b // (2 * HD)
    m = b * t
    shift, scale, gate = (mod[:, i * d:(i + 1) * d] for i in range(3))
    prompt = past is None

    h = _modulate(x, shift.reshape(b, 1, d), scale.reshape(b, 1, d), tt=min(t, 256))
    p = _project(h.reshape(m, d), w_pad, d_a, d_b, tm=min(m, 1024), want_b16=prompt)

    r3 = lambda a: a.reshape(b, t, a.shape[-1])
    r4 = lambda a: a.reshape(a.shape[0], b, t, a.shape[-1])
    k_a, v_a, kw, g_a, g_b = r3(p["k_a"]), r3(p["v_a"]), r3(p["kw"]), r3(p["g_a"]), r3(p["g_b"])
    q_a, q_idx, q_b = r4(p["q_a"]), r4(p["q_idx"]), r4(p["q_b"])
    k_b, v_b = r3(p["k_b"][0]), r3(p["v_b"][0])

    if prompt:
        topk = min(TOPK_MAX, t // 4)
        mix_a = _attn_a_prompt(q_a, k_a, v_a, kw, q_idx, g_a, topk)
        mix_b = _attn_b_prompt(q_b, r4(p["k_b"][1]), r4(p["v_b"][1]), g_b, lamp, subln)
        tm_out, rows_per_gate, gate_rows = min(m, 512), t, gate
    else:
        kc_a, vc_a, kic, kc_b, vc_b = past
        plen = kc_a.shape[1]
        topk = min(TOPK_MAX, (plen + t) // 4)
        mix_a = _attn_a_sample(q_a, kc_a.reshape(b, plen, KV_A * HD), vc_a.reshape(b, plen, KV_A * HD), kic,
                               k_a, v_a, kw, q_idx, g_a, topk)
        mix_b = _attn_b_sample(q_b, kc_b.reshape(b, plen, d_b), vc_b.reshape(b, plen, d_b), k_b, v_b, g_b,
                               lamp, subln)
        tm_out, rows_per_gate = m, 1
        gate_rows = jnp.broadcast_to(gate[:, None, :], (b, t, d)).reshape(m, d)

    y = _out_proj(mix_a.reshape(m, d_a), mix_b.reshape(m, d_b), w_out_a, w_out_b, x.reshape(m, d),
                  gate_rows, ln_g, ln_b, alpha, tm=tm_out, tn=512, rows_per_gate=rows_per_gate)
    rows = (k_a.reshape(1, b, t, KV_A, HD), v_a.reshape(1, b, t, KV_A, HD), kw[None, :, :, :D_IDX],
            k_b.reshape(1, b, t, h_b, 2, HD), v_b.reshape(1, b, t, h_b, 2 * HD))
    return y.reshape(b, t, d), rows


def kernel(x_prompt, x_sample, cache_a_k, cache_a_v, cache_a_kidx, cache_b_k, cache_b_v, c_prompt, c_sample,
           w_ada, b_ada, w_in, w_out, lam_q1, lam_k1, lam_q2, lam_k2, subln_g, ln_g, ln_b):
    depth, d, _ = w_ada.shape
    assert depth == 1, "single-layer step"
    d_b = cache_b_v.shape[3] * cache_b_v.shape[4]
    d_a = w_out.shape[1] - d_b
    alpha = (2.0 * depth) ** 0.25
    bp = x_prompt.shape[0]

    mod = _ada(jnp.concatenate([c_prompt, c_sample], axis=0), w_ada[0], b_ada[0])
    w_pad = _pad_w_in(w_in[0], d_a, d_b, d_b)
    w_out16 = w_out[0].astype(BF16)
    lamp = jnp.concatenate([lam_q1, lam_k1, lam_q2, lam_k2], axis=0)
    common = (w_pad, w_out16[:d_a], w_out16[d_a:], lamp, subln_g, ln_g[0], ln_b[0], alpha)

    y_p, rows_p = _layer(x_prompt, mod[:bp], None, *common)
    past = (cache_a_k[0], cache_a_v[0], cache_a_kidx[0], cache_b_k[0], cache_b_v[0])
    y_s, rows_s = _layer(x_sample, mod[bp:], past, *common)
    return (y_p, y_s) + rows_p + rows_s
```

```python
import functools
import math

import jax
import jax.numpy as jnp
import numpy as np
from jax import lax
from jax.experimental import pallas as pl
from jax.experimental.pallas import tpu as pltpu

F32 = jnp.float32
BF16 = jnp.bfloat16

HD = 128
CHUNK = 64
CHUNK_SHIFT = 6
KV_A = 2
H_IDX = 16
D_IDX = 64
TOPK_MAX = 256
LN_EPS = 1e-5
LAM_INIT = 0.8 - 0.6 * math.exp(-0.3 * 0)

NEG = -1e30
N_BISECT = 16

VMEM_LIMIT = 56 * 1024 * 1024


def _cparams(sem):
    return pltpu.CompilerParams(dimension_semantics=sem, vmem_limit_bytes=VMEM_LIMIT)


def _silu(x):
    return x * jax.nn.sigmoid(x)


def _dot_nt(a, b):
    return lax.dot_general(a, b, (((1,), (1,)), ((), ())), preferred_element_type=F32)


def _alibi_slopes(n):
    return jnp.asarray(2.0 ** (-8.0 * np.arange(1, n + 1) / n), dtype=F32)


def _ada_kernel(c_ref, w_ref, b_ref, o_ref):
    s = _silu(c_ref[...]).astype(BF16)
    o_ref[...] = jnp.dot(s, w_ref[...].astype(BF16), preferred_element_type=F32) + b_ref[...]


def _ada(c, w_ada, b_ada, tn=512):
    n, d = c.shape
    e = w_ada.shape[1]
    return pl.pallas_call(
        _ada_kernel,
        out_shape=jax.ShapeDtypeStruct((n, e), F32),
        grid=(e // tn,),
        in_specs=[pl.BlockSpec((n, d), lambda j: (0, 0)),
                  pl.BlockSpec((d, tn), lambda j: (0, j)),
                  pl.BlockSpec((1, tn), lambda j: (0, j))],
        out_specs=pl.BlockSpec((n, tn), lambda j: (0, j)),
        compiler_params=_cparams(("arbitrary",)),
        name="ada",
    )(c, w_ada, b_ada.reshape(1, e))


def _modulate_kernel(x_ref, shift_ref, scale_ref, o_ref):
    o_ref[0] = (x_ref[0] * (1.0 + scale_ref[0]) + shift_ref[0]).astype(o_ref.dtype)


def _modulate(x, shift, scale, tt):
    b, t, d = x.shape
    return pl.pallas_call(
        _modulate_kernel,
        out_shape=jax.ShapeDtypeStruct((b, t, d), BF16),
        grid=(b, t // tt),
        in_specs=[pl.BlockSpec((1, tt, d), lambda i, j: (i, j, 0)),
                  pl.BlockSpec((1, 1, d), lambda i, j: (i, 0, 0)),
                  pl.BlockSpec((1, 1, d), lambda i, j: (i, 0, 0))],
        out_specs=pl.BlockSpec((1, tt, d), lambda i, j: (i, j, 0)),
        compiler_params=_cparams(("arbitrary", "arbitrary")),
        name="modulate",
    )(x, shift, scale)


LANES = 128
SUBLANES = 8


def _proj_kernel(h_ref, w_ref, *o_refs, outs):
    acc = jnp.dot(h_ref[...], w_ref[...], preferred_element_type=F32)
    for o_ref, (kind, lo, hi, width, scale) in zip(o_refs, outs):
        if kind == "flat":
            o_ref[...] = acc[:, lo:hi].astype(o_ref.dtype)
        elif kind == "heads":
            for hh in range((hi - lo) // width):
                blk = acc[:, lo + hh * width: lo + (hh + 1) * width]
                if scale != 1.0:
                    blk = blk * scale
                o_ref[hh] = blk.astype(o_ref.dtype)
        elif kind == "lanes":
            for hh in range((hi - lo) // LANES):
                o_ref[hh] = acc[:, lo + hh * LANES: lo + (hh + 1) * LANES].astype(o_ref.dtype)
        else:
            for s in range(SUBLANES):
                o_ref[:, s, :] = acc[:, lo + s * LANES: lo + (s + 1) * LANES].astype(o_ref.dtype)


def _proj(h2d, w_pad, col_start, ncols, tn, tm, outs):
    m, k = h2d.shape
    nj = ncols // tn
    assert col_start % tn == 0 and ncols % tn == 0 and m % tm == 0
    j0 = col_start // tn
    out_shapes, out_specs, kouts = [], [], []
    for kind, lo, hi, width, scale, dtype in outs:
        if kind == "flat":
            out_shapes.append(jax.ShapeDtypeStruct((m, nj * (hi - lo)), dtype))
            out_specs.append(pl.BlockSpec((tm, hi - lo), lambda i, j: (i, j)))
        elif kind == "heads":
            nh = (hi - lo) // width
            out_shapes.append(jax.ShapeDtypeStruct((nj * nh, m, width), dtype))
            out_specs.append(pl.BlockSpec((nh, tm, width), lambda i, j: (j, i, 0)))
        elif kind == "lanes":
            nh = (hi - lo) // LANES
            out_shapes.append(jax.ShapeDtypeStruct((nh, m, nj * LANES), dtype))
            out_specs.append(pl.BlockSpec((nh, tm, LANES), lambda i, j: (0, i, j)))
        else:
            assert hi - lo == SUBLANES * LANES
            out_shapes.append(jax.ShapeDtypeStruct((m, nj, SUBLANES, LANES), dtype))
            out_specs.append(pl.BlockSpec((tm, None, SUBLANES, LANES), lambda i, j: (i, j, 0, 0)))
        kouts.append((kind, lo, hi, width, scale))
    return pl.pallas_call(
        functools.partial(_proj_kernel, outs=tuple(kouts)),
        out_shape=out_shapes,
        grid=(m // tm, nj),
        in_specs=[pl.BlockSpec((tm, k), lambda i, j: (i, 0)),
                  pl.BlockSpec((k, tn), lambda i, j: (0, j0 + j))],
        out_specs=out_specs,
        compiler_params=_cparams(("arbitrary", "arbitrary")),
        name=f"proj_c{col_start}",
    )(h2d, w_pad)


def _kth_largest(load, ntiles, k, axis, shape):
    neg_inf = jnp.full(shape, -jnp.inf, F32)

    def reduce_tiles(fn, init):
        return lax.fori_loop(0, ntiles, lambda i, c: fn(load(i), c), init)

    def count_ge(t):
        return reduce_tiles(
            lambda x, c: c + jnp.sum(jnp.where(x >= t, 1.0, 0.0), axis=axis, keepdims=True),
            jnp.zeros(shape, F32))

    def max_below(t, strict):
        def fn(x, c):
            keep = (x < t) if strict else (x <= t)
            return jnp.maximum(c, jnp.max(jnp.where(keep, x, -jnp.inf), axis=axis, keepdims=True))
        return reduce_tiles(fn, neg_inf)

    def minmax(x, c):
        mn, mx = c
        mn = jnp.minimum(mn, jnp.min(jnp.where(x > -jnp.inf, x, jnp.inf), axis=axis, keepdims=True))
        mx = jnp.maximum(mx, jnp.max(x, axis=axis, keepdims=True))
        return mn, mx

    lo, hi = reduce_tiles(minmax, (jnp.full(shape, jnp.inf, F32), neg_inf))
    kf = float(k)

    def bisect(_, c):
        lo, hi = c
        mid = 0.5 * lo + 0.5 * hi
        ok = count_ge(mid) >= kf
        return jnp.where(ok, mid, lo), jnp.where(ok, hi, mid)

    lo, hi = lax.fori_loop(0, N_BISECT, bisect, (lo, hi))
    v = max_below(hi, strict=False)
    c = count_ge(v)

    def walk(vc):
        v, c = vc
        v = jnp.where(c < kf, max_below(v, strict=True), v)
        return v, count_ge(v)

    v, _ = lax.while_loop(lambda vc: jnp.min(vc[1]) < kf, walk, (v, c))
    return v


def _attn_a_kernel(slopes_ref, qa_ref, k_ref, v_ref, kw_ref, kwq_ref, qidx_ref, ga_ref, o_ref,
                   k16, vt16, ki16, st_scr, mb_scr, acc_scr, thr_scr, *, topk, n_heads):
    qi = pl.program_id(1)
    tq = qa_ref.shape[2]
    tk = tq
    rep = n_heads // KV_A
    nk = qi + 1

    @pl.when(qi == 0)
    def _():
        for g in range(KV_A):
            cols = slice(g * HD, (g + 1) * HD)
            k16[g] = k_ref[0, :, cols].astype(BF16)
            for j in range(vt16.shape[1]):
                vt16[g, j] = jnp.transpose(v_ref[0, j * tk:(j + 1) * tk, cols]).astype(BF16)
        ki16[...] = kw_ref[0, :, :D_IDX].astype(BF16)

    w_t = jnp.transpose(kwq_ref[0])
    w_sc = w_t[D_IDX:D_IDX + H_IDX, :] * (H_IDX ** -0.5)

    def score_tile(kj, c):
        ki = ki16[pl.ds(pl.multiple_of(kj * tk, tk), tk), :]
        acc = jnp.zeros((tk, tq), F32)
        for h in range(H_IDX):
            acc = acc + jnp.maximum(_dot_nt(ki, qidx_ref[h, 0]), 0.0) * w_sc[h:h + 1, :]
        st_scr[kj] = acc
        return c

    lax.fori_loop(0, nk, score_tile, 0)
    krow = lax.broadcasted_iota(jnp.int32, (tk, tq), 0)
    qcol = lax.broadcasted_iota(jnp.int32, (tk, tq), 1)
    st_scr[qi] = jnp.where((krow >> CHUNK_SHIFT) <= (qcol >> CHUNK_SHIFT), st_scr[qi], -jnp.inf)

    thr_scr[...] = jnp.full((1, tq), -3e38, F32)

    @pl.when(qi * tq + CHUNK >= topk)
    def _():
        thr_scr[...] = _kth_largest(lambda i: st_scr[i], nk, topk, 0, (1, tq))

    thr = thr_scr[...]

    def mask_tile(kj, c):
        mb_scr[kj] = jnp.where(st_scr[kj] >= thr, 0.0, NEG)
        return c

    lax.fori_loop(0, nk, mask_tile, 0)

    qk_diff = (qcol - krow).astype(F32)

    for g in range(KV_A):
        q_g = qa_ref[g * rep:(g + 1) * rep, 0].reshape(rep * tq, HD)
        acc_scr[...] = jnp.zeros(acc_scr.shape, F32)

        def kv_step(kj, carry):
            m_old, l_old = carry
            rows = pl.ds(pl.multiple_of(kj * tk, tk), tk)
            s_all = _dot_nt(k16[g, rows, :], q_g)
            dist = jnp.abs(qk_diff + ((qi - kj) * tk).astype(F32))
            mb = mb_scr[kj]
            s = jnp.concatenate(
                [s_all[:, hh * tq:(hh + 1) * tq] - slopes_ref[g * rep + hh] * dist + mb for hh in range(rep)],
                axis=1)
            m_new = jnp.maximum(m_old, jnp.max(s, axis=0, keepdims=True))
            alpha = jnp.exp(m_old - m_new)
            p = jnp.exp(s - m_new)
            l_new = alpha * l_old + jnp.sum(p, axis=0, keepdims=True)
            acc_scr[...] = alpha * acc_scr[...] + jnp.dot(vt16[g, kj], p.astype(BF16),
                                                          preferred_element_type=F32)
            return m_new, l_new

        _, l = lax.fori_loop(0, nk, kv_step,
                             (jnp.full((1, rep * tq), NEG, F32), jnp.zeros((1, rep * tq), F32)))
        o_t = acc_scr[...] / l
        for hh in range(rep):
            cols = slice((g * rep + hh) * HD, (g * rep + hh + 1) * HD)
            o = jnp.transpose(o_t[:, hh * tq:(hh + 1) * tq])
            o_ref[0, :, cols] = (o * _silu(ga_ref[0, :, cols].astype(F32))).astype(o_ref.dtype)


def _attn_a_prompt(qa, k, v, kw, qidx, ga, topk, d_mix):
    n_heads, b, t, _ = qa.shape
    tq = min(256, topk)
    assert t % tq == 0 and tq % CHUNK == 0 and tq <= topk <= tq + CHUNK
    nkt = t // tq
    d_a = n_heads * HD
    return pl.pallas_call(
        functools.partial(_attn_a_kernel, topk=topk, n_heads=n_heads),
        out_shape=jax.ShapeDtypeStruct((b, t, d_mix), BF16),
        grid=(b, nkt),
        in_specs=[pl.BlockSpec(memory_space=pltpu.SMEM),
                  pl.BlockSpec((n_heads, 1, tq, HD), lambda i, j: (0, i, j, 0)),
                  pl.BlockSpec((1, t, KV_A * HD), lambda i, j: (i, 0, 0)),
                  pl.BlockSpec((1, t, KV_A * HD), lambda i, j: (i, 0, 0)),
                  pl.BlockSpec((1, t, 128), lambda i, j: (i, 0, 0)),
                  pl.BlockSpec((1, tq, 128), lambda i, j: (i, j, 0)),
                  pl.BlockSpec((H_IDX, 1, tq, D_IDX), lambda i, j: (0, i, j, 0)),
                  pl.BlockSpec((1, tq, d_a), lambda i, j: (i, j, 0))],
        out_specs=pl.BlockSpec((1, tq, d_a), lambda i, j: (i, j, 0)),
        scratch_shapes=[pltpu.VMEM((KV_A, t, HD), BF16),
                        pltpu.VMEM((KV_A, nkt, HD, tq), BF16),
                        pltpu.VMEM((t, D_IDX), BF16),
                        pltpu.VMEM((nkt, tq, tq), F32),
                        pltpu.VMEM((nkt, tq, tq), F32),
                        pltpu.VMEM((HD, n_heads // KV_A * tq), F32),
                        pltpu.VMEM((1, tq), F32)],
        compiler_params=_cparams(("arbitrary", "arbitrary")),
        name="attn_a_prompt",
    )(_alibi_slopes(n_heads), qa, k, v, kw, kw, qidx, ga)


def _lam(lamp_ref):
    s1 = jnp.sum(lamp_ref[0:1, :] * lamp_ref[1:2, :], axis=1, keepdims=True)
    s2 = jnp.sum(lamp_ref[2:3, :] * lamp_ref[3:4, :], axis=1, keepdims=True)
    return jnp.exp(s1) - jnp.exp(s2) + LAM_INIT


def _subln_gate(o, subln, g):
    o = o * lax.rsqrt(jnp.mean(o * o, axis=-1, keepdims=True) + LN_EPS)
    o = o * subln * (1.0 - LAM_INIT)
    return o * _silu(g.astype(F32))


def _attn_b_kernel(slopes_ref, lamp_ref, subln_ref, q_ref, k_ref, v_ref, g_ref, mix_ref, o_ref,
                   vt_scr, s_scr, acc_scr, *, tq):
    h = pl.program_id(1)
    t = q_ref.shape[2]
    tk = tq
    nq = t // tq
    slope = slopes_ref[h]
    lam = _lam(lamp_ref)

    for j in range(nq):
        vt_scr[j] = jnp.transpose(v_ref[0, 0, j * tk:(j + 1) * tk, :].astype(F32)).astype(BF16)

    krow = lax.broadcasted_iota(jnp.int32, (tk, tq), 0)
    qcol = lax.broadcasted_iota(jnp.int32, (tk, tq), 1)
    nrc = -slope * (qcol - krow).astype(F32)
    diag_bias = jnp.where((krow >> CHUNK_SHIFT) <= (qcol >> CHUNK_SHIFT), -jnp.abs(nrc), NEG)

    pairs = [(qi, j) for qi in range(nq) for j in range(qi + 1)]

    def issue_logits(n):
        qi, j = pairs[n]
        for mi in range(2):
            s_scr[n % 2, mi] = _dot_nt(k_ref[mi, 0, j * tk:(j + 1) * tk, :], q_ref[mi, 0, qi * tq:(qi + 1) * tq, :])

    issue_logits(0)
    carry = None
    for n, (qi, j) in enumerate(pairs):
        if n + 1 < len(pairs):
            issue_logits(n + 1)
        if j == 0:
            acc_scr[...] = jnp.zeros(acc_scr.shape, F32)
            init = (jnp.full((1, tq), NEG, F32), jnp.zeros((1, tq), F32))
            carry = [init, init]
        bias = diag_bias if j == qi else nrc - slope * float((qi - j) * tk)
        for mi in range(2):
            m_old, l_old = carry[mi]
            s = s_scr[n % 2, mi] + bias
            m_new = jnp.maximum(m_old, jnp.max(s, axis=0, keepdims=True))
            alpha = jnp.exp(m_old - m_new)
            p = jnp.exp(s - m_new)
            l_new = alpha * l_old + jnp.sum(p, axis=0, keepdims=True)
            acc_scr[mi] = alpha * acc_scr[mi] + jnp.dot(vt_scr[j], p.astype(BF16), preferred_element_type=F32)
            carry[mi] = (m_new, l_new)
        if j == qi:
            o_t = acc_scr[0] / carry[0][1] - lam * (acc_scr[1] / carry[1][1])
            o_t = o_t * lax.rsqrt(jnp.mean(o_t * o_t, axis=0, keepdims=True) + LN_EPS)
            o = jnp.transpose(o_t) * subln_ref[...] * (1.0 - LAM_INIT)
            rows = slice(qi * tq, (qi + 1) * tq)
            o_ref[0, rows, :] = (o * _silu(g_ref[0, rows, :].astype(F32))).astype(o_ref.dtype)


def _attn_b_prompt(qb, kb16, vb16, gb, lamp, subln, mix):
    _, b, t, _ = qb.shape
    n_heads = vb16.shape[0]
    col0 = (mix.shape[2] - n_heads * 2 * HD) // (2 * HD)
    tq = min(512, t)
    assert t % tq == 0 and tq % CHUNK == 0
    return pl.pallas_call(
        functools.partial(_attn_b_kernel, tq=tq),
        out_shape=jax.ShapeDtypeStruct(mix.shape, mix.dtype),
        grid=(b, n_heads),
        in_specs=[pl.BlockSpec(memory_space=pltpu.SMEM),
                  pl.BlockSpec((4, HD), lambda i, h: (0, 0)),
                  pl.BlockSpec((1, 2 * HD), lambda i, h: (0, 0)),
                  pl.BlockSpec((2, 1, t, HD), lambda i, h: (h, i, 0, 0)),
                  pl.BlockSpec((2, 1, t, HD), lambda i, h: (h, i, 0, 0)),
                  pl.BlockSpec((1, 1, t, 2 * HD), lambda i, h: (h, i, 0, 0)),
                  pl.BlockSpec((1, t, 2 * HD), lambda i, h: (i, 0, h)),
                  pl.BlockSpec(memory_space=pl.ANY)],
        out_specs=pl.BlockSpec((1, t, 2 * HD), lambda i, h: (i, 0, col0 + h)),
        input_output_aliases={7: 0},
        scratch_shapes=[pltpu.VMEM((t // tq, 2 * HD, tq), BF16),
                        pltpu.VMEM((2, 2, tq, tq), F32),
                        pltpu.VMEM((2, 2 * HD, tq), F32)],
        compiler_params=_cparams(("arbitrary", "arbitrary")),
        name="attn_b_prompt",
    )(_alibi_slopes(n_heads), lamp, subln, qb, kb16, vb16, gb, mix)


def _attn_a_sample_kernel(slopes_ref, qa_ref, kc_ref, vc_ref, kic_ref, kn_ref, vn_ref, kwn_ref, qidx_ref,
                          ga_ref, o_ref, k16, v16, sc_scr, s_scr, p_scr, *, topk, n_heads, cw):
    t = qa_ref.shape[2]
    past = kc_ref.shape[0] // KV_A
    s_pad = sc_scr.shape[1]
    rep = n_heads // KV_A
    kcol = lax.broadcasted_iota(jnp.int32, (t, s_pad), 1)
    qrow = lax.broadcasted_iota(jnp.int32, (t, s_pad), 0)

    qi_all = qidx_ref[:, 0].reshape(H_IDX * t, D_IDX)
    w_q = kwn_ref[0, :, D_IDX:D_IDX + H_IDX] * (H_IDX ** -0.5)

    def scores(ki):
        r = jnp.maximum(_dot_nt(qi_all, ki), 0.0)
        acc = jnp.zeros((t, ki.shape[0]), F32)
        for h in range(H_IDX):
            acc = acc + r[h * t:(h + 1) * t, :] * w_q[:, h:h + 1]
        return acc

    for c0 in range(0, past, cw):
        sc_scr[:, c0:c0 + cw] = scores(kic_ref[0, c0:c0 + cw, :].astype(BF16))
    tail = s_pad - past
    ki_new = jnp.concatenate(
        [kwn_ref[0, :, :D_IDX], jnp.zeros((tail - t, D_IDX), F32)], axis=0).astype(BF16)
    tail_valid = lax.broadcasted_iota(jnp.int32, (t, tail), 1) < t
    sc_scr[:, past:] = jnp.where(tail_valid, scores(ki_new), -jnp.inf)

    thr = _kth_largest(lambda i: sc_scr[...], 1, topk, 1, (t, 1))
    mbias = jnp.where(sc_scr[...] >= thr, 0.0, NEG)
    dist = jnp.abs((past + qrow - kcol).astype(F32))

    for g in range(KV_A):
        cols = slice(g * HD, (g + 1) * HD)
        k16[:past] = kc_ref[pl.ds(g, past, stride=KV_A), :].astype(BF16)
        v16[:past] = vc_ref[pl.ds(g, past, stride=KV_A), :].astype(BF16)
        zpad = jnp.zeros((tail - t, HD), F32)
        k16[past:] = jnp.concatenate([kn_ref[0, :, cols], zpad], axis=0).astype(BF16)
        v16[past:] = jnp.concatenate([vn_ref[0, :, cols], zpad], axis=0).astype(BF16)
        q_g = qa_ref[g * rep:(g + 1) * rep, 0].reshape(rep * t, HD)
        s_scr[...] = _dot_nt(q_g, k16[...])
        inv_l = []
        for hh in range(rep):
            rows = slice(hh * t, (hh + 1) * t)
            s = s_scr[rows, :] - slopes_ref[g * rep + hh] * dist + mbias
            p = jnp.exp(s - jnp.max(s, axis=1, keepdims=True))
            inv_l.append(1.0 / jnp.sum(p, axis=1, keepdims=True))
            p_scr[rows, :] = p.astype(BF16)
        o_g = jnp.dot(p_scr[...], v16[...], preferred_element_type=F32)
        for hh in range(rep):
            hc = slice((g * rep + hh) * HD, (g * rep + hh + 1) * HD)
            o = o_g[hh * t:(hh + 1) * t, :] * inv_l[hh]
            o_ref[0, :, hc] = (o * _silu(ga_ref[0, :, hc].astype(F32))).astype(o_ref.dtype)


def _attn_a_sample(qa, kc, vc, kic, kn, vn, kwn, qidx, ga, topk, d_mix, cw=512):
    n_heads, b, t, _ = qa.shape
    past = kic.shape[1]
    s_pad = past + 128
    assert t <= 128 and past % cw == 0
    d_a = n_heads * HD
    return pl.pallas_call(
        functools.partial(_attn_a_sample_kernel, topk=topk, n_heads=n_heads, cw=cw),
        out_shape=jax.ShapeDtypeStruct((b, t, d_mix), BF16),
        grid=(b,),
        in_specs=[pl.BlockSpec(memory_space=pltpu.SMEM),
                  pl.BlockSpec((n_heads, 1, t, HD), lambda i: (0, i, 0, 0)),
                  pl.BlockSpec((past * KV_A, HD), lambda i: (i, 0)),
                  pl.BlockSpec((past * KV_A, HD), lambda i: (i, 0)),
                  pl.BlockSpec((1, past, D_IDX), lambda i: (i, 0, 0)),
                  pl.BlockSpec((1, t, KV_A * HD), lambda i: (i, 0, 0)),
                  pl.BlockSpec((1, t, KV_A * HD), lambda i: (i, 0, 0)),
                  pl.BlockSpec((1, t, 128), lambda i: (i, 0, 0)),
                  pl.BlockSpec((H_IDX, 1, t, D_IDX), lambda i: (0, i, 0, 0)),
                  pl.BlockSpec((1, t, d_a), lambda i: (i, 0, 0))],
        out_specs=pl.BlockSpec((1, t, d_a), lambda i: (i, 0, 0)),
        scratch_shapes=[pltpu.VMEM((s_pad, HD), BF16),
                        pltpu.VMEM((s_pad, HD), BF16),
                        pltpu.VMEM((t, s_pad), F32),
                        pltpu.VMEM((n_heads // KV_A * t, s_pad), F32),
                        pltpu.VMEM((n_heads // KV_A * t, s_pad), BF16)],
        compiler_params=_cparams(("arbitrary",)),
        name="attn_a_sample",
    )(_alibi_slopes(n_heads), qa, kc, vc, kic, kn, vn, kwn, qidx, ga)


def _attn_b_sample_kernel(slopes_ref, lamp_ref, subln_ref, q_ref, kc_ref, vc_ref, kn_ref, vn_ref, g_ref, mix_ref,
                          o_ref, m_scr, l_scr, acc_scr, *, n_heads, past, v_half_major):
    c = pl.program_id(1)
    t = q_ref.shape[2]
    slots = 2 * n_heads
    pc = kc_ref.shape[0] // slots
    lam = _lam(lamp_ref)

    @pl.when(c == 0)
    def _():
        m_scr[...] = jnp.full(m_scr.shape, NEG, F32)
        l_scr[...] = jnp.zeros(l_scr.shape, F32)
        acc_scr[...] = jnp.zeros(acc_scr.shape, F32)

    def q_blockdiag(h):
        z = jnp.zeros((t, HD), BF16)
        return jnp.concatenate([jnp.concatenate([q_ref[2 * h, 0], z], axis=1),
                                jnp.concatenate([z, q_ref[2 * h + 1, 0]], axis=1)], axis=0)

    def update(h, s, v16):
        m_old = m_scr[h]
        m_new = jnp.maximum(m_old, jnp.max(s, axis=1, keepdims=True))
        alpha = jnp.exp(m_old - m_new)
        p = jnp.exp(s - m_new)
        l_scr[h] = alpha * l_scr[h] + jnp.sum(p, axis=1, keepdims=True)
        acc_scr[h] = alpha * acc_scr[h] + jnp.dot(p.astype(BF16), v16, preferred_element_type=F32)
        m_scr[h] = m_new

    qrow = lax.broadcasted_iota(jnp.int32, (t, pc), 0)
    kcol = lax.broadcasted_iota(jnp.int32, (t, pc), 1)
    dist = (past + qrow - kcol).astype(F32) - (c * pc).astype(F32)
    for h in range(n_heads):
        k01 = jnp.concatenate([kc_ref[pl.ds(2 * h, pc, stride=slots), :],
                               kc_ref[pl.ds(2 * h + 1, pc, stride=slots), :]], axis=1).astype(BF16)
        v_h = jnp.concatenate([vc_ref[pl.ds(h, pc, stride=slots), :],
                               vc_ref[pl.ds(n_heads + h, pc, stride=slots), :]], axis=1).astype(BF16)
        bias = -slopes_ref[h] * dist
        update(h, _dot_nt(q_blockdiag(h), k01) + jnp.concatenate([bias, bias], axis=0), v_h)

    @pl.when(c == pl.num_programs(1) - 1)
    def _():
        qr = lax.broadcasted_iota(jnp.int32, (t, LANES), 0)
        kc_ = lax.broadcasted_iota(jnp.int32, (t, LANES), 1)
        dist_new = jnp.abs((qr - kc_).astype(F32))
        zpad = jnp.zeros((LANES - t, 2 * HD), F32)
        for h in range(n_heads):
            cols = slice(h * 2 * HD, (h + 1) * 2 * HD)
            kn = jnp.concatenate([kn_ref[0, :, cols], zpad], axis=0).astype(BF16)
            if v_half_major:
                v_new = jnp.concatenate([vn_ref[0, :, (half * n_heads + h) * HD:(half * n_heads + h + 1) * HD]
                                         for half in range(2)], axis=1)
            else:
                v_new = vn_ref[0, :, cols]
            vn = jnp.concatenate([v_new, zpad], axis=0).astype(BF16)
            bias = jnp.where(kc_ < t, -slopes_ref[h] * dist_new, NEG)
            update(h, _dot_nt(q_blockdiag(h), kn) + jnp.concatenate([bias, bias], axis=0), vn)
            o_all = acc_scr[h] / l_scr[h]
            o = o_all[:t] - lam * o_all[t:]
            o_ref[0, :, cols] = _subln_gate(o, subln_ref[...], g_ref[0, :, cols]).astype(o_ref.dtype)


def _attn_b_sample(qb, kc, vc, kn, vn, gb, lamp, subln, mix, past, v_half_major, pc=1024):
    _, b, t, _ = qb.shape
    d_b = gb.shape[2]
    n_heads = d_b // (2 * HD)
    slots = 2 * n_heads
    pc = min(pc, past)
    assert t <= LANES and past % pc == 0 and (mix.shape[2] - d_b) % d_b == 0
    nc = past // pc
    col0 = (mix.shape[2] - d_b) // d_b
    return pl.pallas_call(
        functools.partial(_attn_b_sample_kernel, n_heads=n_heads, past=past, v_half_major=v_half_major),
        out_shape=jax.ShapeDtypeStruct(mix.shape, mix.dtype),
        grid=(b, nc),
        in_specs=[pl.BlockSpec(memory_space=pltpu.SMEM),
                  pl.BlockSpec((4, HD), lambda i, c: (0, 0)),
                  pl.BlockSpec((1, 2 * HD), lambda i, c: (0, 0)),
                  pl.BlockSpec((slots, 1, t, HD), lambda i, c: (0, i, 0, 0)),
                  pl.BlockSpec((pc * slots, LANES), lambda i, c: (i * nc + c, 0)),
                  pl.BlockSpec((pc * slots, LANES), lambda i, c: (i * nc + c, 0)),
                  pl.BlockSpec((1, t, d_b), lambda i, c: (i, 0, 0)),
                  pl.BlockSpec((1, t, d_b), lambda i, c: (i, 0, 0)),
                  pl.BlockSpec((1, t, d_b), lambda i, c: (i, 0, 0)),
                  pl.BlockSpec(memory_space=pl.ANY)],
        out_specs=pl.BlockSpec((1, t, d_b), lambda i, c: (i, 0, col0)),
        input_output_aliases={9: 0},
        scratch_shapes=[pltpu.VMEM((n_heads, 2 * t, 1), F32),
                        pltpu.VMEM((n_heads, 2 * t, 1), F32),
                        pltpu.VMEM((n_heads, 2 * t, 2 * HD), F32)],
        compiler_params=_cparams(("arbitrary", "arbitrary")),
        name="attn_b_sample",
    )(_alibi_slopes(n_heads), lamp, subln, qb, kc, vc, kn, vn, gb, mix)


def _out_kernel(mix_ref, w_ref, x_ref, gate_ref, lng_ref, lnb_ref, o_ref, *, alpha, nj):
    j = pl.program_id(1)
    tn = x_ref.shape[1]
    r = alpha * x_ref[...] + gate_ref[...] * jnp.dot(mix_ref[...], w_ref[...], preferred_element_type=F32)
    for jj in range(nj):
        @pl.when(j == jj)
        def _(jj=jj):
            o_ref[:, jj * tn:(jj + 1) * tn] = r

    @pl.when(j == nj - 1)
    def _():
        d = nj * tn
        tot = jnp.zeros((o_ref.shape[0], 1), F32)
        for c in range(nj):
            tot = tot + jnp.sum(o_ref[:, c * tn:(c + 1) * tn], axis=1, keepdims=True)
        mu = tot / d
        sq = jnp.zeros_like(tot)
        for c in range(nj):
            dv = o_ref[:, c * tn:(c + 1) * tn] - mu
            sq = sq + jnp.sum(dv * dv, axis=1, keepdims=True)
        inv = lax.rsqrt(sq / d + LN_EPS)
        for c in range(nj):
            cols = slice(c * tn, (c + 1) * tn)
            o_ref[:, cols] = (o_ref[:, cols] - mu) * inv * lng_ref[:, cols] + lnb_ref[:, cols]


def _out_proj(mix, w_out, x2d, gate, ln_g, ln_b, alpha, tm, tn, rows_per_gate):
    m, dm = mix.shape
    d = w_out.shape[1]
    assert m % tm == 0 and d % tn == 0
    if rows_per_gate == 1:
        gate_spec = pl.BlockSpec((tm, tn), lambda i, j: (i, j))
    else:
        assert rows_per_gate % tm == 0
        gate = gate.reshape(-1, 1, d)
        gate_spec = pl.BlockSpec((None, 1, tn), lambda i, j: (i * tm // rows_per_gate, 0, j))
    return pl.pallas_call(
        functools.partial(_out_kernel, alpha=alpha, nj=d // tn),
        out_shape=jax.ShapeDtypeStruct((m, d), F32),
        grid=(m // tm, d // tn),
        in_specs=[pl.BlockSpec((tm, dm), lambda i, j: (i, 0)),
                  pl.BlockSpec((dm, tn), lambda i, j: (0, j)),
                  pl.BlockSpec((tm, tn), lambda i, j: (i, j)),
                  gate_spec,
                  pl.BlockSpec((1, d), lambda i, j: (0, 0)),
                  pl.BlockSpec((1, d), lambda i, j: (0, 0))],
        out_specs=pl.BlockSpec((tm, d), lambda i, j: (i, 0)),
        compiler_params=_cparams(("arbitrary", "arbitrary")),
        name="out_proj",
    )(mix, w_out, x2d, gate, ln_g.reshape(1, d), ln_b.reshape(1, d))


def _pad_w_in(w_in, d_a, d_b, n_b, v_half_major):
    sizes = (d_a, KV_A * HD, KV_A * HD, d_a, H_IDX * D_IDX, D_IDX, H_IDX, n_b, n_b, n_b, d_b)
    q_a, k_a, v_a, g_a, q_idx, k_idx, w_idx, q_b, k_b, v_b, g_b = jnp.split(
        w_in, np.cumsum(sizes)[:-1].tolist(), axis=1)
    small = 2 * KV_A * HD + D_IDX + H_IDX
    pad = jnp.zeros((w_in.shape[0], 1024 - small), w_in.dtype)
    if v_half_major:
        v_b = v_b.reshape(-1, n_b // (2 * HD), 2, HD).transpose(0, 2, 1, 3).reshape(-1, n_b)
    return jnp.concatenate([k_a, v_a, k_idx, w_idx, pad, q_a, g_a, q_idx, q_b, k_b, v_b, g_b],
                           axis=1).astype(BF16)


def _project(h2d, w_pad, d_a, d_b, tm, want_b16, native):
    kvw = 2 * KV_A * HD + 128
    k_a, v_a, kw = _proj(h2d, w_pad, 0, kvw, kvw, tm, [
        ("flat", 0, KV_A * HD, 0, 1.0, F32),
        ("flat", KV_A * HD, 2 * KV_A * HD, 0, 1.0, F32),
        ("flat", 2 * KV_A * HD, kvw, 0, 1.0, F32)])
    c = 1024
    tn = 512
    (q_a,) = _proj(h2d, w_pad, c, d_a, tn, tm, [("heads", 0, tn, HD, HD ** -0.5, BF16)])
    c += d_a
    (g_a,) = _proj(h2d, w_pad, c, d_a, tn, tm, [("flat", 0, tn, 0, 1.0, BF16)])
    c += d_a
    (q_idx,) = _proj(h2d, w_pad, c, H_IDX * D_IDX, tn, tm, [("heads", 0, tn, D_IDX, 1.0, BF16)])
    c += H_IDX * D_IDX
    (q_b,) = _proj(h2d, w_pad, c, d_b, tn, tm, [("heads", 0, tn, HD, HD ** -0.5, BF16)])
    c += d_b
    tn_b = tn
    kb_outs = [("flat", 0, tn, 0, 1.0, F32)]
    vb_outs = [("flat", 0, tn, 0, 1.0, F32)]
    if want_b16 and native:
        tn_b = SUBLANES * LANES
        kb_outs = [("native", 0, tn_b, 0, 1.0, F32), ("heads", 0, tn_b, HD, 1.0, BF16)]
        vb_outs = [("native", 0, tn_b, 0, 1.0, F32), ("lanes", 0, tn_b, 0, 1.0, BF16)]
    elif want_b16:
        kb_outs.append(("heads", 0, tn, HD, 1.0, BF16))
        vb_outs.append(("heads", 0, tn, 2 * HD, 1.0, BF16))
    k_b = _proj(h2d, w_pad, c, d_b, tn_b, tm, kb_outs)
    c += d_b
    v_b = _proj(h2d, w_pad, c, d_b, tn_b, tm, vb_outs)
    c += d_b
    (g_b,) = _proj(h2d, w_pad, c, d_b, tn, tm, [("flat", 0, tn, 0, 1.0, BF16)])
    return dict(k_a=k_a, v_a=v_a, kw=kw, q_a=q_a, g_a=g_a, q_idx=q_idx, q_b=q_b, k_b=k_b, v_b=v_b, g_b=g_b)


def _layer(x, mod, past, w_pad, w_out, lamp, subln, ln_g, ln_b, alpha, d_a, d_b, native):
    b, t, d = x.shape
    h_b = d_b // (2 * HD)
    m = b * t
    shift, scale, gate = (mod[:, i * d:(i + 1) * d] for i in range(3))
    prompt = past is None

    h = _modulate(x, shift.reshape(b, 1, d), scale.reshape(b, 1, d), tt=min(t, 256))
    p = _project(h.reshape(m, d), w_pad, d_a, d_b, tm=min(m, 1024), want_b16=prompt, native=native)

    r3 = lambda a: a.reshape(b, t, a.shape[-1])
    r4 = lambda a: a.reshape(a.shape[0], b, t, a.shape[-1])
    k_a, v_a, kw, g_a, g_b = r3(p["k_a"]), r3(p["v_a"]), r3(p["kw"]), r3(p["g_a"]), r3(p["g_b"])
    q_a, q_idx, q_b = r4(p["q_a"]), r4(p["q_idx"]), r4(p["q_b"])

    if prompt:
        topk = min(TOPK_MAX, t // 4)
        mix = _attn_a_prompt(q_a, k_a, v_a, kw, q_idx, g_a, topk, d_a + d_b)
        mix = _attn_b_prompt(q_b, r4(p["k_b"][1]), r4(p["v_b"][1]), g_b, lamp, subln, mix)
        tm_out, rows_per_gate, gate_rows = min(m, 512), t, gate
        new_k_b = p["k_b"][0].reshape(1, b, t, h_b, 2, HD)
        if native:
            new_v_b = p["v_b"][0].reshape(b, t, 2, h_b, HD).transpose(0, 1, 3, 2, 4).reshape(1, b, t, h_b, 2 * HD)
        else:
            new_v_b = p["v_b"][0].reshape(1, b, t, h_b, 2 * HD)
    else:
        kc_a, vc_a, kic, kc_b, vc_b = past
        plen = kic.shape[1]
        topk = min(TOPK_MAX, (plen + t) // 4)
        k_b, v_b = r3(p["k_b"][0]), r3(p["v_b"][0])
        rows = lambda a: a.reshape(-1, LANES)
        vc_b_rows = rows(vc_b.reshape(b, plen, h_b, 2, HD).transpose(0, 1, 3, 2, 4))
        mix = _attn_a_sample(q_a, rows(kc_a), rows(vc_a), kic, k_a, v_a, kw, q_idx, g_a, topk, d_a + d_b)
        mix = _attn_b_sample(q_b, rows(kc_b), vc_b_rows, k_b, v_b, g_b, lamp, subln, mix, plen, native)
        tm_out, rows_per_gate = m, 1
        gate_rows = jnp.broadcast_to(gate[:, None, :], (b, t, d)).reshape(m, d)
        new_k_b = k_b.reshape(1, b, t, h_b, 2, HD)
        if native:
            new_v_b = v_b.reshape(b, t, 2, h_b, HD).transpose(0, 1, 3, 2, 4).reshape(1, b, t, h_b, 2 * HD)
        else:
            new_v_b = v_b.reshape(1, b, t, h_b, 2 * HD)

    y = _out_proj(mix.reshape(m, d_a + d_b), w_out, x.reshape(m, d), gate_rows, ln_g, ln_b, alpha,
                  tm=tm_out, tn=512, rows_per_gate=rows_per_gate)
    rows_out = (k_a.reshape(1, b, t, KV_A, HD), v_a.reshape(1, b, t, KV_A, HD), kw[None, :, :, :D_IDX],
                new_k_b, new_v_b)
    return y.reshape(b, t, d), rows_out


def kernel(x_prompt, x_sample, cache_a_k, cache_a_v, cache_a_kidx, cache_b_k, cache_b_v, c_prompt, c_sample,
           w_ada, b_ada, w_in, w_out, lam_q1, lam_k1, lam_q2, lam_k2, subln_g, ln_g, ln_b):
    depth, d, _ = w_ada.shape
    assert depth == 1, "single-layer step"
    d_b = cache_b_v.shape[3] * cache_b_v.shape[4]
    d_a = w_out.shape[1] - d_b
    alpha = (2.0 * depth) ** 0.25
    bp = x_prompt.shape[0]

    mod = _ada(jnp.concatenate([c_prompt, c_sample], axis=0), w_ada[0], b_ada[0])
    native = d_b == 2 * SUBLANES * LANES
    w_pad = _pad_w_in(w_in[0], d_a, d_b, d_b, native)
    lamp = jnp.concatenate([lam_q1, lam_k1, lam_q2, lam_k2], axis=0)
    common = (w_pad, w_out[0].astype(BF16), lamp, subln_g, ln_g[0], ln_b[0], alpha, d_a, d_b, native)

    y_p, rows_p = _layer(x_prompt, mod[:bp], None, *common)
    past = (cache_a_k[0], cache_a_v[0], cache_a_kidx[0], cache_b_k[0], cache_b_v[0])
    y_s, rows_s = _layer(x_sample, mod[bp:], past, *common)
    return (y_p, y_s) + rows_p + rows_s
```

```python
import functools
import math

import jax
import jax.numpy as jnp
import numpy as np
from jax import lax
from jax.experimental import pallas as pl
from jax.experimental.pallas import tpu as pltpu

F32 = jnp.float32
BF16 = jnp.bfloat16

HD = 128
CHUNK = 64
CHUNK_SHIFT = 6
KV_A = 2
H_IDX = 16
D_IDX = 64
TOPK_MAX = 256
LN_EPS = 1e-5
LAM_INIT = 0.8 - 0.6 * math.exp(-0.3 * 0)

LOG2E = math.log2(math.e)
NEG = -1e30
N_BISECT = 16

VMEM_LIMIT = 56 * 1024 * 1024
LANES = 128
SUBLANES = 8


def _cparams(sem):
    return pltpu.CompilerParams(dimension_semantics=sem, vmem_limit_bytes=VMEM_LIMIT)


def _silu(x):
    return x * jax.nn.sigmoid(x)


def _dot_nt(a, b):
    return lax.dot_general(a, b, (((1,), (1,)), ((), ())), preferred_element_type=F32)


def _alibi_slopes(n):
    return jnp.asarray(2.0 ** (-8.0 * np.arange(1, n + 1) / n), dtype=F32) * LOG2E


def _ada_kernel(c_ref, w_ref, b_ref, o_ref):
    s = _silu(c_ref[...]).astype(BF16)
    o_ref[...] = jnp.dot(s, w_ref[...].astype(BF16), preferred_element_type=F32) + b_ref[...]


def _ada(c, w_ada, b_ada, tn=512):
    n, d = c.shape
    e = w_ada.shape[1]
    return pl.pallas_call(
        _ada_kernel,
        out_shape=jax.ShapeDtypeStruct((n, e), F32),
        grid=(e // tn,),
        in_specs=[pl.BlockSpec((n, d), lambda j: (0, 0)),
                  pl.BlockSpec((d, tn), lambda j: (0, j)),
                  pl.BlockSpec((1, tn), lambda j: (0, j))],
        out_specs=pl.BlockSpec((n, tn), lambda j: (0, j)),
        compiler_params=_cparams(("arbitrary",)),
        name="ada",
    )(c, w_ada, b_ada.reshape(1, e))


def _modproj_kernel(x_ref, shift_ref, scale_ref, w_ref, h_ref, *o_refs, splits):
    h = (x_ref[0] * (1.0 + scale_ref[0]) + shift_ref[0]).astype(h_ref.dtype)
    h_ref[...] = h
    acc = jnp.dot(h, w_ref[...], preferred_element_type=F32)
    for o_ref, (lo, hi) in zip(o_refs, splits):
        o_ref[...] = acc[:, lo:hi]


def _modulate_project(x, shift, scale, w_pad, splits, tm):
    b, t, d = x.shape
    n = splits[-1][1]
    nt = t // tm
    return pl.pallas_call(
        functools.partial(_modproj_kernel, splits=tuple(splits)),
        out_shape=[jax.ShapeDtypeStruct((b * t, d), BF16)]
        + [jax.ShapeDtypeStruct((b * t, hi - lo), F32) for lo, hi in splits],
        grid=(b, nt),
        in_specs=[pl.BlockSpec((1, tm, d), lambda i, j: (i, j, 0)),
                  pl.BlockSpec((1, 1, d), lambda i, j: (i, 0, 0)),
                  pl.BlockSpec((1, 1, d), lambda i, j: (i, 0, 0)),
                  pl.BlockSpec((d, n), lambda i, j: (0, 0))],
        out_specs=[pl.BlockSpec((tm, d), lambda i, j: (i * nt + j, 0))]
        + [pl.BlockSpec((tm, hi - lo), lambda i, j: (i * nt + j, 0)) for lo, hi in splits],
        compiler_params=_cparams(("arbitrary", "arbitrary")),
        name="modulate_proj",
    )(x, shift, scale, w_pad)


def _proj_kernel(h_ref, w_ref, *o_refs, outs):
    acc = jnp.dot(h_ref[...], w_ref[...], preferred_element_type=F32)
    for o_ref, (kind, lo, hi, width, scale) in zip(o_refs, outs):
        if kind == "flat":
            o_ref[...] = acc[:, lo:hi].astype(o_ref.dtype)
        elif kind == "heads":
            for hh in range((hi - lo) // width):
                blk = acc[:, lo + hh * width: lo + (hh + 1) * width]
                if scale != 1.0:
                    blk = blk * scale
                o_ref[hh] = blk.astype(o_ref.dtype)
        elif kind == "lanes":
            for hh in range((hi - lo) // LANES):
                o_ref[hh] = acc[:, lo + hh * LANES: lo + (hh + 1) * LANES].astype(o_ref.dtype)
        else:
            for s in range(SUBLANES):
                o_ref[:, s, :] = acc[:, lo + s * LANES: lo + (s + 1) * LANES].astype(o_ref.dtype)


def _proj(h2d, w_pad, col_start, ncols, tn, tm, outs):
    m, k = h2d.shape
    nj = ncols // tn
    assert col_start % tn == 0 and ncols % tn == 0 and m % tm == 0
    j0 = col_start // tn
    out_shapes, out_specs, kouts = [], [], []
    for kind, lo, hi, width, scale, dtype in outs:
        if kind == "flat":
            out_shapes.append(jax.ShapeDtypeStruct((m, nj * (hi - lo)), dtype))
            out_specs.append(pl.BlockSpec((tm, hi - lo), lambda i, j: (i, j)))
        elif kind == "heads":
            nh = (hi - lo) // width
            out_shapes.append(jax.ShapeDtypeStruct((nj * nh, m, width), dtype))
            out_specs.append(pl.BlockSpec((nh, tm, width), lambda i, j: (j, i, 0)))
        elif kind == "lanes":
            nh = (hi - lo) // LANES
            out_shapes.append(jax.ShapeDtypeStruct((nh, m, nj * LANES), dtype))
            out_specs.append(pl.BlockSpec((nh, tm, LANES), lambda i, j: (0, i, j)))
        else:
            assert hi - lo == SUBLANES * LANES
            out_shapes.append(jax.ShapeDtypeStruct((m, nj, SUBLANES, LANES), dtype))
            out_specs.append(pl.BlockSpec((tm, None, SUBLANES, LANES), lambda i, j: (i, j, 0, 0)))
        kouts.append((kind, lo, hi, width, scale))
    return pl.pallas_call(
        functools.partial(_proj_kernel, outs=tuple(kouts)),
        out_shape=out_shapes,
        grid=(m // tm, nj),
        in_specs=[pl.BlockSpec((tm, k), lambda i, j: (i, 0)),
                  pl.BlockSpec((k, tn), lambda i, j: (0, j0 + j))],
        out_specs=out_specs,
        compiler_params=_cparams(("arbitrary", "arbitrary")),
        name=f"proj_c{col_start}",
    )(h2d, w_pad)


def _kth_largest(load, ntiles, k, axis, shape):
    neg_inf = jnp.full(shape, -jnp.inf, F32)

    def reduce_tiles(fn, init):
        return lax.fori_loop(0, ntiles, lambda i, c: fn(load(i), c), init)

    def count_ge(t):
        return reduce_tiles(
            lambda x, c: c + jnp.sum(jnp.where(x >= t, 1.0, 0.0), axis=axis, keepdims=True),
            jnp.zeros(shape, F32))

    def max_below(t, strict):
        def fn(x, c):
            keep = (x < t) if strict else (x <= t)
            return jnp.maximum(c, jnp.max(jnp.where(keep, x, -jnp.inf), axis=axis, keepdims=True))
        return reduce_tiles(fn, neg_inf)

    def minmax(x, c):
        mn, mx = c
        mn = jnp.minimum(mn, jnp.min(jnp.where(x > -jnp.inf, x, jnp.inf), axis=axis, keepdims=True))
        mx = jnp.maximum(mx, jnp.max(x, axis=axis, keepdims=True))
        return mn, mx

    lo, hi = reduce_tiles(minmax, (jnp.full(shape, jnp.inf, F32), neg_inf))
    kf = float(k)

    def bisect(_, c):
        lo, hi = c
        mid = 0.5 * lo + 0.5 * hi
        ok = count_ge(mid) >= kf
        return jnp.where(ok, mid, lo), jnp.where(ok, hi, mid)

    lo, hi = lax.fori_loop(0, N_BISECT, bisect, (lo, hi))
    v = max_below(hi, strict=False)
    c = count_ge(v)

    def walk(vc):
        v, c = vc
        v = jnp.where(c < kf, max_below(v, strict=True), v)
        return v, count_ge(v)

    v, _ = lax.while_loop(lambda vc: jnp.min(vc[1]) < kf, walk, (v, c))
    return v


def _attn_a_kernel(slopes_ref, qa_ref, k_ref, v_ref, kw_ref, kwq_ref, qidx_ref, ga_ref, o_ref,
                   k16, vt16, ki16, st_scr, mb_scr, acc_scr, thr_scr, *, topk, n_heads):
    qi = pl.program_id(1)
    tq = qa_ref.shape[2]
    tk = tq
    rep = n_heads // KV_A
    nk = qi + 1

    @pl.when(qi == 0)
    def _():
        for g in range(KV_A):
            cols = slice(g * HD, (g + 1) * HD)
            k16[g] = k_ref[0, :, cols].astype(BF16)
            for j in range(vt16.shape[1]):
                vt16[g, j] = jnp.transpose(v_ref[0, j * tk:(j + 1) * tk, cols]).astype(BF16)
        ki16[...] = kw_ref[0, :, :D_IDX].astype(BF16)

    w_t = jnp.transpose(kwq_ref[0])
    w_sc = w_t[D_IDX:D_IDX + H_IDX, :] * (H_IDX ** -0.5)

    def score_tile(kj, c):
        ki = ki16[pl.ds(pl.multiple_of(kj * tk, tk), tk), :]
        acc = jnp.zeros((tk, tq), F32)
        for h in range(H_IDX):
            acc = acc + jnp.maximum(_dot_nt(ki, qidx_ref[h, 0]), 0.0) * w_sc[h:h + 1, :]
        st_scr[kj] = acc
        return c

    lax.fori_loop(0, nk, score_tile, 0)
    krow = lax.broadcasted_iota(jnp.int32, (tk, tq), 0)
    qcol = lax.broadcasted_iota(jnp.int32, (tk, tq), 1)
    st_scr[qi] = jnp.where((krow >> CHUNK_SHIFT) <= (qcol >> CHUNK_SHIFT), st_scr[qi], -jnp.inf)

    thr_scr[...] = jnp.full((1, tq), -3e38, F32)

    @pl.when(qi * tq + CHUNK >= topk)
    def _():
        thr_scr[...] = _kth_largest(lambda i: st_scr[i], nk, topk, 0, (1, tq))

    thr = thr_scr[...]

    def mask_tile(kj, c):
        mb_scr[kj] = jnp.where(st_scr[kj] >= thr, 0.0, NEG)
        return c

    lax.fori_loop(0, nk, mask_tile, 0)

    qk_diff = (qcol - krow).astype(F32)

    for g in range(KV_A):
        q_g = qa_ref[g * rep:(g + 1) * rep, 0].reshape(rep * tq, HD)
        acc_scr[...] = jnp.zeros(acc_scr.shape, F32)

        def kv_step(kj, carry):
            m_old, l_old = carry
            rows = pl.ds(pl.multiple_of(kj * tk, tk), tk)
            s_all = _dot_nt(k16[g, rows, :], q_g)
            dist = jnp.abs(qk_diff + ((qi - kj) * tk).astype(F32))
            mb = mb_scr[kj]
            s = jnp.concatenate(
                [s_all[:, hh * tq:(hh + 1) * tq] - slopes_ref[g * rep + hh] * dist + mb for hh in range(rep)],
                axis=1)
            m_new = jnp.maximum(m_old, jnp.max(s, axis=0, keepdims=True))
            alpha = jnp.exp2(m_old - m_new)
            p = jnp.exp2(s - m_new)
            l_new = alpha * l_old + jnp.sum(p, axis=0, keepdims=True)
            acc_scr[...] = alpha * acc_scr[...] + jnp.dot(vt16[g, kj], p.astype(BF16),
                                                          preferred_element_type=F32)
            return m_new, l_new

        _, l = lax.fori_loop(0, nk, kv_step,
                             (jnp.full((1, rep * tq), NEG, F32), jnp.zeros((1, rep * tq), F32)))
        o_t = acc_scr[...] / l
        for hh in range(rep):
            cols = slice((g * rep + hh) * HD, (g * rep + hh + 1) * HD)
            o = jnp.transpose(o_t[:, hh * tq:(hh + 1) * tq])
            o_ref[0, :, cols] = (o * _silu(ga_ref[0, :, cols].astype(F32))).astype(o_ref.dtype)


def _attn_a_prompt(qa, k, v, kw, qidx, ga, topk, d_mix):
    n_heads, b, t, _ = qa.shape
    tq = min(256, topk)
    assert t % tq == 0 and tq % CHUNK == 0 and tq <= topk <= tq + CHUNK
    nkt = t // tq
    d_a = n_heads * HD
    return pl.pallas_call(
        functools.partial(_attn_a_kernel, topk=topk, n_heads=n_heads),
        out_shape=jax.ShapeDtypeStruct((b, t, d_mix), BF16),
        grid=(b, nkt),
        in_specs=[pl.BlockSpec(memory_space=pltpu.SMEM),
                  pl.BlockSpec((n_heads, 1, tq, HD), lambda i, j: (0, i, j, 0)),
                  pl.BlockSpec((1, t, KV_A * HD), lambda i, j: (i, 0, 0)),
                  pl.BlockSpec((1, t, KV_A * HD), lambda i, j: (i, 0, 0)),
                  pl.BlockSpec((1, t, 128), lambda i, j: (i, 0, 0)),
                  pl.BlockSpec((1, tq, 128), lambda i, j: (i, j, 0)),
                  pl.BlockSpec((H_IDX, 1, tq, D_IDX), lambda i, j: (0, i, j, 0)),
                  pl.BlockSpec((1, tq, d_a), lambda i, j: (i, j, 0))],
        out_specs=pl.BlockSpec((1, tq, d_a), lambda i, j: (i, j, 0)),
        scratch_shapes=[pltpu.VMEM((KV_A, t, HD), BF16),
                        pltpu.VMEM((KV_A, nkt, HD, tq), BF16),
                        pltpu.VMEM((t, D_IDX), BF16),
                        pltpu.VMEM((nkt, tq, tq), F32),
                        pltpu.VMEM((nkt, tq, tq), F32),
                        pltpu.VMEM((HD, n_heads // KV_A * tq), F32),
                        pltpu.VMEM((1, tq), F32)],
        compiler_params=_cparams(("arbitrary", "arbitrary")),
        name="attn_a_prompt",
    )(_alibi_slopes(n_heads), qa, k, v, kw, kw, qidx, ga)


def _lam(lamp_ref):
    s1 = jnp.sum(lamp_ref[0:1, :] * lamp_ref[1:2, :], axis=1, keepdims=True)
    s2 = jnp.sum(lamp_ref[2:3, :] * lamp_ref[3:4, :], axis=1, keepdims=True)
    return jnp.exp(s1) - jnp.exp(s2) + LAM_INIT


def _subln_gate(o, subln, g):
    o = o * lax.rsqrt(jnp.mean(o * o, axis=-1, keepdims=True) + LN_EPS)
    o = o * subln * (1.0 - LAM_INIT)
    return o * _silu(g.astype(F32))


def _attn_b_kernel(slopes_ref, lamp_ref, subln_ref, q_ref, k_ref, v_ref, g_ref, mix_ref, o_ref,
                   vt_scr, s_scr, acc_scr, *, tq):
    h = pl.program_id(1)
    t = q_ref.shape[2]
    tk = tq
    nq = t // tq
    slope = slopes_ref[h]
    lam = _lam(lamp_ref)

    for j in range(nq):
        vt_scr[j] = jnp.transpose(v_ref[0, 0, j * tk:(j + 1) * tk, :].astype(F32)).astype(BF16)

    krow = lax.broadcasted_iota(jnp.int32, (tk, tq), 0)
    qcol = lax.broadcasted_iota(jnp.int32, (tk, tq), 1)
    nrc = -slope * (qcol - krow).astype(F32)
    diag_bias = jnp.where((krow >> CHUNK_SHIFT) <= (qcol >> CHUNK_SHIFT), -jnp.abs(nrc), NEG)

    pairs = [(qi, j) for qi in range(nq) for j in range(qi + 1)]

    def issue_logits(n):
        qi, j = pairs[n]
        for mi in range(2):
            s_scr[n % 2, mi] = _dot_nt(k_ref[mi, 0, j * tk:(j + 1) * tk, :], q_ref[mi, 0, qi * tq:(qi + 1) * tq, :])

    issue_logits(0)
    carry = None
    for n, (qi, j) in enumerate(pairs):
        if n + 1 < len(pairs):
            issue_logits(n + 1)
        if j == 0:
            acc_scr[...] = jnp.zeros(acc_scr.shape, F32)
            init = (jnp.full((1, tq), NEG, F32), jnp.zeros((1, tq), F32))
            carry = [init, init]
        bias = diag_bias if j == qi else nrc - slope * float((qi - j) * tk)
        for mi in range(2):
            m_old, l_old = carry[mi]
            s = s_scr[n % 2, mi] + bias
            m_new = jnp.maximum(m_old, jnp.max(s, axis=0, keepdims=True))
            alpha = jnp.exp2(m_old - m_new)
            p = jnp.exp2(s - m_new)
            l_new = alpha * l_old + jnp.sum(p, axis=0, keepdims=True)
            acc_scr[mi] = alpha * acc_scr[mi] + jnp.dot(vt_scr[j], p.astype(BF16), preferred_element_type=F32)
            carry[mi] = (m_new, l_new)
        if j == qi:
            o_t = acc_scr[0] / carry[0][1] - lam * (acc_scr[1] / carry[1][1])
            o_t = o_t * lax.rsqrt(jnp.mean(o_t * o_t, axis=0, keepdims=True) + LN_EPS)
            o = jnp.transpose(o_t) * subln_ref[...] * (1.0 - LAM_INIT)
            rows = slice(qi * tq, (qi + 1) * tq)
            o_ref[0, rows, :] = (o * _silu(g_ref[0, rows, :].astype(F32))).astype(o_ref.dtype)


def _attn_b_prompt(qb, kb16, vb16, gb, lamp, subln, mix):
    _, b, t, _ = qb.shape
    n_heads = vb16.shape[0]
    col0 = (mix.shape[2] - n_heads * 2 * HD) // (2 * HD)
    tq = min(512, t)
    assert t % tq == 0 and tq % CHUNK == 0
    return pl.pallas_call(
        functools.partial(_attn_b_kernel, tq=tq),
        out_shape=jax.ShapeDtypeStruct(mix.shape, mix.dtype),
        grid=(b, n_heads),
        in_specs=[pl.BlockSpec(memory_space=pltpu.SMEM),
                  pl.BlockSpec((4, HD), lambda i, h: (0, 0)),
                  pl.BlockSpec((1, 2 * HD), lambda i, h: (0, 0)),
                  pl.BlockSpec((2, 1, t, HD), lambda i, h: (h, i, 0, 0)),
                  pl.BlockSpec((2, 1, t, HD), lambda i, h: (h, i, 0, 0)),
                  pl.BlockSpec((1, 1, t, 2 * HD), lambda i, h: (h, i, 0, 0)),
                  pl.BlockSpec((1, t, 2 * HD), lambda i, h: (i, 0, h)),
                  pl.BlockSpec(memory_space=pl.ANY)],
        out_specs=pl.BlockSpec((1, t, 2 * HD), lambda i, h: (i, 0, col0 + h)),
        input_output_aliases={7: 0},
        scratch_shapes=[pltpu.VMEM((t // tq, 2 * HD, tq), BF16),
                        pltpu.VMEM((2, 2, tq, tq), F32),
                        pltpu.VMEM((2, 2 * HD, tq), F32)],
        compiler_params=_cparams(("arbitrary", "arbitrary")),
        name="attn_b_prompt",
    )(_alibi_slopes(n_heads), lamp, subln, qb, kb16, vb16, gb, mix)


def _attn_a_sample_kernel(slopes_ref, qa_ref, kc_ref, vc_ref, kic_ref, kn_ref, vn_ref, kwn_ref, qidx_ref,
                          ga_ref, o_ref, k16, v16, sc_scr, s_scr, p_scr, *, topk, n_heads, cw):
    t = qa_ref.shape[2]
    past = kc_ref.shape[0] // KV_A
    s_pad = sc_scr.shape[1]
    rep = n_heads // KV_A
    kcol = lax.broadcasted_iota(jnp.int32, (t, s_pad), 1)
    qrow = lax.broadcasted_iota(jnp.int32, (t, s_pad), 0)

    qi_all = qidx_ref[:, 0].reshape(H_IDX * t, D_IDX)
    w_q = kwn_ref[0, :, D_IDX:D_IDX + H_IDX] * (H_IDX ** -0.5)

    def scores(ki):
        r = jnp.maximum(_dot_nt(qi_all, ki), 0.0)
        acc = jnp.zeros((t, ki.shape[0]), F32)
        for h in range(H_IDX):
            acc = acc + r[h * t:(h + 1) * t, :] * w_q[:, h:h + 1]
        return acc

    for c0 in range(0, past, cw):
        sc_scr[:, c0:c0 + cw] = scores(kic_ref[0, c0:c0 + cw, :].astype(BF16))
    tail = s_pad - past
    ki_new = jnp.concatenate(
        [kwn_ref[0, :, :D_IDX], jnp.zeros((tail - t, D_IDX), F32)], axis=0).astype(BF16)
    tail_valid = lax.broadcasted_iota(jnp.int32, (t, tail), 1) < t
    sc_scr[:, past:] = jnp.where(tail_valid, scores(ki_new), -jnp.inf)

    thr = _kth_largest(lambda i: sc_scr[...], 1, topk, 1, (t, 1))
    mbias = jnp.where(sc_scr[...] >= thr, 0.0, NEG)
    dist = jnp.abs((past + qrow - kcol).astype(F32))

    for g in range(KV_A):
        cols = slice(g * HD, (g + 1) * HD)
        k16[:past] = kc_ref[pl.ds(g, past, stride=KV_A), :].astype(BF16)
        v16[:past] = vc_ref[pl.ds(g, past, stride=KV_A), :].astype(BF16)
        zpad = jnp.zeros((tail - t, HD), F32)
        k16[past:] = jnp.concatenate([kn_ref[0, :, cols], zpad], axis=0).astype(BF16)
        v16[past:] = jnp.concatenate([vn_ref[0, :, cols], zpad], axis=0).astype(BF16)
        q_g = qa_ref[g * rep:(g + 1) * rep, 0].reshape(rep * t, HD)
        s_scr[...] = _dot_nt(q_g, k16[...])
        inv_l = []
        for hh in range(rep):
            rows = slice(hh * t, (hh + 1) * t)
            s = s_scr[rows, :] - slopes_ref[g * rep + hh] * dist + mbias
            p = jnp.exp2(s - jnp.max(s, axis=1, keepdims=True))
            inv_l.append(1.0 / jnp.sum(p, axis=1, keepdims=True))
            p_scr[rows, :] = p.astype(BF16)
        o_g = jnp.dot(p_scr[...], v16[...], preferred_element_type=F32)
        for hh in range(rep):
            hc = slice((g * rep + hh) * HD, (g * rep + hh + 1) * HD)
            o = o_g[hh * t:(hh + 1) * t, :] * inv_l[hh]
            o_ref[0, :, hc] = (o * _silu(ga_ref[0, :, hc].astype(F32))).astype(o_ref.dtype)


def _attn_a_sample(qa, kc, vc, kic, kn, vn, kwn, qidx, ga, topk, d_mix, cw=512):
    n_heads, b, t, _ = qa.shape
    past = kic.shape[1]
    s_pad = past + 128
    assert t <= 128 and past % cw == 0
    d_a = n_heads * HD
    return pl.pallas_call(
        functools.partial(_attn_a_sample_kernel, topk=topk, n_heads=n_heads, cw=cw),
        out_shape=jax.ShapeDtypeStruct((b, t, d_mix), BF16),
        grid=(b,),
        in_specs=[pl.BlockSpec(memory_space=pltpu.SMEM),
                  pl.BlockSpec((n_heads, 1, t, HD), lambda i: (0, i, 0, 0)),
                  pl.BlockSpec((past * KV_A, HD), lambda i: (i, 0)),
                  pl.BlockSpec((past * KV_A, HD), lambda i: (i, 0)),
                  pl.BlockSpec((1, past, D_IDX), lambda i: (i, 0, 0)),
                  pl.BlockSpec((1, t, KV_A * HD), lambda i: (i, 0, 0)),
                  pl.BlockSpec((1, t, KV_A * HD), lambda i: (i, 0, 0)),
                  pl.BlockSpec((1, t, 128), lambda i: (i, 0, 0)),
                  pl.BlockSpec((H_IDX, 1, t, D_IDX), lambda i: (0, i, 0, 0)),
                  pl.BlockSpec((1, t, d_a), lambda i: (i, 0, 0))],
        out_specs=pl.BlockSpec((1, t, d_a), lambda i: (i, 0, 0)),
        scratch_shapes=[pltpu.VMEM((s_pad, HD), BF16),
                        pltpu.VMEM((s_pad, HD), BF16),
                        pltpu.VMEM((t, s_pad), F32),
                        pltpu.VMEM((n_heads // KV_A * t, s_pad), F32),
                        pltpu.VMEM((n_heads // KV_A * t, s_pad), BF16)],
        compiler_params=_cparams(("arbitrary",)),
        name="attn_a_sample",
    )(_alibi_slopes(n_heads), qa, kc, vc, kic, kn, vn, kwn, qidx, ga)


def _attn_b_sample_kernel(slopes_ref, lamp_ref, subln_ref, q_ref, kc_ref, vc_ref, kn_ref, vn_ref, g_ref, mix_ref,
                          o_ref, m_scr, l_scr, acc_scr, *, n_heads, past, v_half_major):
    c = pl.program_id(1)
    t = q_ref.shape[2]
    slots = 2 * n_heads
    pc = kc_ref.shape[0] // slots
    lam = _lam(lamp_ref)

    @pl.when(c == 0)
    def _():
        m_scr[...] = jnp.full(m_scr.shape, NEG, F32)
        l_scr[...] = jnp.zeros(l_scr.shape, F32)
        acc_scr[...] = jnp.zeros(acc_scr.shape, F32)

    def q_blockdiag(h):
        z = jnp.zeros((t, HD), BF16)
        return jnp.concatenate([jnp.concatenate([q_ref[2 * h, 0], z], axis=1),
                                jnp.concatenate([z, q_ref[2 * h + 1, 0]], axis=1)], axis=0)

    def update(h, s, v16):
        m_old = m_scr[h]
        m_new = jnp.maximum(m_old, jnp.max(s, axis=1, keepdims=True))
        alpha = jnp.exp2(m_old - m_new)
        p = jnp.exp2(s - m_new)
        l_scr[h] = alpha * l_scr[h] + jnp.sum(p, axis=1, keepdims=True)
        acc_scr[h] = alpha * acc_scr[h] + jnp.dot(p.astype(BF16), v16, preferred_element_type=F32)
        m_scr[h] = m_new

    qrow = lax.broadcasted_iota(jnp.int32, (t, pc), 0)
    kcol = lax.broadcasted_iota(jnp.int32, (t, pc), 1)
    dist = (past + qrow - kcol).astype(F32) - (c * pc).astype(F32)
    for h in range(n_heads):
        k01 = jnp.concatenate([kc_ref[pl.ds(2 * h, pc, stride=slots), :],
                               kc_ref[pl.ds(2 * h + 1, pc, stride=slots), :]], axis=1).astype(BF16)
        v_h = jnp.concatenate([vc_ref[pl.ds(h, pc, stride=slots), :],
                               vc_ref[pl.ds(n_heads + h, pc, stride=slots), :]], axis=1).astype(BF16)
        bias = -slopes_ref[h] * dist
        update(h, _dot_nt(q_blockdiag(h), k01) + jnp.concatenate([bias, bias], axis=0), v_h)

    @pl.when(c == pl.num_programs(1) - 1)
    def _():
        qr = lax.broadcasted_iota(jnp.int32, (t, LANES), 0)
        kc_ = lax.broadcasted_iota(jnp.int32, (t, LANES), 1)
        dist_new = jnp.abs((qr - kc_).astype(F32))
        zpad = jnp.zeros((LANES - t, 2 * HD), F32)
        for h in range(n_heads):
            cols = slice(h * 2 * HD, (h + 1) * 2 * HD)
            kn = jnp.concatenate([kn_ref[0, :, cols], zpad], axis=0).astype(BF16)
            if v_half_major:
                v_new = jnp.concatenate([vn_ref[0, :, (half * n_heads + h) * HD:(half * n_heads + h + 1) * HD]
                                         for half in range(2)], axis=1)
            else:
                v_new = vn_ref[0, :, cols]
            vn = jnp.concatenate([v_new, zpad], axis=0).astype(BF16)
            bias = jnp.where(kc_ < t, -slopes_ref[h] * dist_new, NEG)
            update(h, _dot_nt(q_blockdiag(h), kn) + jnp.concatenate([bias, bias], axis=0), vn)
            o_all = acc_scr[h] / l_scr[h]
            o = o_all[:t] - lam * o_all[t:]
            o_ref[0, :, cols] = _subln_gate(o, subln_ref[...], g_ref[0, :, cols]).astype(o_ref.dtype)


def _attn_b_sample(qb, kc, vc, kn, vn, gb, lamp, subln, mix, past, v_half_major, pc=1024):
    _, b, t, _ = qb.shape
    d_b = gb.shape[2]
    n_heads = d_b // (2 * HD)
    slots = 2 * n_heads
    pc = min(pc, past)
    assert t <= LANES and past % pc == 0 and (mix.shape[2] - d_b) % d_b == 0
    nc = past // pc
    col0 = (mix.shape[2] - d_b) // d_b
    return pl.pallas_call(
        functools.partial(_attn_b_sample_kernel, n_heads=n_heads, past=past, v_half_major=v_half_major),
        out_shape=jax.ShapeDtypeStruct(mix.shape, mix.dtype),
        grid=(b, nc),
        in_specs=[pl.BlockSpec(memory_space=pltpu.SMEM),
                  pl.BlockSpec((4, HD), lambda i, c: (0, 0)),
                  pl.BlockSpec((1, 2 * HD), lambda i, c: (0, 0)),
                  pl.BlockSpec((slots, 1, t, HD), lambda i, c: (0, i, 0, 0)),
                  pl.BlockSpec((pc * slots, LANES), lambda i, c: (i * nc + c, 0)),
                  pl.BlockSpec((pc * slots, LANES), lambda i, c: (i * nc + c, 0)),
                  pl.BlockSpec((1, t, d_b), lambda i, c: (i, 0, 0)),
                  pl.BlockSpec((1, t, d_b), lambda i, c: (i, 0, 0)),
                  pl.BlockSpec((1, t, d_b), lambda i, c: (i, 0, 0)),
                  pl.BlockSpec(memory_space=pl.ANY)],
        out_specs=pl.BlockSpec((1, t, d_b), lambda i, c: (i, 0, col0)),
        input_output_aliases={9: 0},
        scratch_shapes=[pltpu.VMEM((n_heads, 2 * t, 1), F32),
                        pltpu.VMEM((n_heads, 2 * t, 1), F32),
                        pltpu.VMEM((n_heads, 2 * t, 2 * HD), F32)],
        compiler_params=_cparams(("arbitrary", "arbitrary")),
        name="attn_b_sample",
    )(_alibi_slopes(n_heads), lamp, subln, qb, kc, vc, kn, vn, gb, mix)


def _out_kernel(mix_ref, w_ref, x_ref, gate_ref, lng_ref, lnb_ref, o_ref, *, alpha, nj):
    j = pl.program_id(1)
    tn = x_ref.shape[1]
    r = alpha * x_ref[...] + gate_ref[...] * jnp.dot(mix_ref[...], w_ref[...], preferred_element_type=F32)
    for jj in range(nj):
        @pl.when(j == jj)
        def _(jj=jj):
            o_ref[:, jj * tn:(jj + 1) * tn] = r

    @pl.when(j == nj - 1)
    def _():
        d = nj * tn
        tot = jnp.zeros((o_ref.shape[0], 1), F32)
        for c in range(nj):
            tot = tot + jnp.sum(o_ref[:, c * tn:(c + 1) * tn], axis=1, keepdims=True)
        mu = tot / d
        sq = jnp.zeros_like(tot)
        for c in range(nj):
            dv = o_ref[:, c * tn:(c + 1) * tn] - mu
            sq = sq + jnp.sum(dv * dv, axis=1, keepdims=True)
        inv = lax.rsqrt(sq / d + LN_EPS)
        for c in range(nj):
            cols = slice(c * tn, (c + 1) * tn)
            o_ref[:, cols] = (o_ref[:, cols] - mu) * inv * lng_ref[:, cols] + lnb_ref[:, cols]


def _out_proj(mix, w_out, x2d, gate, ln_g, ln_b, alpha, tm, tn, rows_per_gate):
    m, dm = mix.shape
    d = w_out.shape[1]
    assert m % tm == 0 and d % tn == 0
    if rows_per_gate == 1:
        gate_spec = pl.BlockSpec((tm, tn), lambda i, j: (i, j))
    else:
        assert rows_per_gate % tm == 0
        gate = gate.reshape(-1, 1, d)
        gate_spec = pl.BlockSpec((None, 1, tn), lambda i, j: (i * tm // rows_per_gate, 0, j))
    return pl.pallas_call(
        functools.partial(_out_kernel, alpha=alpha, nj=d // tn),
        out_shape=jax.ShapeDtypeStruct((m, d), F32),
        grid=(m // tm, d // tn),
        in_specs=[pl.BlockSpec((tm, dm), lambda i, j: (i, 0)),
                  pl.BlockSpec((dm, tn), lambda i, j: (0, j)),
                  pl.BlockSpec((tm, tn), lambda i, j: (i, j)),
                  gate_spec,
                  pl.BlockSpec((1, d), lambda i, j: (0, 0)),
                  pl.BlockSpec((1, d), lambda i, j: (0, 0))],
        out_specs=pl.BlockSpec((tm, d), lambda i, j: (i, 0), pipeline_mode=pl.Buffered(1)),
        compiler_params=_cparams(("arbitrary", "arbitrary")),
        name="out_proj",
    )(mix, w_out, x2d, gate, ln_g.reshape(1, d), ln_b.reshape(1, d))


def _w_in_layout(d_a, d_b, v_half_major):
    kv = KV_A * HD
    sizes = dict(q_a=d_a, k_a=kv, v_a=kv, g_a=d_a, q_idx=H_IDX * D_IDX, kw=D_IDX + H_IDX, q_b=d_b, k_b=d_b,
                 v_b=d_b, g_b=d_b)
    src, off = {}, 0
    for name in ("q_a", "k_a", "v_a", "g_a", "q_idx", "kw", "q_b", "k_b", "v_b", "g_b"):
        src[name] = off
        off += sizes[name]
    moves, dst = [], 0
    for name in ("k_a", "v_a", "kw"):
        moves.append((dst, src[name], sizes[name]))
        dst += sizes[name]
    zero = (dst, 1024)
    dst = 1024
    for name in ("q_a", "g_a", "q_idx", "q_b", "k_b"):
        moves.append((dst, src[name], sizes[name]))
        dst += sizes[name]
    n_b = d_b // (2 * HD)
    if v_half_major:
        for half in range(2):
            for h in range(n_b):
                moves.append((dst + (half * n_b + h) * HD, src["v_b"] + (h * 2 + half) * HD, HD))
    else:
        moves.append((dst, src["v_b"], d_b))
    dst += d_b
    moves.append((dst, src["g_b"], d_b))
    return moves, zero, dst + d_b


def _repack_kernel(w_ref, o_ref, *, moves, zero):
    n_in = w_ref.shape[1]
    o_ref[:, zero[0]:zero[1]] = jnp.zeros((o_ref.shape[0], zero[1] - zero[0]), o_ref.dtype)
    for dst, src, width in moves:
        if src % LANES == 0:
            o_ref[:, dst:dst + width] = w_ref[:, src:src + width].astype(o_ref.dtype)
            continue
        for c in range(0, width, LANES):
            lo = (src + c) // LANES * LANES
            window = w_ref[:, lo:min(lo + 2 * LANES, n_in)]
            o_ref[:, dst + c:dst + c + LANES] = window[:, src + c - lo:src + c - lo + LANES].astype(o_ref.dtype)


def _pad_w_in(w_in, d_a, d_b, v_half_major, tr=128):
    moves, zero, n_out = _w_in_layout(d_a, d_b, v_half_major)
    k, n_in = w_in.shape
    return pl.pallas_call(
        functools.partial(_repack_kernel, moves=tuple(moves), zero=zero),
        out_shape=jax.ShapeDtypeStruct((k, n_out), BF16),
        grid=(k // tr,),
        in_specs=[pl.BlockSpec((tr, n_in), lambda i: (i, 0))],
        out_specs=pl.BlockSpec((tr, n_out), lambda i: (i, 0)),
        compiler_params=_cparams(("arbitrary",)),
        name="repack_w_in",
    )(w_in)


def _project(x, shift, scale, w_pad, d_a, d_b, tm, want_b16, native):
    kv = KV_A * HD
    h2d, k_a, v_a, kw = _modulate_project(x, shift, scale, w_pad, [(0, kv), (kv, 2 * kv), (2 * kv, 2 * kv + 128)],
                                          tm=min(x.shape[1], 512))
    c = 1024
    tn = 512
    q_scale = HD ** -0.5 * LOG2E
    (q_a,) = _proj(h2d, w_pad, c, d_a, tn, tm, [("heads", 0, tn, HD, q_scale, BF16)])
    c += d_a
    (g_a,) = _proj(h2d, w_pad, c, d_a, tn, tm, [("flat", 0, tn, 0, 1.0, BF16)])
    c += d_a
    (q_idx,) = _proj(h2d, w_pad, c, H_IDX * D_IDX, tn, tm, [("heads", 0, tn, D_IDX, 1.0, BF16)])
    c += H_IDX * D_IDX
    (q_b,) = _proj(h2d, w_pad, c, d_b, tn, tm, [("heads", 0, tn, HD, q_scale, BF16)])
    c += d_b
    tn_b = tn
    kb_outs = [("flat", 0, tn, 0, 1.0, F32)]
    vb_outs = [("flat", 0, tn, 0, 1.0, F32)]
    if want_b16 and native:
        tn_b = SUBLANES * LANES
        kb_outs = [("native", 0, tn_b, 0, 1.0, F32), ("heads", 0, tn_b, HD, 1.0, BF16)]
        vb_outs = [("native", 0, tn_b, 0, 1.0, F32), ("lanes", 0, tn_b, 0, 1.0, BF16)]
    elif want_b16:
        kb_outs.append(("heads", 0, tn, HD, 1.0, BF16))
        vb_outs.append(("heads", 0, tn, 2 * HD, 1.0, BF16))
    k_b = _proj(h2d, w_pad, c, d_b, tn_b, tm, kb_outs)
    c += d_b
    v_b = _proj(h2d, w_pad, c, d_b, tn_b, tm, vb_outs)
    c += d_b
    (g_b,) = _proj(h2d, w_pad, c, d_b, tn, tm, [("flat", 0, tn, 0, 1.0, BF16)])
    return dict(k_a=k_a, v_a=v_a, kw=kw, q_a=q_a, g_a=g_a, q_idx=q_idx, q_b=q_b, k_b=k_b, v_b=v_b, g_b=g_b)


def _layer(x, mod, past, w_pad, w_out, lamp, subln, ln_g, ln_b, alpha, d_a, d_b, native):
    b, t, d = x.shape
    h_b = d_b // (2 * HD)
    m = b * t
    shift, scale, gate = (mod[:, i * d:(i + 1) * d] for i in range(3))
    prompt = past is None

    p = _project(x, shift.reshape(b, 1, d), scale.reshape(b, 1, d), w_pad, d_a, d_b, tm=min(m, 1024),
                 want_b16=prompt, native=native)

    r3 = lambda a: a.reshape(b, t, a.shape[-1])
    r4 = lambda a: a.reshape(a.shape[0], b, t, a.shape[-1])
    k_a, v_a, kw, g_a, g_b = r3(p["k_a"]), r3(p["v_a"]), r3(p["kw"]), r3(p["g_a"]), r3(p["g_b"])
    q_a, q_idx, q_b = r4(p["q_a"]), r4(p["q_idx"]), r4(p["q_b"])

    if prompt:
        topk = min(TOPK_MAX, t // 4)
        mix = _attn_a_prompt(q_a, k_a, v_a, kw, q_idx, g_a, topk, d_a + d_b)
        mix = _attn_b_prompt(q_b, r4(p["k_b"][1]), r4(p["v_b"][1]), g_b, lamp, subln, mix)
        tm_out, rows_per_gate, gate_rows = min(m, 1024), t, gate
        new_k_b = p["k_b"][0].reshape(1, b, t, h_b, 2, HD)
        if native:
            new_v_b = p["v_b"][0].reshape(b, t, 2, h_b, HD).transpose(0, 1, 3, 2, 4).reshape(1, b, t, h_b, 2 * HD)
        else:
            new_v_b = p["v_b"][0].reshape(1, b, t, h_b, 2 * HD)
    else:
        kc_a, vc_a, kic, kc_b, vc_b = past
        plen = kic.shape[1]
        topk = min(TOPK_MAX, (plen + t) // 4)
        k_b, v_b = r3(p["k_b"][0]), r3(p["v_b"][0])
        rows = lambda a: a.reshape(-1, LANES)
        vc_b_rows = rows(vc_b.reshape(b, plen, h_b, 2, HD).transpose(0, 1, 3, 2, 4))
        mix = _attn_a_sample(q_a, rows(kc_a), rows(vc_a), kic, k_a, v_a, kw, q_idx, g_a, topk, d_a + d_b)
        mix = _attn_b_sample(q_b, rows(kc_b), vc_b_rows, k_b, v_b, g_b, lamp, subln, mix, plen, native)
        tm_out, rows_per_gate = m, 1
        gate_rows = jnp.broadcast_to(gate[:, None, :], (b, t, d)).reshape(m, d)
        new_k_b = k_b.reshape(1, b, t, h_b, 2, HD)
        if native:
            new_v_b = v_b.reshape(b, t, 2, h_b, HD).transpose(0, 1, 3, 2, 4).reshape(1, b, t, h_b, 2 * HD)
        else:
            new_v_b = v_b.reshape(1, b, t, h_b, 2 * HD)

    y = _out_proj(mix.reshape(m, d_a + d_b), w_out, x.reshape(m, d), gate_rows, ln_g, ln_b, alpha,
                  tm=tm_out, tn=512, rows_per_gate=rows_per_gate)
    rows_out = (k_a.reshape(1, b, t, KV_A, HD), v_a.reshape(1, b, t, KV_A, HD), kw[None, :, :, :D_IDX],
                new_k_b, new_v_b)
    return y.reshape(b, t, d), rows_out


def kernel(x_prompt, x_sample, cache_a_k, cache_a_v, cache_a_kidx, cache_b_k, cache_b_v, c_prompt, c_sample,
           w_ada, b_ada, w_in, w_out, lam_q1, lam_k1, lam_q2, lam_k2, subln_g, ln_g, ln_b):
    depth, d, _ = w_ada.shape
    assert depth == 1, "single-layer step"
    d_b = cache_b_v.shape[3] * cache_b_v.shape[4]
    d_a = w_out.shape[1] - d_b
    alpha = (2.0 * depth) ** 0.25
    bp = x_prompt.shape[0]

    mod = _ada(jnp.concatenate([c_prompt, c_sample], axis=0), w_ada[0], b_ada[0])
    native = d_b == 2 * SUBLANES * LANES
    w_pad = _pad_w_in(w_in[0], d_a, d_b, native)
    lamp = jnp.concatenate([lam_q1, lam_k1, lam_q2, lam_k2], axis=0)
    common = (w_pad, w_out[0].astype(BF16), lamp, subln_g, ln_g[0], ln_b[0], alpha, d_a, d_b, native)

    y_p, rows_p = _layer(x_prompt, mod[:bp], None, *common)
    past = (cache_a_k[0], cache_a_v[0], cache_a_kidx[0], cache_b_k[0], cache_b_v[0])
    y_s, rows_s = _layer(x_sample, mod[bp:], past, *common)
    return (y_p, y_s) + rows_p + rows_s
```

```python
import functools
import math

import jax
import jax.numpy as jnp
import numpy as np
from jax import lax
from jax.experimental import pallas as pl
from jax.experimental.pallas import tpu as pltpu

F32 = jnp.float32
BF16 = jnp.bfloat16

HD = 128
CHUNK = 64
CHUNK_SHIFT = 6
KV_A = 2
H_IDX = 16
D_IDX = 64
TOPK_MAX = 256
LN_EPS = 1e-5
LAM_INIT = 0.8 - 0.6 * math.exp(-0.3 * 0)

LOG2E = math.log2(math.e)
NEG = -1e30
N_BISECT = 16

VMEM_LIMIT = 56 * 1024 * 1024
LANES = 128
SUBLANES = 8


def _cparams(sem):
    return pltpu.CompilerParams(dimension_semantics=sem, vmem_limit_bytes=VMEM_LIMIT)


def _silu(x):
    return x * jax.nn.sigmoid(x)


def _dot_nt(a, b):
    return lax.dot_general(a, b, (((1,), (1,)), ((), ())), preferred_element_type=F32)


def _alibi_slopes(n):
    return jnp.asarray(2.0 ** (-8.0 * np.arange(1, n + 1) / n), dtype=F32) * LOG2E


def _ada_kernel(c_ref, w_ref, b_ref, o_ref):
    s = _silu(c_ref[...]).astype(BF16)
    o_ref[...] = jnp.dot(s, w_ref[...].astype(BF16), preferred_element_type=F32) + b_ref[...]


def _ada(c, w_ada, b_ada, tn=512):
    n, d = c.shape
    e = w_ada.shape[1]
    return pl.pallas_call(
        _ada_kernel,
        out_shape=jax.ShapeDtypeStruct((n, e), F32),
        grid=(e // tn,),
        in_specs=[pl.BlockSpec((n, d), lambda j: (0, 0)),
                  pl.BlockSpec((d, tn), lambda j: (0, j)),
                  pl.BlockSpec((1, tn), lambda j: (0, j))],
        out_specs=pl.BlockSpec((n, tn), lambda j: (0, j)),
        compiler_params=_cparams(("arbitrary",)),
        name="ada",
    )(c, w_ada, b_ada.reshape(1, e))


def _modproj_kernel(x_ref, shift_ref, scale_ref, w_ref, h_ref, *o_refs, splits):
    h = (x_ref[0] * (1.0 + scale_ref[0]) + shift_ref[0]).astype(h_ref.dtype)
    h_ref[...] = h
    acc = jnp.dot(h, w_ref[...], preferred_element_type=F32)
    for o_ref, (lo, hi) in zip(o_refs, splits):
        o_ref[...] = acc[:, lo:hi]


def _modulate_project(x, shift, scale, w_pad, splits, tm):
    b, t, d = x.shape
    n = splits[-1][1]
    nt = t // tm
    return pl.pallas_call(
        functools.partial(_modproj_kernel, splits=tuple(splits)),
        out_shape=[jax.ShapeDtypeStruct((b * t, d), BF16)]
        + [jax.ShapeDtypeStruct((b * t, hi - lo), F32) for lo, hi in splits],
        grid=(b, nt),
        in_specs=[pl.BlockSpec((1, tm, d), lambda i, j: (i, j, 0)),
                  pl.BlockSpec((1, 1, d), lambda i, j: (i, 0, 0)),
                  pl.BlockSpec((1, 1, d), lambda i, j: (i, 0, 0)),
                  pl.BlockSpec((d, n), lambda i, j: (0, 0))],
        out_specs=[pl.BlockSpec((tm, d), lambda i, j: (i * nt + j, 0))]
        + [pl.BlockSpec((tm, hi - lo), lambda i, j: (i * nt + j, 0)) for lo, hi in splits],
        compiler_params=_cparams(("arbitrary", "arbitrary")),
        name="modulate_proj",
    )(x, shift, scale, w_pad)


def _proj_kernel(h_ref, w_ref, *o_refs, outs):
    acc = jnp.dot(h_ref[...], w_ref[...], preferred_element_type=F32)
    for o_ref, (kind, lo, hi, width, scale) in zip(o_refs, outs):
        if kind == "flat":
            o_ref[...] = acc[:, lo:hi].astype(o_ref.dtype)
        elif kind == "heads":
            for hh in range((hi - lo) // width):
                blk = acc[:, lo + hh * width: lo + (hh + 1) * width]
                if scale != 1.0:
                    blk = blk * scale
                o_ref[hh] = blk.astype(o_ref.dtype)
        elif kind == "lanes":
            for hh in range((hi - lo) // LANES):
                o_ref[hh] = acc[:, lo + hh * LANES: lo + (hh + 1) * LANES].astype(o_ref.dtype)
        else:
            for s in range(SUBLANES):
                o_ref[:, s, :] = acc[:, lo + s * LANES: lo + (s + 1) * LANES].astype(o_ref.dtype)


def _proj(h2d, w_pad, col_start, ncols, tn, tm, outs):
    m, k = h2d.shape
    nj = ncols // tn
    assert col_start % tn == 0 and ncols % tn == 0 and m % tm == 0
    j0 = col_start // tn
    out_shapes, out_specs, kouts = [], [], []
    for kind, lo, hi, width, scale, dtype in outs:
        if kind == "flat":
            out_shapes.append(jax.ShapeDtypeStruct((m, nj * (hi - lo)), dtype))
            out_specs.append(pl.BlockSpec((tm, hi - lo), lambda i, j: (i, j)))
        elif kind == "heads":
            nh = (hi - lo) // width
            out_shapes.append(jax.ShapeDtypeStruct((nj * nh, m, width), dtype))
            out_specs.append(pl.BlockSpec((nh, tm, width), lambda i, j: (j, i, 0)))
        elif kind == "lanes":
            nh = (hi - lo) // LANES
            out_shapes.append(jax.ShapeDtypeStruct((nh, m, nj * LANES), dtype))
            out_specs.append(pl.BlockSpec((nh, tm, LANES), lambda i, j: (0, i, j)))
        else:
            assert hi - lo == SUBLANES * LANES
            out_shapes.append(jax.ShapeDtypeStruct((m, nj, SUBLANES, LANES), dtype))
            out_specs.append(pl.BlockSpec((tm, None, SUBLANES, LANES), lambda i, j: (i, j, 0, 0)))
        kouts.append((kind, lo, hi, width, scale))
    return pl.pallas_call(
        functools.partial(_proj_kernel, outs=tuple(kouts)),
        out_shape=out_shapes,
        grid=(m // tm, nj),
        in_specs=[pl.BlockSpec((tm, k), lambda i, j: (i, 0)),
                  pl.BlockSpec((k, tn), lambda i, j: (0, j0 + j))],
        out_specs=out_specs,
        compiler_params=_cparams(("arbitrary", "arbitrary")),
        name=f"proj_c{col_start}",
    )(h2d, w_pad)


def _kth_largest(load, ntiles, k, axis, shape):
    neg_inf = jnp.full(shape, -jnp.inf, F32)

    def reduce_tiles(fn, init):
        return lax.fori_loop(0, ntiles, lambda i, c: fn(load(i), c), init)

    def count_ge(t):
        return reduce_tiles(
            lambda x, c: c + jnp.sum(jnp.where(x >= t, 1.0, 0.0), axis=axis, keepdims=True),
            jnp.zeros(shape, F32))

    def max_below(t, strict):
        def fn(x, c):
            keep = (x < t) if strict else (x <= t)
            return jnp.maximum(c, jnp.max(jnp.where(keep, x, -jnp.inf), axis=axis, keepdims=True))
        return reduce_tiles(fn, neg_inf)

    def minmax(x, c):
        mn, mx = c
        mn = jnp.minimum(mn, jnp.min(jnp.where(x > -jnp.inf, x, jnp.inf), axis=axis, keepdims=True))
        mx = jnp.maximum(mx, jnp.max(x, axis=axis, keepdims=True))
        return mn, mx

    lo, hi = reduce_tiles(minmax, (jnp.full(shape, jnp.inf, F32), neg_inf))
    kf = float(k)

    def bisect(_, c):
        lo, hi = c
        mid = 0.5 * lo + 0.5 * hi
        ok = count_ge(mid) >= kf
        return jnp.where(ok, mid, lo), jnp.where(ok, hi, mid)

    lo, hi = lax.fori_loop(0, N_BISECT, bisect, (lo, hi))
    v = max_below(hi, strict=False)
    c = count_ge(v)

    def walk(vc):
        v, c = vc
        v = jnp.where(c < kf, max_below(v, strict=True), v)
        return v, count_ge(v)

    v, _ = lax.while_loop(lambda vc: jnp.min(vc[1]) < kf, walk, (v, c))
    return v


def _slope_features(n):
    r = _alibi_slopes(n)
    parts = []
    for _ in range(3):
        p = r.astype(BF16).astype(F32)
        parts.append(p)
        r = r - p
    feat = jnp.stack([float(CHUNK) * p for p in parts] + parts, axis=1)
    return jnp.pad(feat, ((0, 0), (0, LANES - feat.shape[1])))


def _position_features(n_rows):
    pos = lax.broadcasted_iota(jnp.int32, (n_rows, HD), 0)
    lane = lax.broadcasted_iota(jnp.int32, (n_rows, HD), 1)
    feat = jnp.where(lane < 3, pos >> CHUNK_SHIFT, jnp.where(lane < 6, pos & (CHUNK - 1), 0))
    return feat.astype(F32).astype(BF16)


def _attn_a_kernel(slopes_ref, sfeat_ref, qa_ref, k_ref, v_ref, kw_ref, kwq_ref, qidx_ref, ga_ref, o_ref,
                   k16, vt16, ki16, st_scr, mb_scr, acc_scr, thr_scr, *, topk, n_heads):
    qi = pl.program_id(1)
    tq = qa_ref.shape[2]
    tk = tq
    rep = n_heads // KV_A
    nk = qi + 1

    @pl.when(qi == 0)
    def _():
        for g in range(KV_A):
            cols = slice(g * HD, (g + 1) * HD)
            k16[g, :, :HD] = k_ref[0, :, cols].astype(BF16)
            k16[g, :, HD:] = _position_features(k16.shape[1])
            for j in range(vt16.shape[1]):
                vt16[g, j] = jnp.transpose(v_ref[0, j * tk:(j + 1) * tk, cols]).astype(BF16)
        ki16[...] = kw_ref[0, :, :D_IDX].astype(BF16)

    w_t = jnp.transpose(kwq_ref[0])
    w_sc = w_t[D_IDX:D_IDX + H_IDX, :] * (H_IDX ** -0.5)

    def score_tile(kj, c):
        ki = ki16[pl.ds(pl.multiple_of(kj * tk, tk), tk), :]
        acc = jnp.zeros((tk, tq), F32)
        for h in range(H_IDX):
            acc = acc + jnp.maximum(_dot_nt(ki, qidx_ref[h, 0]), 0.0) * w_sc[h:h + 1, :]
        st_scr[kj] = acc
        return c

    lax.fori_loop(0, nk, score_tile, 0)
    krow = lax.broadcasted_iota(jnp.int32, (tk, tq), 0)
    qcol = lax.broadcasted_iota(jnp.int32, (tk, tq), 1)
    st_scr[qi] = jnp.where((krow >> CHUNK_SHIFT) <= (qcol >> CHUNK_SHIFT), st_scr[qi], -jnp.inf)

    thr_scr[...] = jnp.full((1, tq), -3e38, F32)

    @pl.when(qi * tq + CHUNK >= topk)
    def _():
        thr_scr[...] = _kth_largest(lambda i: st_scr[i], nk, topk, 0, (1, tq))

    thr = thr_scr[...]

    def mask_tile(kj, c):
        mb_scr[kj] = jnp.where(st_scr[kj] >= thr, 0.0, NEG)
        return c

    lax.fori_loop(0, nk, mask_tile, 0)

    ahead = jnp.maximum(krow - qcol, 0).astype(F32)

    for g in range(KV_A):
        q_g = jnp.concatenate(
            [qa_ref[g * rep:(g + 1) * rep, 0].reshape(rep * tq, HD),
             jnp.concatenate([jnp.broadcast_to(sfeat_ref[g * rep + hh:g * rep + hh + 1, :], (tq, LANES))
                              for hh in range(rep)], axis=0).astype(BF16)], axis=1)
        acc_scr[...] = jnp.zeros(acc_scr.shape, F32)

        def kv_step(kj, carry, diagonal=False):
            m_old, l_old = carry
            rows = pl.ds(pl.multiple_of(kj * tk, tk), tk)
            s_all = _dot_nt(k16[g, rows, :], q_g)
            mb = mb_scr[kj]
            s = jnp.concatenate(
                [s_all[:, hh * tq:(hh + 1) * tq]
                 + (mb - 2.0 * slopes_ref[g * rep + hh] * ahead if diagonal else mb) for hh in range(rep)],
                axis=1)
            m_new = jnp.maximum(m_old, jnp.max(s, axis=0, keepdims=True))
            alpha = jnp.exp2(m_old - m_new)
            p = jnp.exp2(s - m_new)
            l_new = alpha * l_old + jnp.sum(p, axis=0, keepdims=True)
            acc_scr[...] = alpha * acc_scr[...] + jnp.dot(vt16[g, kj], p.astype(BF16),
                                                          preferred_element_type=F32)
            return m_new, l_new

        carry = lax.fori_loop(0, qi, kv_step,
                              (jnp.full((1, rep * tq), NEG, F32), jnp.zeros((1, rep * tq), F32)))
        _, l = kv_step(qi, carry, diagonal=True)
        o_t = acc_scr[...] / l
        for hh in range(rep):
            cols = slice((g * rep + hh) * HD, (g * rep + hh + 1) * HD)
            o = jnp.transpose(o_t[:, hh * tq:(hh + 1) * tq])
            o_ref[0, :, cols] = (o * _silu(ga_ref[0, :, cols].astype(F32))).astype(o_ref.dtype)


def _attn_a_prompt(qa, k, v, kw, qidx, ga, topk, d_mix):
    n_heads, b, t, _ = qa.shape
    tq = min(256, topk)
    assert t % tq == 0 and tq % CHUNK == 0 and tq <= topk <= tq + CHUNK
    nkt = t // tq
    d_a = n_heads * HD
    return pl.pallas_call(
        functools.partial(_attn_a_kernel, topk=topk, n_heads=n_heads),
        out_shape=jax.ShapeDtypeStruct((b, t, d_mix), BF16),
        grid=(b, nkt),
        in_specs=[pl.BlockSpec(memory_space=pltpu.SMEM),
                  pl.BlockSpec((n_heads, LANES), lambda i, j: (0, 0)),
                  pl.BlockSpec((n_heads, 1, tq, HD), lambda i, j: (0, i, j, 0)),
                  pl.BlockSpec((1, t, KV_A * HD), lambda i, j: (i, 0, 0)),
                  pl.BlockSpec((1, t, KV_A * HD), lambda i, j: (i, 0, 0)),
                  pl.BlockSpec((1, t, 128), lambda i, j: (i, 0, 0)),
                  pl.BlockSpec((1, tq, 128), lambda i, j: (i, j, 0)),
                  pl.BlockSpec((H_IDX, 1, tq, D_IDX), lambda i, j: (0, i, j, 0)),
                  pl.BlockSpec((1, tq, d_a), lambda i, j: (i, j, 0))],
        out_specs=pl.BlockSpec((1, tq, d_a), lambda i, j: (i, j, 0)),
        scratch_shapes=[pltpu.VMEM((KV_A, t, 2 * HD), BF16),
                        pltpu.VMEM((KV_A, nkt, HD, tq), BF16),
                        pltpu.VMEM((t, D_IDX), BF16),
                        pltpu.VMEM((nkt, tq, tq), F32),
                        pltpu.VMEM((nkt, tq, tq), F32),
                        pltpu.VMEM((HD, n_heads // KV_A * tq), F32),
                        pltpu.VMEM((1, tq), F32)],
        compiler_params=_cparams(("arbitrary", "arbitrary")),
        name="attn_a_prompt",
    )(_alibi_slopes(n_heads), _slope_features(n_heads), qa, k, v, kw, kw, qidx, ga)


def _lam(lamp_ref):
    s1 = jnp.sum(lamp_ref[0:1, :] * lamp_ref[1:2, :], axis=1, keepdims=True)
    s2 = jnp.sum(lamp_ref[2:3, :] * lamp_ref[3:4, :], axis=1, keepdims=True)
    return jnp.exp(s1) - jnp.exp(s2) + LAM_INIT


def _subln_gate(o, subln, g):
    o = o * lax.rsqrt(jnp.mean(o * o, axis=-1, keepdims=True) + LN_EPS)
    o = o * subln * (1.0 - LAM_INIT)
    return o * _silu(g.astype(F32))


def _attn_b_kernel(slopes_ref, sfeat_ref, lamp_ref, subln_ref, q_ref, k_ref, v_ref, g_ref, mix_ref, o_ref,
                   vt_scr, s_scr, acc_scr, kaug_scr, qaug_scr, *, tq):
    h = pl.program_id(1)
    t = q_ref.shape[2]
    tk = tq
    nq = t // tq
    slope = slopes_ref[h]
    lam = _lam(lamp_ref)

    for j in range(nq):
        vt_scr[j] = jnp.transpose(v_ref[0, 0, j * tk:(j + 1) * tk, :].astype(F32)).astype(BF16)

    pos_feat = _position_features(t)
    slope_feat = jnp.broadcast_to(sfeat_ref[pl.ds(h, 1), :], (t, LANES)).astype(BF16)
    for mi in range(2):
        kaug_scr[mi, :, :HD] = k_ref[mi, 0]
        kaug_scr[mi, :, HD:] = pos_feat
        qaug_scr[mi, :, :HD] = q_ref[mi, 0]
        qaug_scr[mi, :, HD:] = slope_feat

    krow = lax.broadcasted_iota(jnp.int32, (tk, tq), 0)
    qcol = lax.broadcasted_iota(jnp.int32, (tk, tq), 1)
    ahead = jnp.maximum(krow - qcol, 0).astype(F32)
    diag_bias = jnp.where((krow >> CHUNK_SHIFT) <= (qcol >> CHUNK_SHIFT), -2.0 * slope * ahead, NEG)

    pairs = [(qi, j) for qi in range(nq) for j in range(qi + 1)]

    def issue_logits(n):
        qi, j = pairs[n]
        for mi in range(2):
            s_scr[n % 2, mi] = _dot_nt(kaug_scr[mi, j * tk:(j + 1) * tk, :], qaug_scr[mi, qi * tq:(qi + 1) * tq, :])

    issue_logits(0)
    carry = None
    for n, (qi, j) in enumerate(pairs):
        if n + 1 < len(pairs):
            issue_logits(n + 1)
        if j == 0:
            acc_scr[...] = jnp.zeros(acc_scr.shape, F32)
            init = (jnp.full((1, tq), NEG, F32), jnp.zeros((1, tq), F32))
            carry = [init, init]
        for mi in range(2):
            m_old, l_old = carry[mi]
            s = s_scr[n % 2, mi] + diag_bias if j == qi else s_scr[n % 2, mi]
            m_new = jnp.maximum(m_old, jnp.max(s, axis=0, keepdims=True))
            alpha = jnp.exp2(m_old - m_new)
            p = jnp.exp2(s - m_new)
            l_new = alpha * l_old + jnp.sum(p, axis=0, keepdims=True)
            acc_scr[mi] = alpha * acc_scr[mi] + jnp.dot(vt_scr[j], p.astype(BF16), preferred_element_type=F32)
            carry[mi] = (m_new, l_new)
        if j == qi:
            o_t = acc_scr[0] / carry[0][1] - lam * (acc_scr[1] / carry[1][1])
            o_t = o_t * lax.rsqrt(jnp.mean(o_t * o_t, axis=0, keepdims=True) + LN_EPS)
            o = jnp.transpose(o_t) * subln_ref[...] * (1.0 - LAM_INIT)
            rows = slice(qi * tq, (qi + 1) * tq)
            o_ref[0, rows, :] = (o * _silu(g_ref[0, rows, :].astype(F32))).astype(o_ref.dtype)


def _attn_b_prompt(qb, kb16, vb16, gb, lamp, subln, mix):
    _, b, t, _ = qb.shape
    n_heads = vb16.shape[0]
    col0 = (mix.shape[2] - n_heads * 2 * HD) // (2 * HD)
    tq = min(512, t)
    assert t % tq == 0 and tq % CHUNK == 0
    return pl.pallas_call(
        functools.partial(_attn_b_kernel, tq=tq),
        out_shape=jax.ShapeDtypeStruct(mix.shape, mix.dtype),
        grid=(b, n_heads),
        in_specs=[pl.BlockSpec(memory_space=pltpu.SMEM),
                  pl.BlockSpec((n_heads, LANES), lambda i, h: (0, 0)),
                  pl.BlockSpec((4, HD), lambda i, h: (0, 0)),
                  pl.BlockSpec((1, 2 * HD), lambda i, h: (0, 0)),
                  pl.BlockSpec((2, 1, t, HD), lambda i, h: (h, i, 0, 0)),
                  pl.BlockSpec((2, 1, t, HD), lambda i, h: (h, i, 0, 0)),
                  pl.BlockSpec((1, 1, t, 2 * HD), lambda i, h: (h, i, 0, 0)),
                  pl.BlockSpec((1, t, 2 * HD), lambda i, h: (i, 0, h)),
                  pl.BlockSpec(memory_space=pl.ANY)],
        out_specs=pl.BlockSpec((1, t, 2 * HD), lambda i, h: (i, 0, col0 + h)),
        input_output_aliases={8: 0},
        scratch_shapes=[pltpu.VMEM((t // tq, 2 * HD, tq), BF16),
                        pltpu.VMEM((2, 2, tq, tq), F32),
                        pltpu.VMEM((2, 2 * HD, tq), F32),
                        pltpu.VMEM((2, t, 2 * HD), BF16),
                        pltpu.VMEM((2, t, 2 * HD), BF16)],
        compiler_params=_cparams(("arbitrary", "arbitrary")),
        name="attn_b_prompt",
    )(_alibi_slopes(n_heads), _slope_features(n_heads), lamp, subln, qb, kb16, vb16, gb, mix)


def _attn_a_sample_kernel(slopes_ref, qa_ref, kc_ref, vc_ref, kic_ref, kn_ref, vn_ref, kwn_ref, qidx_ref,
                          ga_ref, o_ref, k16, v16, sc_scr, s_scr, p_scr, *, topk, n_heads, cw):
    t = qa_ref.shape[2]
    past = kc_ref.shape[0] // KV_A
    s_pad = sc_scr.shape[1]
    rep = n_heads // KV_A
    kcol = lax.broadcasted_iota(jnp.int32, (t, s_pad), 1)
    qrow = lax.broadcasted_iota(jnp.int32, (t, s_pad), 0)

    qi_all = qidx_ref[:, 0].reshape(H_IDX * t, D_IDX)
    w_q = kwn_ref[0, :, D_IDX:D_IDX + H_IDX] * (H_IDX ** -0.5)

    def scores(ki):
        r = jnp.maximum(_dot_nt(qi_all, ki), 0.0)
        acc = jnp.zeros((t, ki.shape[0]), F32)
        for h in range(H_IDX):
            acc = acc + r[h * t:(h + 1) * t, :] * w_q[:, h:h + 1]
        return acc

    for c0 in range(0, past, cw):
        sc_scr[:, c0:c0 + cw] = scores(kic_ref[0, c0:c0 + cw, :].astype(BF16))
    tail = s_pad - past
    ki_new = jnp.concatenate(
        [kwn_ref[0, :, :D_IDX], jnp.zeros((tail - t, D_IDX), F32)], axis=0).astype(BF16)
    tail_valid = lax.broadcasted_iota(jnp.int32, (t, tail), 1) < t
    sc_scr[:, past:] = jnp.where(tail_valid, scores(ki_new), -jnp.inf)

    thr = _kth_largest(lambda i: sc_scr[...], 1, topk, 1, (t, 1))
    mbias = jnp.where(sc_scr[...] >= thr, 0.0, NEG)
    dist = jnp.abs((past + qrow - kcol).astype(F32))

    for g in range(KV_A):
        cols = slice(g * HD, (g + 1) * HD)
        k16[:past] = kc_ref[pl.ds(g, past, stride=KV_A), :].astype(BF16)
        v16[:past] = vc_ref[pl.ds(g, past, stride=KV_A), :].astype(BF16)
        zpad = jnp.zeros((tail - t, HD), F32)
        k16[past:] = jnp.concatenate([kn_ref[0, :, cols], zpad], axis=0).astype(BF16)
        v16[past:] = jnp.concatenate([vn_ref[0, :, cols], zpad], axis=0).astype(BF16)
        q_g = qa_ref[g * rep:(g + 1) * rep, 0].reshape(rep * t, HD)
        s_scr[...] = _dot_nt(q_g, k16[...])
        inv_l = []
        for hh in range(rep):
            rows = slice(hh * t, (hh + 1) * t)
            s = s_scr[rows, :] - slopes_ref[g * rep + hh] * dist + mbias
            p = jnp.exp2(s - jnp.max(s, axis=1, keepdims=True))
            inv_l.append(1.0 / jnp.sum(p, axis=1, keepdims=True))
            p_scr[rows, :] = p.astype(BF16)
        o_g = jnp.dot(p_scr[...], v16[...], preferred_element_type=F32)
        for hh in range(rep):
            hc = slice((g * rep + hh) * HD, (g * rep + hh + 1) * HD)
            o = o_g[hh * t:(hh + 1) * t, :] * inv_l[hh]
            o_ref[0, :, hc] = (o * _silu(ga_ref[0, :, hc].astype(F32))).astype(o_ref.dtype)


def _attn_a_sample(qa, kc, vc, kic, kn, vn, kwn, qidx, ga, topk, d_mix, cw=512):
    n_heads, b, t, _ = qa.shape
    past = kic.shape[1]
    s_pad = past + 128
    assert t <= 128 and past % cw == 0
    d_a = n_heads * HD
    return pl.pallas_call(
        functools.partial(_attn_a_sample_kernel, topk=topk, n_heads=n_heads, cw=cw),
        out_shape=jax.ShapeDtypeStruct((b, t, d_mix), BF16),
        grid=(b,),
        in_specs=[pl.BlockSpec(memory_space=pltpu.SMEM),
                  pl.BlockSpec((n_heads, 1, t, HD), lambda i: (0, i, 0, 0)),
                  pl.BlockSpec((past * KV_A, HD), lambda i: (i, 0)),
                  pl.BlockSpec((past * KV_A, HD), lambda i: (i, 0)),
                  pl.BlockSpec((1, past, D_IDX), lambda i: (i, 0, 0)),
                  pl.BlockSpec((1, t, KV_A * HD), lambda i: (i, 0, 0)),
                  pl.BlockSpec((1, t, KV_A * HD), lambda i: (i, 0, 0)),
                  pl.BlockSpec((1, t, 128), lambda i: (i, 0, 0)),
                  pl.BlockSpec((H_IDX, 1, t, D_IDX), lambda i: (0, i, 0, 0)),
                  pl.BlockSpec((1, t, d_a), lambda i: (i, 0, 0))],
        out_specs=pl.BlockSpec((1, t, d_a), lambda i: (i, 0, 0)),
        scratch_shapes=[pltpu.VMEM((s_pad, HD), BF16),
                        pltpu.VMEM((s_pad, HD), BF16),
                        pltpu.VMEM((t, s_pad), F32),
                        pltpu.VMEM((n_heads // KV_A * t, s_pad), F32),
                        pltpu.VMEM((n_heads // KV_A * t, s_pad), BF16)],
        compiler_params=_cparams(("arbitrary",)),
        name="attn_a_sample",
    )(_alibi_slopes(n_heads), qa, kc, vc, kic, kn, vn, kwn, qidx, ga)


def _attn_b_sample_kernel(slopes_ref, lamp_ref, subln_ref, q_ref, kc_ref, vc_ref, kn_ref, vn_ref, g_ref, mix_ref,
                          o_ref, m_scr, l_scr, acc_scr, *, n_heads, past, v_half_major):
    c = pl.program_id(1)
    t = q_ref.shape[2]
    slots = 2 * n_heads
    pc = kc_ref.shape[0] // slots
    lam = _lam(lamp_ref)

    @pl.when(c == 0)
    def _():
        m_scr[...] = jnp.full(m_scr.shape, NEG, F32)
        l_scr[...] = jnp.zeros(l_scr.shape, F32)
        acc_scr[...] = jnp.zeros(acc_scr.shape, F32)

    def q_blockdiag(h):
        z = jnp.zeros((t, HD), BF16)
        return jnp.concatenate([jnp.concatenate([q_ref[2 * h, 0], z], axis=1),
                                jnp.concatenate([z, q_ref[2 * h + 1, 0]], axis=1)], axis=0)

    def update(h, s, v16):
        m_old = m_scr[h]
        m_new = jnp.maximum(m_old, jnp.max(s, axis=1, keepdims=True))
        alpha = jnp.exp2(m_old - m_new)
        p = jnp.exp2(s - m_new)
        l_scr[h] = alpha * l_scr[h] + jnp.sum(p, axis=1, keepdims=True)
        acc_scr[h] = alpha * acc_scr[h] + jnp.dot(p.astype(BF16), v16, preferred_element_type=F32)
        m_scr[h] = m_new

    qrow = lax.broadcasted_iota(jnp.int32, (t, pc), 0)
    kcol = lax.broadcasted_iota(jnp.int32, (t, pc), 1)
    dist = (past + qrow - kcol).astype(F32) - (c * pc).astype(F32)
    for h in range(n_heads):
        k01 = jnp.concatenate([kc_ref[pl.ds(2 * h, pc, stride=slots), :],
                               kc_ref[pl.ds(2 * h + 1, pc, stride=slots), :]], axis=1).astype(BF16)
        v_h = jnp.concatenate([vc_ref[pl.ds(h, pc, stride=slots), :],
                               vc_ref[pl.ds(n_heads + h, pc, stride=slots), :]], axis=1).astype(BF16)
        bias = -slopes_ref[h] * dist
        update(h, _dot_nt(q_blockdiag(h), k01) + jnp.concatenate([bias, bias], axis=0), v_h)

    @pl.when(c == pl.num_programs(1) - 1)
    def _():
        qr = lax.broadcasted_iota(jnp.int32, (t, LANES), 0)
        kc_ = lax.broadcasted_iota(jnp.int32, (t, LANES), 1)
        dist_new = jnp.abs((qr - kc_).astype(F32))
        zpad = jnp.zeros((LANES - t, 2 * HD), F32)
        for h in range(n_heads):
            cols = slice(h * 2 * HD, (h + 1) * 2 * HD)
            kn = jnp.concatenate([kn_ref[0, :, cols], zpad], axis=0).astype(BF16)
            if v_half_major:
                v_new = jnp.concatenate([vn_ref[0, :, (half * n_heads + h) * HD:(half * n_heads + h + 1) * HD]
                                         for half in range(2)], axis=1)
            else:
                v_new = vn_ref[0, :, cols]
            vn = jnp.concatenate([v_new, zpad], axis=0).astype(BF16)
            bias = jnp.where(kc_ < t, -slopes_ref[h] * dist_new, NEG)
            update(h, _dot_nt(q_blockdiag(h), kn) + jnp.concatenate([bias, bias], axis=0), vn)
            o_all = acc_scr[h] / l_scr[h]
            o = o_all[:t] - lam * o_all[t:]
            o_ref[0, :, cols] = _subln_gate(o, subln_ref[...], g_ref[0, :, cols]).astype(o_ref.dtype)


def _attn_b_sample(qb, kc, vc, kn, vn, gb, lamp, subln, mix, past, v_half_major, pc=1024):
    _, b, t, _ = qb.shape
    d_b = gb.shape[2]
    n_heads = d_b // (2 * HD)
    slots = 2 * n_heads
    pc = min(pc, past)
    assert t <= LANES and past % pc == 0 and (mix.shape[2] - d_b) % d_b == 0
    nc = past // pc
    col0 = (mix.shape[2] - d_b) // d_b
    return pl.pallas_call(
        functools.partial(_attn_b_sample_kernel, n_heads=n_heads, past=past, v_half_major=v_half_major),
        out_shape=jax.ShapeDtypeStruct(mix.shape, mix.dtype),
        grid=(b, nc),
        in_specs=[pl.BlockSpec(memory_space=pltpu.SMEM),
                  pl.BlockSpec((4, HD), lambda i, c: (0, 0)),
                  pl.BlockSpec((1, 2 * HD), lambda i, c: (0, 0)),
                  pl.BlockSpec((slots, 1, t, HD), lambda i, c: (0, i, 0, 0)),
                  pl.BlockSpec((pc * slots, LANES), lambda i, c: (i * nc + c, 0)),
                  pl.BlockSpec((pc * slots, LANES), lambda i, c: (i * nc + c, 0)),
                  pl.BlockSpec((1, t, d_b), lambda i, c: (i, 0, 0)),
                  pl.BlockSpec((1, t, d_b), lambda i, c: (i, 0, 0)),
                  pl.BlockSpec((1, t, d_b), lambda i, c: (i, 0, 0)),
                  pl.BlockSpec(memory_space=pl.ANY)],
        out_specs=pl.BlockSpec((1, t, d_b), lambda i, c: (i, 0, col0)),
        input_output_aliases={9: 0},
        scratch_shapes=[pltpu.VMEM((n_heads, 2 * t, 1), F32),
                        pltpu.VMEM((n_heads, 2 * t, 1), F32),
                        pltpu.VMEM((n_heads, 2 * t, 2 * HD), F32)],
        compiler_params=_cparams(("arbitrary", "arbitrary")),
        name="attn_b_sample",
    )(_alibi_slopes(n_heads), lamp, subln, qb, kc, vc, kn, vn, gb, mix)


def _out_kernel(mix_ref, w_ref, x_ref, gate_ref, lng_ref, lnb_ref, o_ref, *, alpha, nj):
    j = pl.program_id(1)
    tn = x_ref.shape[1]
    r = alpha * x_ref[...] + gate_ref[...] * jnp.dot(mix_ref[...], w_ref[...], preferred_element_type=F32)
    o_ref[:, pl.ds(pl.multiple_of(j * tn, tn), tn)] = r

    @pl.when(j == nj - 1)
    def _():
        d = nj * tn
        tot = jnp.zeros((o_ref.shape[0], 1), F32)
        for c in range(nj):
            tot = tot + jnp.sum(o_ref[:, c * tn:(c + 1) * tn], axis=1, keepdims=True)
        mu = tot / d
        sq = jnp.zeros_like(tot)
        for c in range(nj):
            dv = o_ref[:, c * tn:(c + 1) * tn] - mu
            sq = sq + jnp.sum(dv * dv, axis=1, keepdims=True)
        inv = lax.rsqrt(sq / d + LN_EPS)
        for c in range(nj):
            cols = slice(c * tn, (c + 1) * tn)
            o_ref[:, cols] = (o_ref[:, cols] - mu) * inv * lng_ref[:, cols] + lnb_ref[:, cols]


def _out_proj(mix, w_out, x2d, gate, ln_g, ln_b, alpha, tm, tn, rows_per_gate):
    m, dm = mix.shape
    d = w_out.shape[1]
    assert m % tm == 0 and d % tn == 0
    if rows_per_gate == 1:
        gate_spec = pl.BlockSpec((tm, tn), lambda i, j: (i, j))
    else:
        assert rows_per_gate % tm == 0
        gate = gate.reshape(-1, 1, d)
        gate_spec = pl.BlockSpec((None, 1, tn), lambda i, j: (i * tm // rows_per_gate, 0, j))
    return pl.pallas_call(
        functools.partial(_out_kernel, alpha=alpha, nj=d // tn),
        out_shape=jax.ShapeDtypeStruct((m, d), F32),
        grid=(m // tm, d // tn),
        in_specs=[pl.BlockSpec((tm, dm), lambda i, j: (i, 0)),
                  pl.BlockSpec((dm, tn), lambda i, j: (0, j)),
                  pl.BlockSpec((tm, tn), lambda i, j: (i, j)),
                  gate_spec,
                  pl.BlockSpec((1, d), lambda i, j: (0, 0)),
                  pl.BlockSpec((1, d), lambda i, j: (0, 0))],
        out_specs=pl.BlockSpec((tm, d), lambda i, j: (i, 0), pipeline_mode=pl.Buffered(1)),
        compiler_params=_cparams(("arbitrary", "arbitrary")),
        name="out_proj",
    )(mix, w_out, x2d, gate, ln_g.reshape(1, d), ln_b.reshape(1, d))


def _w_in_layout(d_a, d_b, v_half_major):
    kv = KV_A * HD
    sizes = dict(q_a=d_a, k_a=kv, v_a=kv, g_a=d_a, q_idx=H_IDX * D_IDX, kw=D_IDX + H_IDX, q_b=d_b, k_b=d_b,
                 v_b=d_b, g_b=d_b)
    src, off = {}, 0
    for name in ("q_a", "k_a", "v_a", "g_a", "q_idx", "kw", "q_b", "k_b", "v_b", "g_b"):
        src[name] = off
        off += sizes[name]
    moves, dst = [], 0
    for name in ("k_a", "v_a", "kw"):
        moves.append((dst, src[name], sizes[name]))
        dst += sizes[name]
    zero = (dst, 1024)
    dst = 1024
    for name in ("q_a", "g_a", "q_idx", "q_b", "k_b"):
        moves.append((dst, src[name], sizes[name]))
        dst += sizes[name]
    n_b = d_b // (2 * HD)
    if v_half_major:
        for half in range(2):
            for h in range(n_b):
                moves.append((dst + (half * n_b + h) * HD, src["v_b"] + (h * 2 + half) * HD, HD))
    else:
        moves.append((dst, src["v_b"], d_b))
    dst += d_b
    moves.append((dst, src["g_b"], d_b))
    return moves, zero, dst + d_b


def _repack_kernel(src_ref, w_ref, o_ref):
    o_ref[...] = jnp.transpose(w_ref[...]).astype(o_ref.dtype)


ROW_ALIGN = 16


def _pad_w_in(w_in_t, d_a, d_b, v_half_major):
    moves, _, n_out = _w_in_layout(d_a, d_b, v_half_major)
    n_in, k = w_in_t.shape
    src_rows = np.zeros(n_out // LANES, np.int32)
    for dst, src, width in moves:
        for c in range(0, width, LANES):
            assert dst % LANES == 0 and (src + c) % ROW_ALIGN == 0 and src + c + LANES <= n_in
            src_rows[(dst + c) // LANES] = (src + c) // ROW_ALIGN
    return pl.pallas_call(
        _repack_kernel,
        out_shape=jax.ShapeDtypeStruct((k, n_out), BF16),
        grid_spec=pltpu.PrefetchScalarGridSpec(
            num_scalar_prefetch=1,
            grid=(n_out // LANES,),
            in_specs=[pl.BlockSpec((pl.Element(LANES), pl.Element(k)),
                                   lambda i, src: (src[i] * ROW_ALIGN, 0))],
            out_specs=pl.BlockSpec((k, LANES), lambda i, src: (0, i))),
        compiler_params=_cparams(("arbitrary",)),
        name="repack_w_in",
    )(jnp.asarray(src_rows), w_in_t)


def _project(x, shift, scale, w_pad, d_a, d_b, tm, want_b16, native):
    kv = KV_A * HD
    h2d, k_a, v_a, kw = _modulate_project(x, shift, scale, w_pad, [(0, kv), (kv, 2 * kv), (2 * kv, 2 * kv + 128)],
                                          tm=min(x.shape[1], 512))
    c = 1024
    tn = 512
    q_scale = HD ** -0.5 * LOG2E
    (q_a,) = _proj(h2d, w_pad, c, d_a, tn, tm, [("heads", 0, tn, HD, q_scale, BF16)])
    c += d_a
    (g_a,) = _proj(h2d, w_pad, c, d_a, tn, tm, [("flat", 0, tn, 0, 1.0, BF16)])
    c += d_a
    (q_idx,) = _proj(h2d, w_pad, c, H_IDX * D_IDX, tn, tm, [("heads", 0, tn, D_IDX, 1.0, BF16)])
    c += H_IDX * D_IDX
    (q_b,) = _proj(h2d, w_pad, c, d_b, tn, tm, [("heads", 0, tn, HD, q_scale, BF16)])
    c += d_b
    tn_b = tn
    kb_outs = [("flat", 0, tn, 0, 1.0, F32)]
    vb_outs = [("flat", 0, tn, 0, 1.0, F32)]
    if want_b16 and native:
        tn_b = SUBLANES * LANES
        kb_outs = [("native", 0, tn_b, 0, 1.0, F32), ("heads", 0, tn_b, HD, 1.0, BF16)]
        vb_outs = [("native", 0, tn_b, 0, 1.0, F32), ("lanes", 0, tn_b, 0, 1.0, BF16)]
    elif want_b16:
        kb_outs.append(("heads", 0, tn, HD, 1.0, BF16))
        vb_outs.append(("heads", 0, tn, 2 * HD, 1.0, BF16))
    k_b = _proj(h2d, w_pad, c, d_b, tn_b, tm, kb_outs)
    c += d_b
    v_b = _proj(h2d, w_pad, c, d_b, tn_b, tm, vb_outs)
    c += d_b
    (g_b,) = _proj(h2d, w_pad, c, d_b, tn, tm, [("flat", 0, tn, 0, 1.0, BF16)])
    return dict(k_a=k_a, v_a=v_a, kw=kw, q_a=q_a, g_a=g_a, q_idx=q_idx, q_b=q_b, k_b=k_b, v_b=v_b, g_b=g_b)


def _layer(x, mod, past, w_pad, w_out, lamp, subln, ln_g, ln_b, alpha, d_a, d_b, native):
    b, t, d = x.shape
    h_b = d_b // (2 * HD)
    m = b * t
    shift, scale, gate = (mod[:, i * d:(i + 1) * d] for i in range(3))
    prompt = past is None

    p = _project(x, shift.reshape(b, 1, d), scale.reshape(b, 1, d), w_pad, d_a, d_b, tm=min(m, 1024),
                 want_b16=prompt, native=native)

    r3 = lambda a: a.reshape(b, t, a.shape[-1])
    r4 = lambda a: a.reshape(a.shape[0], b, t, a.shape[-1])
    k_a, v_a, kw, g_a, g_b = r3(p["k_a"]), r3(p["v_a"]), r3(p["kw"]), r3(p["g_a"]), r3(p["g_b"])
    q_a, q_idx, q_b = r4(p["q_a"]), r4(p["q_idx"]), r4(p["q_b"])

    if prompt:
        topk = min(TOPK_MAX, t // 4)
        mix = _attn_a_prompt(q_a, k_a, v_a, kw, q_idx, g_a, topk, d_a + d_b)
        mix = _attn_b_prompt(q_b, r4(p["k_b"][1]), r4(p["v_b"][1]), g_b, lamp, subln, mix)
        tm_out, rows_per_gate, gate_rows = min(m, 1024), t, gate
        new_k_b = p["k_b"][0].reshape(1, b, t, h_b, 2, HD)
        if native:
            new_v_b = p["v_b"][0].reshape(b, t, 2, h_b, HD).transpose(0, 1, 3, 2, 4).reshape(1, b, t, h_b, 2 * HD)
        else:
            new_v_b = p["v_b"][0].reshape(1, b, t, h_b, 2 * HD)
    else:
        kc_a, vc_a, kic, kc_b, vc_b = past
        plen = kic.shape[1]
        topk = min(TOPK_MAX, (plen + t) // 4)
        k_b, v_b = r3(p["k_b"][0]), r3(p["v_b"][0])
        rows = lambda a: a.reshape(-1, LANES)
        vc_b_rows = rows(vc_b.reshape(b, plen, h_b, 2, HD).transpose(0, 1, 3, 2, 4))
        mix = _attn_a_sample(q_a, rows(kc_a), rows(vc_a), kic, k_a, v_a, kw, q_idx, g_a, topk, d_a + d_b)
        mix = _attn_b_sample(q_b, rows(kc_b), vc_b_rows, k_b, v_b, g_b, lamp, subln, mix, plen, native)
        tm_out, rows_per_gate = m, 1
        gate_rows = jnp.broadcast_to(gate[:, None, :], (b, t, d)).reshape(m, d)
        new_k_b = k_b.reshape(1, b, t, h_b, 2, HD)
        if native:
            new_v_b = v_b.reshape(b, t, 2, h_b, HD).transpose(0, 1, 3, 2, 4).reshape(1, b, t, h_b, 2 * HD)
        else:
            new_v_b = v_b.reshape(1, b, t, h_b, 2 * HD)

    y = _out_proj(mix.reshape(m, d_a + d_b), w_out, x.reshape(m, d), gate_rows, ln_g, ln_b, alpha,
                  tm=tm_out, tn=512, rows_per_gate=rows_per_gate)
    rows_out = (k_a.reshape(1, b, t, KV_A, HD), v_a.reshape(1, b, t, KV_A, HD), kw[None, :, :, :D_IDX],
                new_k_b, new_v_b)
    return y.reshape(b, t, d), rows_out


def kernel(x_prompt, x_sample, cache_a_k, cache_a_v, cache_a_kidx, cache_b_k, cache_b_v, c_prompt, c_sample,
           w_ada, b_ada, w_in, w_out, lam_q1, lam_k1, lam_q2, lam_k2, subln_g, ln_g, ln_b):
    depth, d, _ = w_ada.shape
    assert depth == 1, "single-layer step"
    d_b = cache_b_v.shape[3] * cache_b_v.shape[4]
    d_a = w_out.shape[1] - d_b
    alpha = (2.0 * depth) ** 0.25
    bp = x_prompt.shape[0]

    mod = _ada(jnp.concatenate([c_prompt, c_sample], axis=0), w_ada[0], b_ada[0])
    native = d_b == 2 * SUBLANES * LANES
    w_pad = _pad_w_in(jnp.transpose(w_in[0]), d_a, d_b, native)
    lamp = jnp.concatenate([lam_q1, lam_k1, lam_q2, lam_k2], axis=0)
    common = (w_pad, w_out[0].astype(BF16), lamp, subln_g, ln_g[0], ln_b[0], alpha, d_a, d_b, native)

    y_p, rows_p = _layer(x_prompt, mod[:bp], None, *common)
    past = (cache_a_k[0], cache_a_v[0], cache_a_kidx[0], cache_b_k[0], cache_b_v[0])
    y_s, rows_s = _layer(x_sample, mod[bp:], past, *common)
    return (y_p, y_s) + rows_p + rows_s
```

```python
import functools
import math

import jax
import jax.numpy as jnp
import numpy as np
from jax import lax
from jax.experimental import pallas as pl
from jax.experimental.pallas import tpu as pltpu

F32 = jnp.float32
BF16 = jnp.bfloat16

HD = 128
CHUNK = 64
CHUNK_SHIFT = 6
KV_A = 2
H_IDX = 16
D_IDX = 64
TOPK_MAX = 256
LN_EPS = 1e-5
LAM_INIT = 0.8 - 0.6 * math.exp(-0.3 * 0)

LOG2E = math.log2(math.e)
NEG = -1e30
N_BISECT = 16

VMEM_LIMIT = 56 * 1024 * 1024
MXU_DIM = 256
LANES = 128
SUBLANES = 8


def _cparams(sem):
    return pltpu.CompilerParams(dimension_semantics=sem, vmem_limit_bytes=VMEM_LIMIT)


def _silu(x):
    return x * jax.nn.sigmoid(x)


def _dot_nt(a, b):
    return lax.dot_general(a, b, (((1,), (1,)), ((), ())), preferred_element_type=F32)


def _alibi_slopes(n):
    return jnp.asarray(2.0 ** (-8.0 * np.arange(1, n + 1) / n), dtype=F32) * LOG2E


def _ada_kernel(c_ref, w_ref, b_ref, o_ref):
    s = _silu(c_ref[...]).astype(BF16)
    o_ref[...] = jnp.dot(s, w_ref[...].astype(BF16), preferred_element_type=F32) + b_ref[...]


def _ada(c, w_ada, b_ada, tn=512):
    n, d = c.shape
    e = w_ada.shape[1]
    return pl.pallas_call(
        _ada_kernel,
        out_shape=jax.ShapeDtypeStruct((n, e), F32),
        grid=(e // tn,),
        in_specs=[pl.BlockSpec((n, d), lambda j: (0, 0)),
                  pl.BlockSpec((d, tn), lambda j: (0, j)),
                  pl.BlockSpec((1, tn), lambda j: (0, j))],
        out_specs=pl.BlockSpec((n, tn), lambda j: (0, j)),
        compiler_params=_cparams(("arbitrary",)),
        name="ada",
    )(c, w_ada, b_ada.reshape(1, e))


def _modproj_kernel(x_ref, shift_ref, scale_ref, w_ref, h_ref, *o_refs, splits):
    h = (x_ref[0] * (1.0 + scale_ref[0]) + shift_ref[0]).astype(h_ref.dtype)
    h_ref[...] = h
    acc = jnp.dot(h, w_ref[...], preferred_element_type=F32)
    for o_ref, (lo, hi) in zip(o_refs, splits):
        o_ref[...] = acc[:, lo:hi]


def _modulate_project(x, shift, scale, w_pad, splits, tm):
    b, t, d = x.shape
    n = splits[-1][1]
    nt = t // tm
    return pl.pallas_call(
        functools.partial(_modproj_kernel, splits=tuple(splits)),
        out_shape=[jax.ShapeDtypeStruct((b * t, d), BF16)]
        + [jax.ShapeDtypeStruct((b * t, hi - lo), F32) for lo, hi in splits],
        grid=(b, nt),
        in_specs=[pl.BlockSpec((1, tm, d), lambda i, j: (i, j, 0)),
                  pl.BlockSpec((1, 1, d), lambda i, j: (i, 0, 0)),
                  pl.BlockSpec((1, 1, d), lambda i, j: (i, 0, 0)),
                  pl.BlockSpec((d, n), lambda i, j: (0, 0))],
        out_specs=[pl.BlockSpec((tm, d), lambda i, j: (i * nt + j, 0))]
        + [pl.BlockSpec((tm, hi - lo), lambda i, j: (i * nt + j, 0)) for lo, hi in splits],
        compiler_params=_cparams(("arbitrary", "arbitrary")),
        name="modulate_proj",
    )(x, shift, scale, w_pad)


def _proj_kernel(h_ref, w_ref, *o_refs, outs, stream_weights):
    if stream_weights:
        acc = jnp.transpose(_dot_nt(jnp.transpose(w_ref[...]), h_ref[...]))
    else:
        acc = jnp.dot(h_ref[...], w_ref[...], preferred_element_type=F32)
    for o_ref, (kind, lo, hi, width, scale) in zip(o_refs, outs):
        if kind == "flat":
            o_ref[...] = acc[:, lo:hi].astype(o_ref.dtype)
        elif kind == "heads":
            for hh in range((hi - lo) // width):
                blk = acc[:, lo + hh * width: lo + (hh + 1) * width]
                if scale != 1.0:
                    blk = blk * scale
                o_ref[hh] = blk.astype(o_ref.dtype)
        elif kind == "lanes":
            for hh in range((hi - lo) // LANES):
                o_ref[hh] = acc[:, lo + hh * LANES: lo + (hh + 1) * LANES].astype(o_ref.dtype)
        else:
            for s in range(SUBLANES):
                o_ref[:, s, :] = acc[:, lo + s * LANES: lo + (s + 1) * LANES].astype(o_ref.dtype)


def _proj(h2d, w_pad, col_start, ncols, tn, tm, outs):
    m, k = h2d.shape
    nj = ncols // tn
    assert col_start % tn == 0 and ncols % tn == 0 and m % tm == 0
    j0 = col_start // tn
    out_shapes, out_specs, kouts = [], [], []
    for kind, lo, hi, width, scale, dtype in outs:
        if kind == "flat":
            out_shapes.append(jax.ShapeDtypeStruct((m, nj * (hi - lo)), dtype))
            out_specs.append(pl.BlockSpec((tm, hi - lo), lambda i, j: (i, j)))
        elif kind == "heads":
            nh = (hi - lo) // width
            out_shapes.append(jax.ShapeDtypeStruct((nj * nh, m, width), dtype))
            out_specs.append(pl.BlockSpec((nh, tm, width), lambda i, j: (j, i, 0)))
        elif kind == "lanes":
            nh = (hi - lo) // LANES
            out_shapes.append(jax.ShapeDtypeStruct((nh, m, nj * LANES), dtype))
            out_specs.append(pl.BlockSpec((nh, tm, LANES), lambda i, j: (0, i, j)))
        else:
            assert hi - lo == SUBLANES * LANES
            out_shapes.append(jax.ShapeDtypeStruct((m, nj, SUBLANES, LANES), dtype))
            out_specs.append(pl.BlockSpec((tm, None, SUBLANES, LANES), lambda i, j: (i, j, 0, 0)))
        kouts.append((kind, lo, hi, width, scale))
    return pl.pallas_call(
        functools.partial(_proj_kernel, outs=tuple(kouts), stream_weights=m <= MXU_DIM),
        out_shape=out_shapes,
        grid=(m // tm, nj),
        in_specs=[pl.BlockSpec((tm, k), lambda i, j: (i, 0)),
                  pl.BlockSpec((k, tn), lambda i, j: (0, j0 + j))],
        out_specs=out_specs,
        compiler_params=_cparams(("arbitrary", "arbitrary")),
        name=f"proj_c{col_start}",
    )(h2d, w_pad)


def _kth_largest(load, ntiles, k, axis, shape):
    part = (SUBLANES, shape[1]) if axis == 0 else shape

    def fold(x, op):
        if axis == 0:
            return op(x.reshape(x.shape[0] // SUBLANES, SUBLANES, x.shape[1]), axis=0)
        return op(x, axis=axis, keepdims=True)

    def finish(c, op):
        return op(c, axis=0, keepdims=True) if axis == 0 else c

    def reduce_tiles(fn, init):
        return lax.fori_loop(0, ntiles, lambda i, c: fn(load(i), c), init)

    def count_ge(t):
        c = reduce_tiles(lambda x, c: c + fold(jnp.where(x >= t, 1.0, 0.0), jnp.sum), jnp.zeros(part, F32))
        return finish(c, jnp.sum)

    def max_below(t, strict):
        def fn(x, c):
            keep = (x < t) if strict else (x <= t)
            return jnp.maximum(c, fold(jnp.where(keep, x, -jnp.inf), jnp.max))
        return finish(reduce_tiles(fn, jnp.full(part, -jnp.inf, F32)), jnp.max)

    def minmax(x, c):
        mn, mx = c
        mn = jnp.minimum(mn, fold(jnp.where(x > -jnp.inf, x, jnp.inf), jnp.min))
        mx = jnp.maximum(mx, fold(x, jnp.max))
        return mn, mx

    lo, hi = reduce_tiles(minmax, (jnp.full(part, jnp.inf, F32), jnp.full(part, -jnp.inf, F32)))
    lo, hi = finish(lo, jnp.min), finish(hi, jnp.max)
    kf = float(k)

    def bisect(_, c):
        lo, hi = c
        mid = 0.5 * lo + 0.5 * hi
        ok = count_ge(mid) >= kf
        return jnp.where(ok, mid, lo), jnp.where(ok, hi, mid)

    lo, hi = lax.fori_loop(0, N_BISECT, bisect, (lo, hi))
    v = max_below(hi, strict=False)
    c = count_ge(v)

    def walk(vc):
        v, c = vc
        v = jnp.where(c < kf, max_below(v, strict=True), v)
        return v, count_ge(v)

    v, _ = lax.while_loop(lambda vc: jnp.min(vc[1]) < kf, walk, (v, c))
    return v


def _slope_features(n):
    r = _alibi_slopes(n)
    parts = []
    for _ in range(3):
        p = r.astype(BF16).astype(F32)
        parts.append(p)
        r = r - p
    feat = jnp.stack([float(CHUNK) * p for p in parts] + parts, axis=1)
    return jnp.pad(feat, ((0, 0), (0, LANES - feat.shape[1])))


def _position_features(n_rows):
    pos = lax.broadcasted_iota(jnp.int32, (n_rows, HD), 0)
    lane = lax.broadcasted_iota(jnp.int32, (n_rows, HD), 1)
    feat = jnp.where(lane < 3, pos >> CHUNK_SHIFT, jnp.where(lane < 6, pos & (CHUNK - 1), 0))
    return feat.astype(F32).astype(BF16)


def _attn_a_kernel(slopes_ref, sfeat_ref, qa_ref, k_ref, v_ref, kw_ref, kwq_ref, qidx_ref, ga_ref, o_ref,
                   k16, vt16, ki16, st_scr, mb_scr, acc_scr, thr_scr, *, topk, n_heads):
    qi = pl.program_id(1)
    tq = qa_ref.shape[2]
    tk = tq
    rep = n_heads // KV_A
    nk = qi + 1

    @pl.when(qi == 0)
    def _():
        for g in range(KV_A):
            cols = slice(g * HD, (g + 1) * HD)
            k16[g, :, :HD] = k_ref[0, :, cols].astype(BF16)
            k16[g, :, HD:] = _position_features(k16.shape[1])
            for j in range(vt16.shape[1]):
                vt16[g, j] = jnp.transpose(v_ref[0, j * tk:(j + 1) * tk, cols]).astype(BF16)
        ki16[...] = kw_ref[0, :, :D_IDX].astype(BF16)

    w_t = jnp.transpose(kwq_ref[0])
    w_sc = w_t[D_IDX:D_IDX + H_IDX, :] * (H_IDX ** -0.5)

    def score_tile(kj, c):
        ki = ki16[pl.ds(pl.multiple_of(kj * tk, tk), tk), :]
        acc = jnp.zeros((tk, tq), F32)
        for h in range(H_IDX):
            acc = acc + jnp.maximum(_dot_nt(ki, qidx_ref[h, 0]), 0.0) * w_sc[h:h + 1, :]
        st_scr[kj] = acc
        return c

    lax.fori_loop(0, nk, score_tile, 0)
    krow = lax.broadcasted_iota(jnp.int32, (tk, tq), 0)
    qcol = lax.broadcasted_iota(jnp.int32, (tk, tq), 1)
    st_scr[qi] = jnp.where((krow >> CHUNK_SHIFT) <= (qcol >> CHUNK_SHIFT), st_scr[qi], -jnp.inf)

    thr_scr[...] = jnp.full((1, tq), -3e38, F32)

    @pl.when(qi * tq + CHUNK >= topk)
    def _():
        thr_scr[...] = _kth_largest(lambda i: st_scr[i], nk, topk, 0, (1, tq))

    thr = thr_scr[...]

    def mask_tile(kj, c):
        mb_scr[kj] = jnp.where(st_scr[kj] >= thr, 0.0, NEG)
        return c

    lax.fori_loop(0, nk, mask_tile, 0)

    ahead = jnp.maximum(krow - qcol, 0).astype(F32)

    for g in range(KV_A):
        q_g = jnp.concatenate(
            [qa_ref[g * rep:(g + 1) * rep, 0].reshape(rep * tq, HD),
             jnp.concatenate([jnp.broadcast_to(sfeat_ref[g * rep + hh:g * rep + hh + 1, :], (tq, LANES))
                              for hh in range(rep)], axis=0).astype(BF16)], axis=1)
        acc_scr[...] = jnp.zeros(acc_scr.shape, F32)

        def kv_step(kj, carry, diagonal=False):
            m_old, l_old = carry
            rows = pl.ds(pl.multiple_of(kj * tk, tk), tk)
            s_all = _dot_nt(k16[g, rows, :], q_g)
            mb = mb_scr[kj]
            s = jnp.concatenate(
                [s_all[:, hh * tq:(hh + 1) * tq]
                 + (mb - 2.0 * slopes_ref[g * rep + hh] * ahead if diagonal else mb) for hh in range(rep)],
                axis=1)
            m_new = jnp.maximum(m_old, jnp.max(s, axis=0, keepdims=True))
            alpha = jnp.exp2(m_old - m_new)
            p = jnp.exp2(s - m_new)
            l_new = alpha * l_old + jnp.sum(p, axis=0, keepdims=True)
            acc_scr[...] = alpha * acc_scr[...] + jnp.dot(vt16[g, kj], p.astype(BF16),
                                                          preferred_element_type=F32)
            return m_new, l_new

        carry = lax.fori_loop(0, qi, kv_step,
                              (jnp.full((1, rep * tq), NEG, F32), jnp.zeros((1, rep * tq), F32)))
        _, l = kv_step(qi, carry, diagonal=True)
        o_t = acc_scr[...] / l
        for hh in range(rep):
            cols = slice((g * rep + hh) * HD, (g * rep + hh + 1) * HD)
            o = jnp.transpose(o_t[:, hh * tq:(hh + 1) * tq])
            o_ref[0, :, cols] = (o * _silu(ga_ref[0, :, cols].astype(F32))).astype(o_ref.dtype)


def _attn_a_prompt(qa, k, v, kw, qidx, ga, topk, d_mix):
    n_heads, b, t, _ = qa.shape
    tq = min(256, topk)
    assert t % tq == 0 and tq % CHUNK == 0 and tq <= topk <= tq + CHUNK
    nkt = t // tq
    d_a = n_heads * HD
    return pl.pallas_call(
        functools.partial(_attn_a_kernel, topk=topk, n_heads=n_heads),
        out_shape=jax.ShapeDtypeStruct((b, t, d_mix), BF16),
        grid=(b, nkt),
        in_specs=[pl.BlockSpec(memory_space=pltpu.SMEM),
                  pl.BlockSpec((n_heads, LANES), lambda i, j: (0, 0)),
                  pl.BlockSpec((n_heads, 1, tq, HD), lambda i, j: (0, i, j, 0)),
                  pl.BlockSpec((1, t, KV_A * HD), lambda i, j: (i, 0, 0)),
                  pl.BlockSpec((1, t, KV_A * HD), lambda i, j: (i, 0, 0)),
                  pl.BlockSpec((1, t, 128), lambda i, j: (i, 0, 0)),
                  pl.BlockSpec((1, tq, 128), lambda i, j: (i, j, 0)),
                  pl.BlockSpec((H_IDX, 1, tq, D_IDX), lambda i, j: (0, i, j, 0)),
                  pl.BlockSpec((1, tq, d_a), lambda i, j: (i, j, 0))],
        out_specs=pl.BlockSpec((1, tq, d_a), lambda i, j: (i, j, 0)),
        scratch_shapes=[pltpu.VMEM((KV_A, t, 2 * HD), BF16),
                        pltpu.VMEM((KV_A, nkt, HD, tq), BF16),
                        pltpu.VMEM((t, D_IDX), BF16),
                        pltpu.VMEM((nkt, tq, tq), F32),
                        pltpu.VMEM((nkt, tq, tq), F32),
                        pltpu.VMEM((HD, n_heads // KV_A * tq), F32),
                        pltpu.VMEM((1, tq), F32)],
        compiler_params=_cparams(("arbitrary", "arbitrary")),
        name="attn_a_prompt",
    )(_alibi_slopes(n_heads), _slope_features(n_heads), qa, k, v, kw, kw, qidx, ga)


def _lam(lamp_ref):
    s1 = jnp.sum(lamp_ref[0:1, :] * lamp_ref[1:2, :], axis=1, keepdims=True)
    s2 = jnp.sum(lamp_ref[2:3, :] * lamp_ref[3:4, :], axis=1, keepdims=True)
    return jnp.exp(s1) - jnp.exp(s2) + LAM_INIT


def _subln_gate(o, subln, g):
    o = o * lax.rsqrt(jnp.mean(o * o, axis=-1, keepdims=True) + LN_EPS)
    o = o * subln * (1.0 - LAM_INIT)
    return o * _silu(g.astype(F32))


def _attn_b_kernel(slopes_ref, sfeat_ref, lamp_ref, subln_ref, q_ref, k_ref, v_ref, g_ref, mix_ref, o_ref,
                   vt_scr, s_scr, acc_scr, kaug_scr, qaug_scr, *, tq):
    h = pl.program_id(1)
    t = q_ref.shape[2]
    tk = tq
    nq = t // tq
    slope = slopes_ref[h]
    lam = _lam(lamp_ref)

    for j in range(nq):
        vt_scr[j] = jnp.transpose(v_ref[0, 0, j * tk:(j + 1) * tk, :].astype(F32)).astype(BF16)

    pos_feat = _position_features(t)
    slope_feat = jnp.broadcast_to(sfeat_ref[pl.ds(h, 1), :], (t, LANES)).astype(BF16)
    for mi in range(2):
        kaug_scr[mi, :, :HD] = k_ref[mi, 0]
        kaug_scr[mi, :, HD:] = pos_feat
        qaug_scr[mi, :, :HD] = q_ref[mi, 0]
        qaug_scr[mi, :, HD:] = slope_feat

    krow = lax.broadcasted_iota(jnp.int32, (tk, tq), 0)
    qcol = lax.broadcasted_iota(jnp.int32, (tk, tq), 1)
    ahead = jnp.maximum(krow - qcol, 0).astype(F32)
    diag_bias = jnp.where((krow >> CHUNK_SHIFT) <= (qcol >> CHUNK_SHIFT), -2.0 * slope * ahead, NEG)

    pairs = [(qi, j) for qi in range(nq) for j in range(qi + 1)]

    def issue_logits(n):
        qi, j = pairs[n]
        for mi in range(2):
            s_scr[n % 2, mi] = _dot_nt(kaug_scr[mi, j * tk:(j + 1) * tk, :], qaug_scr[mi, qi * tq:(qi + 1) * tq, :])

    issue_logits(0)
    carry = None
    for n, (qi, j) in enumerate(pairs):
        if n + 1 < len(pairs):
            issue_logits(n + 1)
        if j == 0:
            acc_scr[...] = jnp.zeros(acc_scr.shape, F32)
            init = (jnp.full((1, tq), NEG, F32), jnp.zeros((1, tq), F32))
            carry = [init, init]
        for mi in range(2):
            m_old, l_old = carry[mi]
            s = s_scr[n % 2, mi] + diag_bias if j == qi else s_scr[n % 2, mi]
            m_new = jnp.maximum(m_old, jnp.max(s, axis=0, keepdims=True))
            alpha = jnp.exp2(m_old - m_new)
            p = jnp.exp2(s - m_new)
            l_new = alpha * l_old + jnp.sum(p, axis=0, keepdims=True)
            acc_scr[mi] = alpha * acc_scr[mi] + jnp.dot(vt_scr[j], p.astype(BF16), preferred_element_type=F32)
            carry[mi] = (m_new, l_new)
        if j == qi:
            o_t = acc_scr[0] / carry[0][1] - lam * (acc_scr[1] / carry[1][1])
            o_t = o_t * lax.rsqrt(jnp.mean(o_t * o_t, axis=0, keepdims=True) + LN_EPS)
            o = jnp.transpose(o_t) * subln_ref[...] * (1.0 - LAM_INIT)
            rows = slice(qi * tq, (qi + 1) * tq)
            o_ref[0, rows, :] = (o * _silu(g_ref[0, rows, :].astype(F32))).astype(o_ref.dtype)


def _attn_b_prompt(qb, kb16, vb16, gb, lamp, subln, mix):
    _, b, t, _ = qb.shape
    n_heads = vb16.shape[0]
    col0 = (mix.shape[2] - n_heads * 2 * HD) // (2 * HD)
    tq = min(512, t)
    assert t % tq == 0 and tq % CHUNK == 0
    return pl.pallas_call(
        functools.partial(_attn_b_kernel, tq=tq),
        out_shape=jax.ShapeDtypeStruct(mix.shape, mix.dtype),
        grid=(b, n_heads),
        in_specs=[pl.BlockSpec(memory_space=pltpu.SMEM),
                  pl.BlockSpec((n_heads, LANES), lambda i, h: (0, 0)),
                  pl.BlockSpec((4, HD), lambda i, h: (0, 0)),
                  pl.BlockSpec((1, 2 * HD), lambda i, h: (0, 0)),
                  pl.BlockSpec((2, 1, t, HD), lambda i, h: (h, i, 0, 0)),
                  pl.BlockSpec((2, 1, t, HD), lambda i, h: (h, i, 0, 0)),
                  pl.BlockSpec((1, 1, t, 2 * HD), lambda i, h: (h, i, 0, 0)),
                  pl.BlockSpec((1, t, 2 * HD), lambda i, h: (i, 0, h)),
                  pl.BlockSpec(memory_space=pl.ANY)],
        out_specs=pl.BlockSpec((1, t, 2 * HD), lambda i, h: (i, 0, col0 + h)),
        input_output_aliases={8: 0},
        scratch_shapes=[pltpu.VMEM((t // tq, 2 * HD, tq), BF16),
                        pltpu.VMEM((2, 2, tq, tq), F32),
                        pltpu.VMEM((2, 2 * HD, tq), F32),
                        pltpu.VMEM((2, t, 2 * HD), BF16),
                        pltpu.VMEM((2, t, 2 * HD), BF16)],
        compiler_params=_cparams(("arbitrary", "arbitrary")),
        name="attn_b_prompt",
    )(_alibi_slopes(n_heads), _slope_features(n_heads), lamp, subln, qb, kb16, vb16, gb, mix)


def _attn_a_sample_kernel(slopes_ref, qa_ref, kc_ref, vc_ref, kic_ref, kn_ref, vn_ref, kwn_ref, qidx_ref,
                          ga_ref, o_ref, k16, v16, sc_scr, s_scr, p_scr, *, topk, n_heads, cw):
    t = qa_ref.shape[2]
    past = kc_ref.shape[0] // KV_A
    s_pad = sc_scr.shape[1]
    rep = n_heads // KV_A
    kcol = lax.broadcasted_iota(jnp.int32, (t, s_pad), 1)
    qrow = lax.broadcasted_iota(jnp.int32, (t, s_pad), 0)

    qi_all = qidx_ref[:, 0].reshape(H_IDX * t, D_IDX)
    w_q = kwn_ref[0, :, D_IDX:D_IDX + H_IDX] * (H_IDX ** -0.5)

    def scores(ki):
        r = jnp.maximum(_dot_nt(qi_all, ki), 0.0)
        acc = jnp.zeros((t, ki.shape[0]), F32)
        for h in range(H_IDX):
            acc = acc + r[h * t:(h + 1) * t, :] * w_q[:, h:h + 1]
        return acc

    for c0 in range(0, past, cw):
        sc_scr[:, c0:c0 + cw] = scores(kic_ref[0, c0:c0 + cw, :].astype(BF16))
    tail = s_pad - past
    ki_new = jnp.concatenate(
        [kwn_ref[0, :, :D_IDX], jnp.zeros((tail - t, D_IDX), F32)], axis=0).astype(BF16)
    tail_valid = lax.broadcasted_iota(jnp.int32, (t, tail), 1) < t
    sc_scr[:, past:] = jnp.where(tail_valid, scores(ki_new), -jnp.inf)

    thr = _kth_largest(lambda i: sc_scr[...], 1, topk, 1, (t, 1))
    mbias = jnp.where(sc_scr[...] >= thr, 0.0, NEG)
    dist = jnp.abs((past + qrow - kcol).astype(F32))

    for g in range(KV_A):
        cols = slice(g * HD, (g + 1) * HD)
        k16[:past] = kc_ref[pl.ds(g, past, stride=KV_A), :].astype(BF16)
        v16[:past] = vc_ref[pl.ds(g, past, stride=KV_A), :].astype(BF16)
        zpad = jnp.zeros((tail - t, HD), F32)
        k16[past:] = jnp.concatenate([kn_ref[0, :, cols], zpad], axis=0).astype(BF16)
        v16[past:] = jnp.concatenate([vn_ref[0, :, cols], zpad], axis=0).astype(BF16)
        q_g = qa_ref[g * rep:(g + 1) * rep, 0].reshape(rep * t, HD)
        s_scr[...] = _dot_nt(q_g, k16[...])
        inv_l = []
        for hh in range(rep):
            rows = slice(hh * t, (hh + 1) * t)
            s = s_scr[rows, :] - slopes_ref[g * rep + hh] * dist + mbias
            p = jnp.exp2(s - jnp.max(s, axis=1, keepdims=True))
            inv_l.append(1.0 / jnp.sum(p, axis=1, keepdims=True))
            p_scr[rows, :] = p.astype(BF16)
        o_g = jnp.dot(p_scr[...], v16[...], preferred_element_type=F32)
        for hh in range(rep):
            hc = slice((g * rep + hh) * HD, (g * rep + hh + 1) * HD)
            o = o_g[hh * t:(hh + 1) * t, :] * inv_l[hh]
            o_ref[0, :, hc] = (o * _silu(ga_ref[0, :, hc].astype(F32))).astype(o_ref.dtype)


def _attn_a_sample(qa, kc, vc, kic, kn, vn, kwn, qidx, ga, topk, d_mix, cw=512):
    n_heads, b, t, _ = qa.shape
    past = kic.shape[1]
    s_pad = past + 128
    assert t <= 128 and past % cw == 0
    d_a = n_heads * HD
    return pl.pallas_call(
        functools.partial(_attn_a_sample_kernel, topk=topk, n_heads=n_heads, cw=cw),
        out_shape=jax.ShapeDtypeStruct((b, t, d_mix), BF16),
        grid=(b,),
        in_specs=[pl.BlockSpec(memory_space=pltpu.SMEM),
                  pl.BlockSpec((n_heads, 1, t, HD), lambda i: (0, i, 0, 0)),
                  pl.BlockSpec((past * KV_A, HD), lambda i: (i, 0)),
                  pl.BlockSpec((past * KV_A, HD), lambda i: (i, 0)),
                  pl.BlockSpec((1, past, D_IDX), lambda i: (i, 0, 0)),
                  pl.BlockSpec((1, t, KV_A * HD), lambda i: (i, 0, 0)),
                  pl.BlockSpec((1, t, KV_A * HD), lambda i: (i, 0, 0)),
                  pl.BlockSpec((1, t, 128), lambda i: (i, 0, 0)),
                  pl.BlockSpec((H_IDX, 1, t, D_IDX), lambda i: (0, i, 0, 0)),
                  pl.BlockSpec((1, t, d_a), lambda i: (i, 0, 0))],
        out_specs=pl.BlockSpec((1, t, d_a), lambda i: (i, 0, 0)),
        scratch_shapes=[pltpu.VMEM((s_pad, HD), BF16),
                        pltpu.VMEM((s_pad, HD), BF16),
                        pltpu.VMEM((t, s_pad), F32),
                        pltpu.VMEM((n_heads // KV_A * t, s_pad), F32),
                        pltpu.VMEM((n_heads // KV_A * t, s_pad), BF16)],
        compiler_params=_cparams(("arbitrary",)),
        name="attn_a_sample",
    )(_alibi_slopes(n_heads), qa, kc, vc, kic, kn, vn, kwn, qidx, ga)


def _attn_b_sample_kernel(slopes_ref, lamp_ref, subln_ref, q_ref, kc_ref, vc_ref, kn_ref, vn_ref, g_ref, mix_ref,
                          o_ref, m_scr, l_scr, acc_scr, *, n_heads, past, v_half_major):
    c = pl.program_id(1)
    t = q_ref.shape[2]
    slots = 2 * n_heads
    pc = kc_ref.shape[0] // slots
    lam = _lam(lamp_ref)

    @pl.when(c == 0)
    def _():
        m_scr[...] = jnp.full(m_scr.shape, NEG, F32)
        l_scr[...] = jnp.zeros(l_scr.shape, F32)
        acc_scr[...] = jnp.zeros(acc_scr.shape, F32)

    def q_blockdiag(h):
        z = jnp.zeros((t, HD), BF16)
        return jnp.concatenate([jnp.concatenate([q_ref[2 * h, 0], z], axis=1),
                                jnp.concatenate([z, q_ref[2 * h + 1, 0]], axis=1)], axis=0)

    def update(h, s, v16):
        m_old = m_scr[h]
        m_new = jnp.maximum(m_old, jnp.max(s, axis=1, keepdims=True))
        alpha = jnp.exp2(m_old - m_new)
        p = jnp.exp2(s - m_new)
        l_scr[h] = alpha * l_scr[h] + jnp.sum(p, axis=1, keepdims=True)
        acc_scr[h] = alpha * acc_scr[h] + jnp.dot(p.astype(BF16), v16, preferred_element_type=F32)
        m_scr[h] = m_new

    qrow = lax.broadcasted_iota(jnp.int32, (t, pc), 0)
    kcol = lax.broadcasted_iota(jnp.int32, (t, pc), 1)
    dist = (past + qrow - kcol).astype(F32) - (c * pc).astype(F32)
    for h in range(n_heads):
        k01 = jnp.concatenate([kc_ref[pl.ds(2 * h, pc, stride=slots), :],
                               kc_ref[pl.ds(2 * h + 1, pc, stride=slots), :]], axis=1).astype(BF16)
        v_h = jnp.concatenate([vc_ref[pl.ds(h, pc, stride=slots), :],
                               vc_ref[pl.ds(n_heads + h, pc, stride=slots), :]], axis=1).astype(BF16)
        bias = -slopes_ref[h] * dist
        update(h, _dot_nt(q_blockdiag(h), k01) + jnp.concatenate([bias, bias], axis=0), v_h)

    @pl.when(c == pl.num_programs(1) - 1)
    def _():
        qr = lax.broadcasted_iota(jnp.int32, (t, LANES), 0)
        kc_ = lax.broadcasted_iota(jnp.int32, (t, LANES), 1)
        dist_new = jnp.abs((qr - kc_).astype(F32))
        zpad = jnp.zeros((LANES - t, 2 * HD), F32)
        for h in range(n_heads):
            cols = slice(h * 2 * HD, (h + 1) * 2 * HD)
            kn = jnp.concatenate([kn_ref[0, :, cols], zpad], axis=0).astype(BF16)
            if v_half_major:
                v_new = jnp.concatenate([vn_ref[0, :, (half * n_heads + h) * HD:(half * n_heads + h + 1) * HD]
                                         for half in range(2)], axis=1)
            else:
                v_new = vn_ref[0, :, cols]
            vn = jnp.concatenate([v_new, zpad], axis=0).astype(BF16)
            bias = jnp.where(kc_ < t, -slopes_ref[h] * dist_new, NEG)
            update(h, _dot_nt(q_blockdiag(h), kn) + jnp.concatenate([bias, bias], axis=0), vn)
            o_all = acc_scr[h] / l_scr[h]
            o = o_all[:t] - lam * o_all[t:]
            o_ref[0, :, cols] = _subln_gate(o, subln_ref[...], g_ref[0, :, cols]).astype(o_ref.dtype)


def _attn_b_sample(qb, kc, vc, kn, vn, gb, lamp, subln, mix, past, v_half_major, pc=1024):
    _, b, t, _ = qb.shape
    d_b = gb.shape[2]
    n_heads = d_b // (2 * HD)
    slots = 2 * n_heads
    pc = min(pc, past)
    assert t <= LANES and past % pc == 0 and (mix.shape[2] - d_b) % d_b == 0
    nc = past // pc
    col0 = (mix.shape[2] - d_b) // d_b
    return pl.pallas_call(
        functools.partial(_attn_b_sample_kernel, n_heads=n_heads, past=past, v_half_major=v_half_major),
        out_shape=jax.ShapeDtypeStruct(mix.shape, mix.dtype),
        grid=(b, nc),
        in_specs=[pl.BlockSpec(memory_space=pltpu.SMEM),
                  pl.BlockSpec((4, HD), lambda i, c: (0, 0)),
                  pl.BlockSpec((1, 2 * HD), lambda i, c: (0, 0)),
                  pl.BlockSpec((slots, 1, t, HD), lambda i, c: (0, i, 0, 0)),
                  pl.BlockSpec((pc * slots, LANES), lambda i, c: (i * nc + c, 0)),
                  pl.BlockSpec((pc * slots, LANES), lambda i, c: (i * nc + c, 0)),
                  pl.BlockSpec((1, t, d_b), lambda i, c: (i, 0, 0)),
                  pl.BlockSpec((1, t, d_b), lambda i, c: (i, 0, 0)),
                  pl.BlockSpec((1, t, d_b), lambda i, c: (i, 0, 0)),
                  pl.BlockSpec(memory_space=pl.ANY)],
        out_specs=pl.BlockSpec((1, t, d_b), lambda i, c: (i, 0, col0)),
        input_output_aliases={9: 0},
        scratch_shapes=[pltpu.VMEM((n_heads, 2 * t, 1), F32),
                        pltpu.VMEM((n_heads, 2 * t, 1), F32),
                        pltpu.VMEM((n_heads, 2 * t, 2 * HD), F32)],
        compiler_params=_cparams(("arbitrary", "arbitrary")),
        name="attn_b_sample",
    )(_alibi_slopes(n_heads), lamp, subln, qb, kc, vc, kn, vn, gb, mix)


def _out_kernel(mix_ref, w_ref, x_ref, gate_ref, lng_ref, lnb_ref, o_ref, shift_scr, s1_scr, s2_scr, *, alpha, nj):
    j = pl.program_id(1)
    tn = x_ref.shape[1]
    r = alpha * x_ref[...] + gate_ref[...] * jnp.dot(mix_ref[...], w_ref[...], preferred_element_type=F32)
    o_ref[:, pl.ds(pl.multiple_of(j * tn, tn), tn)] = r

    @pl.when(j == 0)
    def _():
        shift_scr[...] = jnp.mean(r, axis=1, keepdims=True)
        s1_scr[...] = jnp.zeros(s1_scr.shape, F32)
        s2_scr[...] = jnp.zeros(s2_scr.shape, F32)

    dv = r - shift_scr[...]
    s1_scr[...] += jnp.sum(dv, axis=1, keepdims=True)
    s2_scr[...] += jnp.sum(dv * dv, axis=1, keepdims=True)

    @pl.when(j == nj - 1)
    def _():
        d = nj * tn
        m1 = s1_scr[...] / d
        mu = shift_scr[...] + m1
        inv = lax.rsqrt(s2_scr[...] / d - m1 * m1 + LN_EPS)
        for c in range(nj):
            cols = slice(c * tn, (c + 1) * tn)
            o_ref[:, cols] = (o_ref[:, cols] - mu) * inv * lng_ref[:, cols] + lnb_ref[:, cols]


def _out_proj(mix, w_out, x2d, gate, ln_g, ln_b, alpha, tm, tn, rows_per_gate):
    m, dm = mix.shape
    d = w_out.shape[1]
    assert m % tm == 0 and d % tn == 0
    if rows_per_gate == 1:
        gate_spec = pl.BlockSpec((tm, tn), lambda i, j: (i, j))
    else:
        assert rows_per_gate % tm == 0
        gate = gate.reshape(-1, 1, d)
        gate_spec = pl.BlockSpec((None, 1, tn), lambda i, j: (i * tm // rows_per_gate, 0, j))
    return pl.pallas_call(
        functools.partial(_out_kernel, alpha=alpha, nj=d // tn),
        out_shape=jax.ShapeDtypeStruct((m, d), F32),
        grid=(m // tm, d // tn),
        in_specs=[pl.BlockSpec((tm, dm), lambda i, j: (i, 0)),
                  pl.BlockSpec((dm, tn), lambda i, j: (0, j)),
                  pl.BlockSpec((tm, tn), lambda i, j: (i, j)),
                  gate_spec,
                  pl.BlockSpec((1, d), lambda i, j: (0, 0)),
                  pl.BlockSpec((1, d), lambda i, j: (0, 0))],
        out_specs=pl.BlockSpec((tm, d), lambda i, j: (i, 0), pipeline_mode=pl.Buffered(1)),
        scratch_shapes=[pltpu.VMEM((tm, 1), F32)] * 3,
        compiler_params=_cparams(("arbitrary", "arbitrary")),
        name="out_proj",
    )(mix, w_out, x2d, gate, ln_g.reshape(1, d), ln_b.reshape(1, d))


def _w_in_layout(d_a, d_b, v_half_major):
    kv = KV_A * HD
    sizes = dict(q_a=d_a, k_a=kv, v_a=kv, g_a=d_a, q_idx=H_IDX * D_IDX, kw=D_IDX + H_IDX, q_b=d_b, k_b=d_b,
                 v_b=d_b, g_b=d_b)
    src, off = {}, 0
    for name in ("q_a", "k_a", "v_a", "g_a", "q_idx", "kw", "q_b", "k_b", "v_b", "g_b"):
        src[name] = off
        off += sizes[name]
    moves, dst = [], 0
    for name in ("k_a", "v_a", "kw"):
        moves.append((dst, src[name], sizes[name]))
        dst += sizes[name]
    zero = (dst, 1024)
    dst = 1024
    for name in ("q_a", "g_a", "q_idx", "q_b", "k_b"):
        moves.append((dst, src[name], sizes[name]))
        dst += sizes[name]
    n_b = d_b // (2 * HD)
    if v_half_major:
        for half in range(2):
            for h in range(n_b):
                moves.append((dst + (half * n_b + h) * HD, src["v_b"] + (h * 2 + half) * HD, HD))
    else:
        moves.append((dst, src["v_b"], d_b))
    dst += d_b
    moves.append((dst, src["g_b"], d_b))
    return moves, zero, dst + d_b


def _repack_kernel(src_ref, *refs):
    o_ref = refs[-1]
    for piece, w_ref in enumerate(refs[:-1]):
        o_ref[:, piece * LANES:(piece + 1) * LANES] = jnp.transpose(w_ref[...]).astype(o_ref.dtype)


ROW_ALIGN = 16


def _pad_w_in(w_in_t, d_a, d_b, v_half_major, per_step=4):
    moves, _, n_out = _w_in_layout(d_a, d_b, v_half_major)
    n_in, k = w_in_t.shape
    src_rows = np.zeros(n_out // LANES, np.int32)
    for dst, src, width in moves:
        for c in range(0, width, LANES):
            assert dst % LANES == 0 and (src + c) % ROW_ALIGN == 0 and src + c + LANES <= n_in
            src_rows[(dst + c) // LANES] = (src + c) // ROW_ALIGN
    return pl.pallas_call(
        _repack_kernel,
        out_shape=jax.ShapeDtypeStruct((k, n_out), BF16),
        grid_spec=pltpu.PrefetchScalarGridSpec(
            num_scalar_prefetch=1,
            grid=(n_out // (per_step * LANES),),
            in_specs=[pl.BlockSpec((pl.Element(LANES), pl.Element(k)),
                                   lambda i, src, p=p: (src[i * per_step + p] * ROW_ALIGN, 0))
                      for p in range(per_step)],
            out_specs=pl.BlockSpec((k, per_step * LANES), lambda i, src: (0, i))),
        compiler_params=_cparams(("arbitrary",)),
        name="repack_w_in",
    )(jnp.asarray(src_rows), *([w_in_t] * per_step))


def _project(x, shift, scale, w_pad, d_a, d_b, tm, want_b16, native):
    kv = KV_A * HD
    h2d, k_a, v_a, kw = _modulate_project(x, shift, scale, w_pad, [(0, kv), (kv, 2 * kv), (2 * kv, 2 * kv + 128)],
                                          tm=min(x.shape[1], 512))
    c = 1024
    tn = 512
    q_scale = HD ** -0.5 * LOG2E
    (q_a,) = _proj(h2d, w_pad, c, d_a, tn, tm, [("heads", 0, tn, HD, q_scale, BF16)])
    c += d_a
    (g_a,) = _proj(h2d, w_pad, c, d_a, tn, tm, [("flat", 0, tn, 0, 1.0, BF16)])
    c += d_a
    (q_idx,) = _proj(h2d, w_pad, c, H_IDX * D_IDX, tn, tm, [("heads", 0, tn, D_IDX, 1.0, BF16)])
    c += H_IDX * D_IDX
    (q_b,) = _proj(h2d, w_pad, c, d_b, tn, tm, [("heads", 0, tn, HD, q_scale, BF16)])
    c += d_b
    tn_b = tn
    kb_outs = [("flat", 0, tn, 0, 1.0, F32)]
    vb_outs = [("flat", 0, tn, 0, 1.0, F32)]
    if want_b16 and native:
        tn_b = SUBLANES * LANES
        kb_outs = [("native", 0, tn_b, 0, 1.0, F32), ("heads", 0, tn_b, HD, 1.0, BF16)]
        vb_outs = [("native", 0, tn_b, 0, 1.0, F32), ("lanes", 0, tn_b, 0, 1.0, BF16)]
    elif want_b16:
        kb_outs.append(("heads", 0, tn, HD, 1.0, BF16))
        vb_outs.append(("heads", 0, tn, 2 * HD, 1.0, BF16))
    k_b = _proj(h2d, w_pad, c, d_b, tn_b, tm, kb_outs)
    c += d_b
    v_b = _proj(h2d, w_pad, c, d_b, tn_b, tm, vb_outs)
    c += d_b
    (g_b,) = _proj(h2d, w_pad, c, d_b, tn, tm, [("flat", 0, tn, 0, 1.0, BF16)])
    return dict(k_a=k_a, v_a=v_a, kw=kw, q_a=q_a, g_a=g_a, q_idx=q_idx, q_b=q_b, k_b=k_b, v_b=v_b, g_b=g_b)


def _layer(x, mod, past, w_pad, w_out, lamp, subln, ln_g, ln_b, alpha, d_a, d_b, native):
    b, t, d = x.shape
    h_b = d_b // (2 * HD)
    m = b * t
    shift, scale, gate = (mod[:, i * d:(i + 1) * d] for i in range(3))
    prompt = past is None

    p = _project(x, shift.reshape(b, 1, d), scale.reshape(b, 1, d), w_pad, d_a, d_b, tm=min(m, 1024),
                 want_b16=prompt, native=native)

    r3 = lambda a: a.reshape(b, t, a.shape[-1])
    r4 = lambda a: a.reshape(a.shape[0], b, t, a.shape[-1])
    k_a, v_a, kw, g_a, g_b = r3(p["k_a"]), r3(p["v_a"]), r3(p["kw"]), r3(p["g_a"]), r3(p["g_b"])
    q_a, q_idx, q_b = r4(p["q_a"]), r4(p["q_idx"]), r4(p["q_b"])

    if prompt:
        topk = min(TOPK_MAX, t // 4)
        mix = _attn_a_prompt(q_a, k_a, v_a, kw, q_idx, g_a, topk, d_a + d_b)
        mix = _attn_b_prompt(q_b, r4(p["k_b"][1]), r4(p["v_b"][1]), g_b, lamp, subln, mix)
        tm_out, rows_per_gate, gate_rows = min(m, 1024), t, gate
        new_k_b = p["k_b"][0].reshape(1, b, t, h_b, 2, HD)
        if native:
            new_v_b = p["v_b"][0].reshape(b, t, 2, h_b, HD).transpose(0, 1, 3, 2, 4).reshape(1, b, t, h_b, 2 * HD)
        else:
            new_v_b = p["v_b"][0].reshape(1, b, t, h_b, 2 * HD)
    else:
        kc_a, vc_a, kic, kc_b, vc_b = past
        plen = kic.shape[1]
        topk = min(TOPK_MAX, (plen + t) // 4)
        k_b, v_b = r3(p["k_b"][0]), r3(p["v_b"][0])
        rows = lambda a: a.reshape(-1, LANES)
        vc_b_rows = rows(vc_b.reshape(b, plen, h_b, 2, HD).transpose(0, 1, 3, 2, 4))
        mix = _attn_a_sample(q_a, rows(kc_a), rows(vc_a), kic, k_a, v_a, kw, q_idx, g_a, topk, d_a + d_b)
        mix = _attn_b_sample(q_b, rows(kc_b), vc_b_rows, k_b, v_b, g_b, lamp, subln, mix, plen, native)
        tm_out, rows_per_gate = m, 1
        gate_rows = jnp.broadcast_to(gate[:, None, :], (b, t, d)).reshape(m, d)
        new_k_b = k_b.reshape(1, b, t, h_b, 2, HD)
        if native:
            new_v_b = v_b.reshape(b, t, 2, h_b, HD).transpose(0, 1, 3, 2, 4).reshape(1, b, t, h_b, 2 * HD)
        else:
            new_v_b = v_b.reshape(1, b, t, h_b, 2 * HD)

    y = _out_proj(mix.reshape(m, d_a + d_b), w_out, x.reshape(m, d), gate_rows, ln_g, ln_b, alpha,
                  tm=tm_out, tn=512, rows_per_gate=rows_per_gate)
    rows_out = (k_a.reshape(1, b, t, KV_A, HD), v_a.reshape(1, b, t, KV_A, HD), kw[None, :, :, :D_IDX],
                new_k_b, new_v_b)
    return y.reshape(b, t, d), rows_out


def kernel(x_prompt, x_sample, cache_a_k, cache_a_v, cache_a_kidx, cache_b_k, cache_b_v, c_prompt, c_sample,
           w_ada, b_ada, w_in, w_out, lam_q1, lam_k1, lam_q2, lam_k2, subln_g, ln_g, ln_b):
    depth, d, _ = w_ada.shape
    assert depth == 1, "single-layer step"
    d_b = cache_b_v.shape[3] * cache_b_v.shape[4]
    d_a = w_out.shape[1] - d_b
    alpha = (2.0 * depth) ** 0.25
    bp = x_prompt.shape[0]

    mod = _ada(jnp.concatenate([c_prompt, c_sample], axis=0), w_ada[0], b_ada[0])
    native = d_b == 2 * SUBLANES * LANES
    w_pad = _pad_w_in(jnp.transpose(w_in[0]), d_a, d_b, native)
    lamp = jnp.concatenate([lam_q1, lam_k1, lam_q2, lam_k2], axis=0)
    common = (w_pad, w_out[0].astype(BF16), lamp, subln_g, ln_g[0], ln_b[0], alpha, d_a, d_b, native)

    y_p, rows_p = _layer(x_prompt, mod[:bp], None, *common)
    past = (cache_a_k[0], cache_a_v[0], cache_a_kidx[0], cache_b_k[0], cache_b_v[0])
    y_s, rows_s = _layer(x_sample, mod[bp:], past, *common)
    return (y_p, y_s) + rows_p + rows_s
```

```python
import functools
import math

import jax
import jax.numpy as jnp
import numpy as np
from jax import lax
from jax.experimental import pallas as pl
from jax.experimental.pallas import tpu as pltpu

F32 = jnp.float32
BF16 = jnp.bfloat16

HD = 128
CHUNK = 64
CHUNK_SHIFT = 6
KV_A = 2
H_IDX = 16
D_IDX = 64
TOPK_MAX = 256
LN_EPS = 1e-5
LAM_INIT = 0.8 - 0.6 * math.exp(-0.3 * 0)

LOG2E = math.log2(math.e)
NEG = -1e30
N_BISECT = 16

VMEM_LIMIT = 56 * 1024 * 1024
MXU_DIM = 256
LANES = 128
SUBLANES = 8


def _cparams(sem):
    return pltpu.CompilerParams(dimension_semantics=sem, vmem_limit_bytes=VMEM_LIMIT)


def _silu(x):
    return x * jax.nn.sigmoid(x)


def _dot_nt(a, b):
    return lax.dot_general(a, b, (((1,), (1,)), ((), ())), preferred_element_type=F32)


def _alibi_slopes(n):
    return jnp.asarray(2.0 ** (-8.0 * np.arange(1, n + 1) / n), dtype=F32) * LOG2E


def _ada_kernel(c_ref, w_ref, b_ref, o_ref):
    s = _silu(c_ref[...]).astype(BF16)
    o_ref[...] = jnp.dot(s, w_ref[...].astype(BF16), preferred_element_type=F32) + b_ref[...]


def _ada(c, w_ada, b_ada, tn=512):
    n, d = c.shape
    e = w_ada.shape[1]
    return pl.pallas_call(
        _ada_kernel,
        out_shape=jax.ShapeDtypeStruct((n, e), F32),
        grid=(e // tn,),
        in_specs=[pl.BlockSpec((n, d), lambda j: (0, 0)),
                  pl.BlockSpec((d, tn), lambda j: (0, j)),
                  pl.BlockSpec((1, tn), lambda j: (0, j))],
        out_specs=pl.BlockSpec((n, tn), lambda j: (0, j)),
        compiler_params=_cparams(("arbitrary",)),
        name="ada",
    )(c, w_ada, b_ada.reshape(1, e))


def _modproj_kernel(x_ref, shift_ref, scale_ref, w_ref, h_ref, *o_refs, splits):
    h = (x_ref[0] * (1.0 + scale_ref[0]) + shift_ref[0]).astype(h_ref.dtype)
    h_ref[...] = h
    acc = jnp.dot(h, w_ref[...], preferred_element_type=F32)
    tm = acc.shape[0]
    for o_ref, (lo, hi, interleave) in zip(o_refs, splits):
        if interleave:
            for g in range((hi - lo) // LANES):
                o_ref[pl.ds(g, tm, stride=(hi - lo) // LANES), :] = acc[:, lo + g * LANES:lo + (g + 1) * LANES]
        else:
            o_ref[...] = acc[:, lo:hi]


def _modulate_project(x, shift, scale, w_pad, splits, tm):
    b, t, d = x.shape
    n = splits[-1][1]
    nt = t // tm

    def out_shape_block(split):
        lo, hi, interleave = split
        groups = (hi - lo) // LANES if interleave else 1
        return (b * t * groups, (hi - lo) // groups), (tm * groups, (hi - lo) // groups)

    return pl.pallas_call(
        functools.partial(_modproj_kernel, splits=tuple(splits)),
        out_shape=[jax.ShapeDtypeStruct((b * t, d), BF16)]
        + [jax.ShapeDtypeStruct(shape, F32) for shape, _ in map(out_shape_block, splits)],
        grid=(b, nt),
        in_specs=[pl.BlockSpec((1, tm, d), lambda i, j: (i, j, 0)),
                  pl.BlockSpec((1, 1, d), lambda i, j: (i, 0, 0)),
                  pl.BlockSpec((1, 1, d), lambda i, j: (i, 0, 0)),
                  pl.BlockSpec((d, n), lambda i, j: (0, 0))],
        out_specs=[pl.BlockSpec((tm, d), lambda i, j: (i * nt + j, 0))]
        + [pl.BlockSpec(block, lambda i, j: (i * nt + j, 0)) for _, block in map(out_shape_block, splits)],
        compiler_params=_cparams(("arbitrary", "arbitrary")),
        name="modulate_proj",
    )(x, shift, scale, w_pad)


def _proj_kernel(h_ref, w_ref, *o_refs, outs, stream_weights):
    if stream_weights:
        acc = jnp.transpose(_dot_nt(jnp.transpose(w_ref[...]), h_ref[...]))
    else:
        acc = jnp.dot(h_ref[...], w_ref[...], preferred_element_type=F32)
    for o_ref, (kind, lo, hi, width, scale) in zip(o_refs, outs):
        if kind == "flat":
            o_ref[...] = acc[:, lo:hi].astype(o_ref.dtype)
        elif kind == "heads":
            for hh in range((hi - lo) // width):
                blk = acc[:, lo + hh * width: lo + (hh + 1) * width]
                if scale != 1.0:
                    blk = blk * scale
                o_ref[hh] = blk.astype(o_ref.dtype)
        elif kind == "lanes":
            for hh in range((hi - lo) // LANES):
                o_ref[hh] = acc[:, lo + hh * LANES: lo + (hh + 1) * LANES].astype(o_ref.dtype)
        else:
            for s in range(SUBLANES):
                o_ref[:, s, :] = acc[:, lo + s * LANES: lo + (s + 1) * LANES].astype(o_ref.dtype)


def _proj(h2d, w_pad, col_start, ncols, tn, tm, outs):
    m, k = h2d.shape
    nj = ncols // tn
    assert col_start % tn == 0 and ncols % tn == 0 and m % tm == 0
    j0 = col_start // tn
    out_shapes, out_specs, kouts = [], [], []
    for kind, lo, hi, width, scale, dtype in outs:
        if kind == "flat":
            out_shapes.append(jax.ShapeDtypeStruct((m, nj * (hi - lo)), dtype))
            out_specs.append(pl.BlockSpec((tm, hi - lo), lambda i, j: (i, j)))
        elif kind == "heads":
            nh = (hi - lo) // width
            out_shapes.append(jax.ShapeDtypeStruct((nj * nh, m, width), dtype))
            out_specs.append(pl.BlockSpec((nh, tm, width), lambda i, j: (j, i, 0)))
        elif kind == "lanes":
            nh = (hi - lo) // LANES
            out_shapes.append(jax.ShapeDtypeStruct((nh, m, nj * LANES), dtype))
            out_specs.append(pl.BlockSpec((nh, tm, LANES), lambda i, j: (0, i, j)))
        else:
            assert hi - lo == SUBLANES * LANES
            out_shapes.append(jax.ShapeDtypeStruct((m, nj, SUBLANES, LANES), dtype))
            out_specs.append(pl.BlockSpec((tm, None, SUBLANES, LANES), lambda i, j: (i, j, 0, 0)))
        kouts.append((kind, lo, hi, width, scale))
    return pl.pallas_call(
        functools.partial(_proj_kernel, outs=tuple(kouts), stream_weights=m <= MXU_DIM),
        out_shape=out_shapes,
        grid=(m // tm, nj),
        in_specs=[pl.BlockSpec((tm, k), lambda i, j: (i, 0)),
                  pl.BlockSpec((k, tn), lambda i, j: (0, j0 + j))],
        out_specs=out_specs,
        compiler_params=_cparams(("arbitrary", "arbitrary")),
        name=f"proj_c{col_start}",
    )(h2d, w_pad)


def _kth_largest(load, ntiles, k, axis, shape):
    part = (SUBLANES, shape[1]) if axis == 0 else shape

    def fold(x, op):
        if axis == 0:
            return op(x.reshape(x.shape[0] // SUBLANES, SUBLANES, x.shape[1]), axis=0)
        return op(x, axis=axis, keepdims=True)

    def finish(c, op):
        return op(c, axis=0, keepdims=True) if axis == 0 else c

    def reduce_tiles(fn, init):
        return lax.fori_loop(0, ntiles, lambda i, c: fn(load(i), c), init)

    def count_ge(t):
        c = reduce_tiles(lambda x, c: c + fold(jnp.where(x >= t, 1.0, 0.0), jnp.sum), jnp.zeros(part, F32))
        return finish(c, jnp.sum)

    def max_below(t, strict):
        def fn(x, c):
            keep = (x < t) if strict else (x <= t)
            return jnp.maximum(c, fold(jnp.where(keep, x, -jnp.inf), jnp.max))
        return finish(reduce_tiles(fn, jnp.full(part, -jnp.inf, F32)), jnp.max)

    def minmax(x, c):
        mn, mx = c
        mn = jnp.minimum(mn, fold(jnp.where(x > -jnp.inf, x, jnp.inf), jnp.min))
        mx = jnp.maximum(mx, fold(x, jnp.max))
        return mn, mx

    lo, hi = reduce_tiles(minmax, (jnp.full(part, jnp.inf, F32), jnp.full(part, -jnp.inf, F32)))
    lo, hi = finish(lo, jnp.min), finish(hi, jnp.max)
    kf = float(k)

    def bisect(_, c):
        lo, hi = c
        mid = 0.5 * lo + 0.5 * hi
        ok = count_ge(mid) >= kf
        return jnp.where(ok, mid, lo), jnp.where(ok, hi, mid)

    lo, hi = lax.fori_loop(0, N_BISECT, bisect, (lo, hi))
    v = max_below(hi, strict=False)
    c = count_ge(v)

    def walk(vc):
        v, c = vc
        v = jnp.where(c < kf, max_below(v, strict=True), v)
        return v, count_ge(v)

    v, _ = lax.while_loop(lambda vc: jnp.min(vc[1]) < kf, walk, (v, c))
    return v


def _slope_features(n):
    r = _alibi_slopes(n)
    parts = []
    for _ in range(3):
        p = r.astype(BF16).astype(F32)
        parts.append(p)
        r = r - p
    feat = jnp.stack([float(CHUNK) * p for p in parts] + parts, axis=1)
    return jnp.pad(feat, ((0, 0), (0, LANES - feat.shape[1])))


def _position_features(n_rows):
    pos = lax.broadcasted_iota(jnp.int32, (n_rows, HD), 0)
    lane = lax.broadcasted_iota(jnp.int32, (n_rows, HD), 1)
    feat = jnp.where(lane < 3, pos >> CHUNK_SHIFT, jnp.where(lane < 6, pos & (CHUNK - 1), 0))
    return feat.astype(F32).astype(BF16)


def _attn_a_kernel(slopes_ref, sfeat_ref, qa_ref, k_ref, v_ref, kw_ref, kwq_ref, qidx_ref, ga_ref, o_ref,
                   k16, vt16, ki16, st_scr, mb_scr, acc_scr, thr_scr, sq_scr, *, topk, n_heads):
    qi = pl.program_id(1)
    tq = qa_ref.shape[2]
    tk = tq
    rep = n_heads // KV_A
    nk = qi + 1

    @pl.when(qi == 0)
    def _():
        for g in range(KV_A):
            t = k16.shape[1]
            k16[g, :, :HD] = k_ref[pl.ds(g, t, stride=KV_A), :].astype(BF16)
            k16[g, :, HD:] = _position_features(t)
            for j in range(vt16.shape[1]):
                vt16[g, j] = jnp.transpose(v_ref[pl.ds(j * tk * KV_A + g, tk, stride=KV_A), :]).astype(BF16)
        ki16[...] = kw_ref[0, :, :D_IDX].astype(BF16)

    w_t = jnp.transpose(kwq_ref[0])
    w_sc = w_t[D_IDX:D_IDX + H_IDX, :] * (H_IDX ** -0.5)

    def score_tile(kj, c):
        ki = ki16[pl.ds(pl.multiple_of(kj * tk, tk), tk), :]
        acc = jnp.zeros((tk, tq), F32)
        for h in range(H_IDX):
            acc = acc + jnp.maximum(_dot_nt(ki, qidx_ref[h, 0]), 0.0) * w_sc[h:h + 1, :]
        st_scr[kj] = acc
        return c

    lax.fori_loop(0, nk, score_tile, 0)
    krow = lax.broadcasted_iota(jnp.int32, (tk, tq), 0)
    qcol = lax.broadcasted_iota(jnp.int32, (tk, tq), 1)
    st_scr[qi] = jnp.where((krow >> CHUNK_SHIFT) <= (qcol >> CHUNK_SHIFT), st_scr[qi], -jnp.inf)

    thr_scr[...] = jnp.full((1, tq), -3e38, F32)

    @pl.when(qi * tq + CHUNK >= topk)
    def _():
        thr_scr[...] = _kth_largest(lambda i: st_scr[i], nk, topk, 0, (1, tq))

    thr = thr_scr[...]

    def mask_tile(kj, c):
        mb_scr[kj] = jnp.where(st_scr[kj] >= thr, 0.0, NEG)
        return c

    lax.fori_loop(0, nk, mask_tile, 0)

    ahead = jnp.maximum(krow - qcol, 0).astype(F32)

    for g in range(KV_A):
        q_g = jnp.concatenate(
            [qa_ref[g * rep:(g + 1) * rep, 0].reshape(rep * tq, HD),
             jnp.concatenate([jnp.broadcast_to(sfeat_ref[g * rep + hh:g * rep + hh + 1, :], (tq, LANES))
                              for hh in range(rep)], axis=0).astype(BF16)], axis=1)
        acc_scr[...] = jnp.zeros(acc_scr.shape, F32)

        def logits(kj, dst):
            rows = pl.ds(pl.multiple_of(kj * tk, tk), tk)
            sq_scr[dst] = _dot_nt(k16[g, rows, :], q_g)

        def update(kj, src, carry, diagonal=False):
            m_old, l_old = carry
            mb = mb_scr[kj]
            s = jnp.concatenate(
                [sq_scr[src, :, hh * tq:(hh + 1) * tq]
                 + (mb - 2.0 * slopes_ref[g * rep + hh] * ahead if diagonal else mb) for hh in range(rep)],
                axis=1)
            m_new = jnp.maximum(m_old, jnp.max(s, axis=0, keepdims=True))
            alpha = jnp.exp2(m_old - m_new)
            p = jnp.exp2(s - m_new)
            l_new = alpha * l_old + jnp.sum(p, axis=0, keepdims=True)
            acc_scr[...] = alpha * acc_scr[...] + jnp.dot(vt16[g, kj], p.astype(BF16),
                                                          preferred_element_type=F32)
            return m_new, l_new

        def pair(p, carry):
            logits(2 * p + 1, 1)
            carry = update(2 * p, 0, carry)
            logits(2 * p + 2, 0)
            return update(2 * p + 1, 1, carry)

        def tail_odd(carry):
            logits(qi, 1)
            return update(qi, 1, update(qi - 1, 0, carry), diagonal=True)

        def tail_even(carry):
            return update(qi, 0, carry, diagonal=True)

        logits(0, 0)
        carry = lax.fori_loop(0, qi // 2, pair,
                              (jnp.full((1, rep * tq), NEG, F32), jnp.zeros((1, rep * tq), F32)))
        _, l = lax.cond(qi % 2 == 1, tail_odd, tail_even, carry)
        o_t = acc_scr[...] / l
        for hh in range(rep):
            cols = slice((g * rep + hh) * HD, (g * rep + hh + 1) * HD)
            o = jnp.transpose(o_t[:, hh * tq:(hh + 1) * tq])
            o_ref[0, :, cols] = (o * _silu(ga_ref[0, :, cols].astype(F32))).astype(o_ref.dtype)


def _attn_a_prompt(qa, k, v, kw, qidx, ga, topk, d_mix):
    n_heads, b, t, _ = qa.shape
    tq = min(256, topk)
    assert t % tq == 0 and tq % CHUNK == 0 and tq <= topk <= tq + CHUNK
    nkt = t // tq
    d_a = n_heads * HD
    return pl.pallas_call(
        functools.partial(_attn_a_kernel, topk=topk, n_heads=n_heads),
        out_shape=jax.ShapeDtypeStruct((b, t, d_mix), BF16),
        grid=(b, nkt),
        in_specs=[pl.BlockSpec(memory_space=pltpu.SMEM),
                  pl.BlockSpec((n_heads, LANES), lambda i, j: (0, 0)),
                  pl.BlockSpec((n_heads, 1, tq, HD), lambda i, j: (0, i, j, 0)),
                  pl.BlockSpec((t * KV_A, HD), lambda i, j: (i, 0)),
                  pl.BlockSpec((t * KV_A, HD), lambda i, j: (i, 0)),
                  pl.BlockSpec((1, t, 128), lambda i, j: (i, 0, 0)),
                  pl.BlockSpec((1, tq, 128), lambda i, j: (i, j, 0)),
                  pl.BlockSpec((H_IDX, 1, tq, D_IDX), lambda i, j: (0, i, j, 0)),
                  pl.BlockSpec((1, tq, d_a), lambda i, j: (i, j, 0))],
        out_specs=pl.BlockSpec((1, tq, d_a), lambda i, j: (i, j, 0)),
        scratch_shapes=[pltpu.VMEM((KV_A, t, 2 * HD), BF16),
                        pltpu.VMEM((KV_A, nkt, HD, tq), BF16),
                        pltpu.VMEM((t, D_IDX), BF16),
                        pltpu.VMEM((nkt, tq, tq), F32),
                        pltpu.VMEM((nkt, tq, tq), F32),
                        pltpu.VMEM((HD, n_heads // KV_A * tq), F32),
                        pltpu.VMEM((1, tq), F32),
                        pltpu.VMEM((2, tq, n_heads // KV_A * tq), F32)],
        compiler_params=_cparams(("arbitrary", "arbitrary")),
        name="attn_a_prompt",
    )(_alibi_slopes(n_heads), _slope_features(n_heads), qa, k, v, kw, kw, qidx, ga)


def _lam(lamp_ref):
    s1 = jnp.sum(lamp_ref[0:1, :] * lamp_ref[1:2, :], axis=1, keepdims=True)
    s2 = jnp.sum(lamp_ref[2:3, :] * lamp_ref[3:4, :], axis=1, keepdims=True)
    return jnp.exp(s1) - jnp.exp(s2) + LAM_INIT


def _subln_gate(o, subln, g):
    o = o * lax.rsqrt(jnp.mean(o * o, axis=-1, keepdims=True) + LN_EPS)
    o = o * subln * (1.0 - LAM_INIT)
    return o * _silu(g.astype(F32))


def _attn_b_kernel(slopes_ref, sfeat_ref, lamp_ref, subln_ref, q_ref, k_ref, v_ref, g_ref, mix_ref, o_ref,
                   vt_scr, s_scr, acc_scr, kaug_scr, qaug_scr, *, tq):
    h = pl.program_id(1)
    t = q_ref.shape[2]
    tk = tq
    nq = t // tq
    slope = slopes_ref[h]
    lam = _lam(lamp_ref)

    for j in range(nq):
        vt_scr[j] = jnp.transpose(v_ref[0, 0, j * tk:(j + 1) * tk, :].astype(F32)).astype(BF16)

    pos_feat = _position_features(t)
    slope_feat = jnp.broadcast_to(sfeat_ref[pl.ds(h, 1), :], (t, LANES)).astype(BF16)
    for mi in range(2):
        kaug_scr[mi, :, :HD] = k_ref[mi, 0]
        kaug_scr[mi, :, HD:] = pos_feat
        qaug_scr[mi, :, :HD] = q_ref[mi, 0]
        qaug_scr[mi, :, HD:] = slope_feat

    krow = lax.broadcasted_iota(jnp.int32, (tk, tq), 0)
    qcol = lax.broadcasted_iota(jnp.int32, (tk, tq), 1)
    ahead = jnp.maximum(krow - qcol, 0).astype(F32)
    diag_bias = jnp.where((krow >> CHUNK_SHIFT) <= (qcol >> CHUNK_SHIFT), -2.0 * slope * ahead, NEG)

    pairs = [(qi, j) for qi in range(nq) for j in range(qi + 1)]

    def issue_logits(n):
        qi, j = pairs[n]
        for mi in range(2):
            s_scr[n % 2, mi] = _dot_nt(kaug_scr[mi, j * tk:(j + 1) * tk, :], qaug_scr[mi, qi * tq:(qi + 1) * tq, :])

    issue_logits(0)
    carry = None
    for n, (qi, j) in enumerate(pairs):
        if n + 1 < len(pairs):
            issue_logits(n + 1)
        if j == 0:
            acc_scr[...] = jnp.zeros(acc_scr.shape, F32)
            init = (jnp.full((1, tq), NEG, F32), jnp.zeros((1, tq), F32))
            carry = [init, init]
        for mi in range(2):
            m_old, l_old = carry[mi]
            s = s_scr[n % 2, mi] + diag_bias if j == qi else s_scr[n % 2, mi]
            m_new = jnp.maximum(m_old, jnp.max(s, axis=0, keepdims=True))
            alpha = jnp.exp2(m_old - m_new)
            p = jnp.exp2(s - m_new)
            l_new = alpha * l_old + jnp.sum(p, axis=0, keepdims=True)
            acc_scr[mi] = alpha * acc_scr[mi] + jnp.dot(vt_scr[j], p.astype(BF16), preferred_element_type=F32)
            carry[mi] = (m_new, l_new)
        if j == qi:
            o_t = acc_scr[0] / carry[0][1] - lam * (acc_scr[1] / carry[1][1])
            o_t = o_t * lax.rsqrt(jnp.mean(o_t * o_t, axis=0, keepdims=True) + LN_EPS)
            o = jnp.transpose(o_t) * subln_ref[...] * (1.0 - LAM_INIT)
            rows = slice(qi * tq, (qi + 1) * tq)
            o_ref[0, rows, :] = (o * _silu(g_ref[0, rows, :].astype(F32))).astype(o_ref.dtype)


def _attn_b_prompt(qb, kb16, vb16, gb, lamp, subln, mix):
    _, b, t, _ = qb.shape
    n_heads = vb16.shape[0]
    col0 = (mix.shape[2] - n_heads * 2 * HD) // (2 * HD)
    tq = min(512, t)
    assert t % tq == 0 and tq % CHUNK == 0
    return pl.pallas_call(
        functools.partial(_attn_b_kernel, tq=tq),
        out_shape=jax.ShapeDtypeStruct(mix.shape, mix.dtype),
        grid=(b, n_heads),
        in_specs=[pl.BlockSpec(memory_space=pltpu.SMEM),
                  pl.BlockSpec((n_heads, LANES), lambda i, h: (0, 0)),
                  pl.BlockSpec((4, HD), lambda i, h: (0, 0)),
                  pl.BlockSpec((1, 2 * HD), lambda i, h: (0, 0)),
                  pl.BlockSpec((2, 1, t, HD), lambda i, h: (h, i, 0, 0)),
                  pl.BlockSpec((2, 1, t, HD), lambda i, h: (h, i, 0, 0)),
                  pl.BlockSpec((1, 1, t, 2 * HD), lambda i, h: (h, i, 0, 0)),
                  pl.BlockSpec((1, t, 2 * HD), lambda i, h: (i, 0, h)),
                  pl.BlockSpec(memory_space=pl.ANY)],
        out_specs=pl.BlockSpec((1, t, 2 * HD), lambda i, h: (i, 0, col0 + h)),
        input_output_aliases={8: 0},
        scratch_shapes=[pltpu.VMEM((t // tq, 2 * HD, tq), BF16),
                        pltpu.VMEM((2, 2, tq, tq), F32),
                        pltpu.VMEM((2, 2 * HD, tq), F32),
                        pltpu.VMEM((2, t, 2 * HD), BF16),
                        pltpu.VMEM((2, t, 2 * HD), BF16)],
        compiler_params=_cparams(("arbitrary", "arbitrary")),
        name="attn_b_prompt",
    )(_alibi_slopes(n_heads), _slope_features(n_heads), lamp, subln, qb, kb16, vb16, gb, mix)


def _attn_a_sample_kernel(slopes_ref, qa_ref, kc_ref, vc_ref, kic_ref, kn_ref, vn_ref, kwn_ref, qidx_ref,
                          ga_ref, o_ref, k16, v16, sc_scr, s_scr, p_scr, *, topk, n_heads, cw):
    t = qa_ref.shape[2]
    past = kc_ref.shape[0] // KV_A
    s_pad = sc_scr.shape[1]
    rep = n_heads // KV_A
    kcol = lax.broadcasted_iota(jnp.int32, (t, s_pad), 1)
    qrow = lax.broadcasted_iota(jnp.int32, (t, s_pad), 0)

    qi_all = qidx_ref[:, 0].reshape(H_IDX * t, D_IDX)
    w_q = kwn_ref[0, :, D_IDX:D_IDX + H_IDX] * (H_IDX ** -0.5)

    def scores(ki):
        r = jnp.maximum(_dot_nt(qi_all, ki), 0.0)
        acc = jnp.zeros((t, ki.shape[0]), F32)
        for h in range(H_IDX):
            acc = acc + r[h * t:(h + 1) * t, :] * w_q[:, h:h + 1]
        return acc

    for c0 in range(0, past, cw):
        sc_scr[:, c0:c0 + cw] = scores(kic_ref[0, c0:c0 + cw, :].astype(BF16))
    tail = s_pad - past
    ki_new = jnp.concatenate(
        [kwn_ref[0, :, :D_IDX], jnp.zeros((tail - t, D_IDX), F32)], axis=0).astype(BF16)
    tail_valid = lax.broadcasted_iota(jnp.int32, (t, tail), 1) < t
    sc_scr[:, past:] = jnp.where(tail_valid, scores(ki_new), -jnp.inf)

    thr = _kth_largest(lambda i: sc_scr[...], 1, topk, 1, (t, 1))
    mbias = jnp.where(sc_scr[...] >= thr, 0.0, NEG)
    dist = jnp.abs((past + qrow - kcol).astype(F32))

    for g in range(KV_A):
        k16[:past] = kc_ref[pl.ds(g, past, stride=KV_A), :].astype(BF16)
        v16[:past] = vc_ref[pl.ds(g, past, stride=KV_A), :].astype(BF16)
        zpad = jnp.zeros((tail - t, HD), F32)
        k16[past:] = jnp.concatenate([kn_ref[pl.ds(g, t, stride=KV_A), :], zpad], axis=0).astype(BF16)
        v16[past:] = jnp.concatenate([vn_ref[pl.ds(g, t, stride=KV_A), :], zpad], axis=0).astype(BF16)
        q_g = qa_ref[g * rep:(g + 1) * rep, 0].reshape(rep * t, HD)
        s_scr[...] = _dot_nt(q_g, k16[...])
        inv_l = []
        for hh in range(rep):
            rows = slice(hh * t, (hh + 1) * t)
            s = s_scr[rows, :] - slopes_ref[g * rep + hh] * dist + mbias
            p = jnp.exp2(s - jnp.max(s, axis=1, keepdims=True))
            inv_l.append(1.0 / jnp.sum(p, axis=1, keepdims=True))
            p_scr[rows, :] = p.astype(BF16)
        o_g = jnp.dot(p_scr[...], v16[...], preferred_element_type=F32)
        for hh in range(rep):
            hc = slice((g * rep + hh) * HD, (g * rep + hh + 1) * HD)
            o = o_g[hh * t:(hh + 1) * t, :] * inv_l[hh]
            o_ref[0, :, hc] = (o * _silu(ga_ref[0, :, hc].astype(F32))).astype(o_ref.dtype)


def _attn_a_sample(qa, kc, vc, kic, kn, vn, kwn, qidx, ga, topk, d_mix, cw=512):
    n_heads, b, t, _ = qa.shape
    past = kic.shape[1]
    s_pad = past + 128
    assert t <= 128 and past % cw == 0
    d_a = n_heads * HD
    return pl.pallas_call(
        functools.partial(_attn_a_sample_kernel, topk=topk, n_heads=n_heads, cw=cw),
        out_shape=jax.ShapeDtypeStruct((b, t, d_mix), BF16),
        grid=(b,),
        in_specs=[pl.BlockSpec(memory_space=pltpu.SMEM),
                  pl.BlockSpec((n_heads, 1, t, HD), lambda i: (0, i, 0, 0)),
                  pl.BlockSpec((past * KV_A, HD), lambda i: (i, 0)),
                  pl.BlockSpec((past * KV_A, HD), lambda i: (i, 0)),
                  pl.BlockSpec((1, past, D_IDX), lambda i: (i, 0, 0)),
                  pl.BlockSpec((t * KV_A, HD), lambda i: (i, 0)),
                  pl.BlockSpec((t * KV_A, HD), lambda i: (i, 0)),
                  pl.BlockSpec((1, t, 128), lambda i: (i, 0, 0)),
                  pl.BlockSpec((H_IDX, 1, t, D_IDX), lambda i: (0, i, 0, 0)),
                  pl.BlockSpec((1, t, d_a), lambda i: (i, 0, 0))],
        out_specs=pl.BlockSpec((1, t, d_a), lambda i: (i, 0, 0)),
        scratch_shapes=[pltpu.VMEM((s_pad, HD), BF16),
                        pltpu.VMEM((s_pad, HD), BF16),
                        pltpu.VMEM((t, s_pad), F32),
                        pltpu.VMEM((n_heads // KV_A * t, s_pad), F32),
                        pltpu.VMEM((n_heads // KV_A * t, s_pad), BF16)],
        compiler_params=_cparams(("arbitrary",)),
        name="attn_a_sample",
    )(_alibi_slopes(n_heads), qa, kc, vc, kic, kn, vn, kwn, qidx, ga)


def _attn_b_sample_kernel(slopes_ref, lamp_ref, subln_ref, q_ref, kc_ref, vc_ref, kn_ref, vn_ref, g_ref, mix_ref,
                          o_ref, m_scr, l_scr, acc_scr, *, n_heads, past, v_half_major):
    c = pl.program_id(1)
    t = q_ref.shape[2]
    slots = 2 * n_heads
    pc = kc_ref.shape[0] // slots
    lam = _lam(lamp_ref)

    @pl.when(c == 0)
    def _():
        m_scr[...] = jnp.full(m_scr.shape, NEG, F32)
        l_scr[...] = jnp.zeros(l_scr.shape, F32)
        acc_scr[...] = jnp.zeros(acc_scr.shape, F32)

    def q_blockdiag(h):
        z = jnp.zeros((t, HD), BF16)
        return jnp.concatenate([jnp.concatenate([q_ref[2 * h, 0], z], axis=1),
                                jnp.concatenate([z, q_ref[2 * h + 1, 0]], axis=1)], axis=0)

    def update(h, s, v16):
        m_old = m_scr[h]
        m_new = jnp.maximum(m_old, jnp.max(s, axis=1, keepdims=True))
        alpha = jnp.exp2(m_old - m_new)
        p = jnp.exp2(s - m_new)
        l_scr[h] = alpha * l_scr[h] + jnp.sum(p, axis=1, keepdims=True)
        acc_scr[h] = alpha * acc_scr[h] + jnp.dot(p.astype(BF16), v16, preferred_element_type=F32)
        m_scr[h] = m_new

    qrow = lax.broadcasted_iota(jnp.int32, (t, pc), 0)
    kcol = lax.broadcasted_iota(jnp.int32, (t, pc), 1)
    dist = (past + qrow - kcol).astype(F32) - (c * pc).astype(F32)
    for h in range(n_heads):
        k01 = jnp.concatenate([kc_ref[pl.ds(2 * h, pc, stride=slots), :],
                               kc_ref[pl.ds(2 * h + 1, pc, stride=slots), :]], axis=1).astype(BF16)
        v_h = jnp.concatenate([vc_ref[pl.ds(h, pc, stride=slots), :],
                               vc_ref[pl.ds(n_heads + h, pc, stride=slots), :]], axis=1).astype(BF16)
        bias = -slopes_ref[h] * dist
        update(h, _dot_nt(q_blockdiag(h), k01) + jnp.concatenate([bias, bias], axis=0), v_h)

    @pl.when(c == pl.num_programs(1) - 1)
    def _():
        qr = lax.broadcasted_iota(jnp.int32, (t, LANES), 0)
        kc_ = lax.broadcasted_iota(jnp.int32, (t, LANES), 1)
        dist_new = jnp.abs((qr - kc_).astype(F32))
        zpad = jnp.zeros((LANES - t, 2 * HD), F32)
        for h in range(n_heads):
            cols = slice(h * 2 * HD, (h + 1) * 2 * HD)
            kn = jnp.concatenate([kn_ref[0, :, cols], zpad], axis=0).astype(BF16)
            if v_half_major:
                v_new = jnp.concatenate([vn_ref[0, :, (half * n_heads + h) * HD:(half * n_heads + h + 1) * HD]
                                         for half in range(2)], axis=1)
            else:
                v_new = vn_ref[0, :, cols]
            vn = jnp.concatenate([v_new, zpad], axis=0).astype(BF16)
            bias = jnp.where(kc_ < t, -slopes_ref[h] * dist_new, NEG)
            update(h, _dot_nt(q_blockdiag(h), kn) + jnp.concatenate([bias, bias], axis=0), vn)
            o_all = acc_scr[h] / l_scr[h]
            o = o_all[:t] - lam * o_all[t:]
            o_ref[0, :, cols] = _subln_gate(o, subln_ref[...], g_ref[0, :, cols]).astype(o_ref.dtype)


def _attn_b_sample(qb, kc, vc, kn, vn, gb, lamp, subln, mix, past, v_half_major, pc=1024):
    _, b, t, _ = qb.shape
    d_b = gb.shape[2]
    n_heads = d_b // (2 * HD)
    slots = 2 * n_heads
    pc = min(pc, past)
    assert t <= LANES and past % pc == 0 and (mix.shape[2] - d_b) % d_b == 0
    nc = past // pc
    col0 = (mix.shape[2] - d_b) // d_b
    return pl.pallas_call(
        functools.partial(_attn_b_sample_kernel, n_heads=n_heads, past=past, v_half_major=v_half_major),
        out_shape=jax.ShapeDtypeStruct(mix.shape, mix.dtype),
        grid=(b, nc),
        in_specs=[pl.BlockSpec(memory_space=pltpu.SMEM),
                  pl.BlockSpec((4, HD), lambda i, c: (0, 0)),
                  pl.BlockSpec((1, 2 * HD), lambda i, c: (0, 0)),
                  pl.BlockSpec((slots, 1, t, HD), lambda i, c: (0, i, 0, 0)),
                  pl.BlockSpec((pc * slots, LANES), lambda i, c: (i * nc + c, 0)),
                  pl.BlockSpec((pc * slots, LANES), lambda i, c: (i * nc + c, 0)),
                  pl.BlockSpec((1, t, d_b), lambda i, c: (i, 0, 0)),
                  pl.BlockSpec((1, t, d_b), lambda i, c: (i, 0, 0)),
                  pl.BlockSpec((1, t, d_b), lambda i, c: (i, 0, 0)),
                  pl.BlockSpec(memory_space=pl.ANY)],
        out_specs=pl.BlockSpec((1, t, d_b), lambda i, c: (i, 0, col0)),
        input_output_aliases={9: 0},
        scratch_shapes=[pltpu.VMEM((n_heads, 2 * t, 1), F32),
                        pltpu.VMEM((n_heads, 2 * t, 1), F32),
                        pltpu.VMEM((n_heads, 2 * t, 2 * HD), F32)],
        compiler_params=_cparams(("arbitrary", "arbitrary")),
        name="attn_b_sample",
    )(_alibi_slopes(n_heads), lamp, subln, qb, kc, vc, kn, vn, gb, mix)


def _out_kernel(mix_ref, w_ref, x_ref, gate_ref, lng_ref, lnb_ref, o_ref, *, alpha, nj):
    j = pl.program_id(1)
    tn = x_ref.shape[1]
    r = alpha * x_ref[...] + gate_ref[...] * jnp.dot(mix_ref[...], w_ref[...], preferred_element_type=F32)
    o_ref[:, pl.ds(pl.multiple_of(j * tn, tn), tn)] = r

    @pl.when(j == nj - 1)
    def _():
        d = nj * tn
        tot = jnp.zeros((o_ref.shape[0], 1), F32)
        for c in range(nj):
            tot = tot + jnp.sum(o_ref[:, c * tn:(c + 1) * tn], axis=1, keepdims=True)
        mu = tot / d
        sq = jnp.zeros_like(tot)
        for c in range(nj):
            dv = o_ref[:, c * tn:(c + 1) * tn] - mu
            sq = sq + jnp.sum(dv * dv, axis=1, keepdims=True)
        inv = lax.rsqrt(sq / d + LN_EPS)
        for c in range(nj):
            cols = slice(c * tn, (c + 1) * tn)
            o_ref[:, cols] = (o_ref[:, cols] - mu) * inv * lng_ref[:, cols] + lnb_ref[:, cols]


def _out_proj(mix, w_out, x2d, gate, ln_g, ln_b, alpha, tm, tn, rows_per_gate):
    m, dm = mix.shape
    d = w_out.shape[1]
    assert m % tm == 0 and d % tn == 0
    if rows_per_gate == 1:
        gate_spec = pl.BlockSpec((tm, tn), lambda i, j: (i, j))
    else:
        assert rows_per_gate % tm == 0
        gate = gate.reshape(-1, 1, d)
        gate_spec = pl.BlockSpec((None, 1, tn), lambda i, j: (i * tm // rows_per_gate, 0, j))
    return pl.pallas_call(
        functools.partial(_out_kernel, alpha=alpha, nj=d // tn),
        out_shape=jax.ShapeDtypeStruct((m, d), F32),
        grid=(m // tm, d // tn),
        in_specs=[pl.BlockSpec((tm, dm), lambda i, j: (i, 0)),
                  pl.BlockSpec((dm, tn), lambda i, j: (0, j)),
                  pl.BlockSpec((tm, tn), lambda i, j: (i, j)),
                  gate_spec,
                  pl.BlockSpec((1, d), lambda i, j: (0, 0)),
                  pl.BlockSpec((1, d), lambda i, j: (0, 0))],
        out_specs=pl.BlockSpec((tm, d), lambda i, j: (i, 0), pipeline_mode=pl.Buffered(1)),
        compiler_params=_cparams(("arbitrary", "arbitrary")),
        name="out_proj",
    )(mix, w_out, x2d, gate, ln_g.reshape(1, d), ln_b.reshape(1, d))


def _w_in_layout(d_a, d_b, v_half_major):
    kv = KV_A * HD
    sizes = dict(q_a=d_a, k_a=kv, v_a=kv, g_a=d_a, q_idx=H_IDX * D_IDX, kw=D_IDX + H_IDX, q_b=d_b, k_b=d_b,
                 v_b=d_b, g_b=d_b)
    src, off = {}, 0
    for name in ("q_a", "k_a", "v_a", "g_a", "q_idx", "kw", "q_b", "k_b", "v_b", "g_b"):
        src[name] = off
        off += sizes[name]
    moves, dst = [], 0
    for name in ("k_a", "v_a", "kw"):
        moves.append((dst, src[name], sizes[name]))
        dst += sizes[name]
    zero = (dst, 1024)
    dst = 1024
    for name in ("q_a", "g_a", "q_idx", "q_b", "k_b"):
        moves.append((dst, src[name], sizes[name]))
        dst += sizes[name]
    n_b = d_b // (2 * HD)
    if v_half_major:
        for half in range(2):
            for h in range(n_b):
                moves.append((dst + (half * n_b + h) * HD, src["v_b"] + (h * 2 + half) * HD, HD))
    else:
        moves.append((dst, src["v_b"], d_b))
    dst += d_b
    moves.append((dst, src["g_b"], d_b))
    return moves, zero, dst + d_b


def _repack_kernel(src_ref, *refs):
    o_ref = refs[-1]
    for piece, w_ref in enumerate(refs[:-1]):
        o_ref[:, piece * LANES:(piece + 1) * LANES] = jnp.transpose(w_ref[...]).astype(o_ref.dtype)


ROW_ALIGN = 16


def _pad_w_in(w_in_t, d_a, d_b, v_half_major, per_step=4):
    moves, _, n_out = _w_in_layout(d_a, d_b, v_half_major)
    n_in, k = w_in_t.shape
    src_rows = np.zeros(n_out // LANES, np.int32)
    for dst, src, width in moves:
        for c in range(0, width, LANES):
            assert dst % LANES == 0 and (src + c) % ROW_ALIGN == 0 and src + c + LANES <= n_in
            src_rows[(dst + c) // LANES] = (src + c) // ROW_ALIGN
    return pl.pallas_call(
        _repack_kernel,
        out_shape=jax.ShapeDtypeStruct((k, n_out), BF16),
        grid_spec=pltpu.PrefetchScalarGridSpec(
            num_scalar_prefetch=1,
            grid=(n_out // (per_step * LANES),),
            in_specs=[pl.BlockSpec((pl.Element(LANES), pl.Element(k)),
                                   lambda i, src, p=p: (src[i * per_step + p] * ROW_ALIGN, 0))
                      for p in range(per_step)],
            out_specs=pl.BlockSpec((k, per_step * LANES), lambda i, src: (0, i))),
        compiler_params=_cparams(("arbitrary",)),
        name="repack_w_in",
    )(jnp.asarray(src_rows), *([w_in_t] * per_step))


def _project(x, shift, scale, w_pad, d_a, d_b, tm, want_b16, native):
    kv = KV_A * HD
    h2d, k_a, v_a, kw = _modulate_project(
        x, shift, scale, w_pad, [(0, kv, True), (kv, 2 * kv, True), (2 * kv, 2 * kv + 128, False)],
        tm=min(x.shape[1], 512))
    c = 1024
    tn = 512
    q_scale = HD ** -0.5 * LOG2E
    (q_a,) = _proj(h2d, w_pad, c, d_a, tn, tm, [("heads", 0, tn, HD, q_scale, BF16)])
    c += d_a
    (g_a,) = _proj(h2d, w_pad, c, d_a, tn, tm, [("flat", 0, tn, 0, 1.0, BF16)])
    c += d_a
    (q_idx,) = _proj(h2d, w_pad, c, H_IDX * D_IDX, tn, tm, [("heads", 0, tn, D_IDX, 1.0, BF16)])
    c += H_IDX * D_IDX
    (q_b,) = _proj(h2d, w_pad, c, d_b, tn, tm, [("heads", 0, tn, HD, q_scale, BF16)])
    c += d_b
    tn_b = tn
    kb_outs = [("flat", 0, tn, 0, 1.0, F32)]
    vb_outs = [("flat", 0, tn, 0, 1.0, F32)]
    if want_b16 and native:
        tn_b = SUBLANES * LANES
        kb_outs = [("native", 0, tn_b, 0, 1.0, F32), ("heads", 0, tn_b, HD, 1.0, BF16)]
        vb_outs = [("native", 0, tn_b, 0, 1.0, F32), ("lanes", 0, tn_b, 0, 1.0, BF16)]
    elif want_b16:
        kb_outs.append(("heads", 0, tn, HD, 1.0, BF16))
        vb_outs.append(("heads", 0, tn, 2 * HD, 1.0, BF16))
    k_b = _proj(h2d, w_pad, c, d_b, tn_b, tm, kb_outs)
    c += d_b
    v_b = _proj(h2d, w_pad, c, d_b, tn_b, tm, vb_outs)
    c += d_b
    (g_b,) = _proj(h2d, w_pad, c, d_b, tn, tm, [("flat", 0, tn, 0, 1.0, BF16)])
    return dict(k_a=k_a, v_a=v_a, kw=kw, q_a=q_a, g_a=g_a, q_idx=q_idx, q_b=q_b, k_b=k_b, v_b=v_b, g_b=g_b)


def _layer(x, mod, past, w_pad, w_out, lamp, subln, ln_g, ln_b, alpha, d_a, d_b, native):
    b, t, d = x.shape
    h_b = d_b // (2 * HD)
    m = b * t
    shift, scale, gate = (mod[:, i * d:(i + 1) * d] for i in range(3))
    prompt = past is None

    p = _project(x, shift.reshape(b, 1, d), scale.reshape(b, 1, d), w_pad, d_a, d_b, tm=min(m, 1024),
                 want_b16=prompt, native=native)

    r3 = lambda a: a.reshape(b, t, a.shape[-1])
    r4 = lambda a: a.reshape(a.shape[0], b, t, a.shape[-1])
    k_a, v_a = p["k_a"], p["v_a"]
    kw, g_a, g_b = r3(p["kw"]), r3(p["g_a"]), r3(p["g_b"])
    q_a, q_idx, q_b = r4(p["q_a"]), r4(p["q_idx"]), r4(p["q_b"])

    if prompt:
        topk = min(TOPK_MAX, t // 4)
        mix = _attn_a_prompt(q_a, k_a, v_a, kw, q_idx, g_a, topk, d_a + d_b)
        mix = _attn_b_prompt(q_b, r4(p["k_b"][1]), r4(p["v_b"][1]), g_b, lamp, subln, mix)
        tm_out, rows_per_gate, gate_rows = min(m, 1024), t, gate
        new_k_b = p["k_b"][0].reshape(1, b, t, h_b, 2, HD)
        if native:
            new_v_b = p["v_b"][0].reshape(b, t, 2, h_b, HD).transpose(0, 1, 3, 2, 4).reshape(1, b, t, h_b, 2 * HD)
        else:
            new_v_b = p["v_b"][0].reshape(1, b, t, h_b, 2 * HD)
    else:
        kc_a, vc_a, kic, kc_b, vc_b = past
        plen = kic.shape[1]
        topk = min(TOPK_MAX, (plen + t) // 4)
        k_b, v_b = r3(p["k_b"][0]), r3(p["v_b"][0])
        rows = lambda a: a.reshape(-1, LANES)
        vc_b_rows = rows(vc_b.reshape(b, plen, h_b, 2, HD).transpose(0, 1, 3, 2, 4))
        mix = _attn_a_sample(q_a, rows(kc_a), rows(vc_a), kic, k_a, v_a, kw, q_idx, g_a, topk, d_a + d_b)
        mix = _attn_b_sample(q_b, rows(kc_b), vc_b_rows, k_b, v_b, g_b, lamp, subln, mix, plen, native)
        tm_out, rows_per_gate = m, 1
        gate_rows = jnp.broadcast_to(gate[:, None, :], (b, t, d)).reshape(m, d)
        new_k_b = k_b.reshape(1, b, t, h_b, 2, HD)
        if native:
            new_v_b = v_b.reshape(b, t, 2, h_b, HD).transpose(0, 1, 3, 2, 4).reshape(1, b, t, h_b, 2 * HD)
        else:
            new_v_b = v_b.reshape(1, b, t, h_b, 2 * HD)

    y = _out_proj(mix.reshape(m, d_a + d_b), w_out, x.reshape(m, d), gate_rows, ln_g, ln_b, alpha,
                  tm=tm_out, tn=512, rows_per_gate=rows_per_gate)
    rows_out = (k_a.reshape(1, b, t, KV_A, HD), v_a.reshape(1, b, t, KV_A, HD), kw[None, :, :, :D_IDX],
                new_k_b, new_v_b)
    return y.reshape(b, t, d), rows_out


def kernel(x_prompt, x_sample, cache_a_k, cache_a_v, cache_a_kidx, cache_b_k, cache_b_v, c_prompt, c_sample,
           w_ada, b_ada, w_in, w_out, lam_q1, lam_k1, lam_q2, lam_k2, subln_g, ln_g, ln_b):
    depth, d, _ = w_ada.shape
    assert depth == 1, "single-layer step"
    d_b = cache_b_v.shape[3] * cache_b_v.shape[4]
    d_a = w_out.shape[1] - d_b
    alpha = (2.0 * depth) ** 0.25
    bp = x_prompt.shape[0]

    mod = _ada(jnp.concatenate([c_prompt, c_sample], axis=0), w_ada[0], b_ada[0])
    native = d_b == 2 * SUBLANES * LANES
    w_pad = _pad_w_in(jnp.transpose(w_in[0]), d_a, d_b, native)
    lamp = jnp.concatenate([lam_q1, lam_k1, lam_q2, lam_k2], axis=0)
    common = (w_pad, w_out[0].astype(BF16), lamp, subln_g, ln_g[0], ln_b[0], alpha, d_a, d_b, native)

    y_p, rows_p = _layer(x_prompt, mod[:bp], None, *common)
    past = (cache_a_k[0], cache_a_v[0], cache_a_kidx[0], cache_b_k[0], cache_b_v[0])
    y_s, rows_s = _layer(x_sample, mod[bp:], past, *common)
    return (y_p, y_s) + rows_p + rows_s
```

```python
import functools
import math

import jax
import jax.numpy as jnp
import numpy as np
from jax import lax
from jax.experimental import pallas as pl
from jax.experimental.pallas import tpu as pltpu

F32 = jnp.float32
BF16 = jnp.bfloat16

HD = 128
CHUNK = 64
CHUNK_SHIFT = 6
KV_A = 2
H_IDX = 16
D_IDX = 64
TOPK_MAX = 256
LN_EPS = 1e-5
LAM_INIT = 0.8 - 0.6 * math.exp(-0.3 * 0)

LOG2E = math.log2(math.e)
NEG = -1e30
N_BISECT = 16

VMEM_LIMIT = 56 * 1024 * 1024
MXU_DIM = 256
LANES = 128
SUBLANES = 8


def _cparams(sem):
    return pltpu.CompilerParams(dimension_semantics=sem, vmem_limit_bytes=VMEM_LIMIT)


def _silu(x):
    return x * jax.nn.sigmoid(x)


def _dot_nt(a, b):
    return lax.dot_general(a, b, (((1,), (1,)), ((), ())), preferred_element_type=F32)


def _alibi_slopes(n):
    return jnp.asarray(2.0 ** (-8.0 * np.arange(1, n + 1) / n), dtype=F32) * LOG2E


def _ada_kernel(c_ref, w_ref, b_ref, o_ref):
    s = _silu(c_ref[...]).astype(BF16)
    o_ref[...] = jnp.dot(s, w_ref[...].astype(BF16), preferred_element_type=F32) + b_ref[...]


def _ada(c, w_ada, b_ada, tn=512):
    n, d = c.shape
    e = w_ada.shape[1]
    return pl.pallas_call(
        _ada_kernel,
        out_shape=jax.ShapeDtypeStruct((n, e), F32),
        grid=(e // tn,),
        in_specs=[pl.BlockSpec((n, d), lambda j: (0, 0)),
                  pl.BlockSpec((d, tn), lambda j: (0, j)),
                  pl.BlockSpec((1, tn), lambda j: (0, j))],
        out_specs=pl.BlockSpec((n, tn), lambda j: (0, j)),
        compiler_params=_cparams(("arbitrary",)),
        name="ada",
    )(c, w_ada, b_ada.reshape(1, e))


def _modproj_kernel(x_ref, shift_ref, scale_ref, w_ref, h_ref, *o_refs, splits):
    h = (x_ref[0] * (1.0 + scale_ref[0]) + shift_ref[0]).astype(h_ref.dtype)
    h_ref[...] = h
    acc = jnp.dot(h, w_ref[...], preferred_element_type=F32)
    tm = acc.shape[0]
    for o_ref, (lo, hi, interleave) in zip(o_refs, splits):
        if interleave:
            for g in range((hi - lo) // LANES):
                o_ref[pl.ds(g, tm, stride=(hi - lo) // LANES), :] = acc[:, lo + g * LANES:lo + (g + 1) * LANES]
        else:
            o_ref[...] = acc[:, lo:hi]


def _modulate_project(x, shift, scale, w_pad, splits, tm):
    b, t, d = x.shape
    n = splits[-1][1]
    nt = t // tm

    def out_shape_block(split):
        lo, hi, interleave = split
        groups = (hi - lo) // LANES if interleave else 1
        return (b * t * groups, (hi - lo) // groups), (tm * groups, (hi - lo) // groups)

    return pl.pallas_call(
        functools.partial(_modproj_kernel, splits=tuple(splits)),
        out_shape=[jax.ShapeDtypeStruct((b * t, d), BF16)]
        + [jax.ShapeDtypeStruct(shape, F32) for shape, _ in map(out_shape_block, splits)],
        grid=(b, nt),
        in_specs=[pl.BlockSpec((1, tm, d), lambda i, j: (i, j, 0)),
                  pl.BlockSpec((1, 1, d), lambda i, j: (i, 0, 0)),
                  pl.BlockSpec((1, 1, d), lambda i, j: (i, 0, 0)),
                  pl.BlockSpec((d, n), lambda i, j: (0, 0))],
        out_specs=[pl.BlockSpec((tm, d), lambda i, j: (i * nt + j, 0))]
        + [pl.BlockSpec(block, lambda i, j: (i * nt + j, 0)) for _, block in map(out_shape_block, splits)],
        compiler_params=_cparams(("arbitrary", "arbitrary")),
        name="modulate_proj",
    )(x, shift, scale, w_pad)


def _proj_kernel(h_ref, w_ref, *o_refs, outs, stream_weights):
    if stream_weights:
        acc = jnp.transpose(_dot_nt(jnp.transpose(w_ref[...]), h_ref[...]))
    else:
        acc = jnp.dot(h_ref[...], w_ref[...], preferred_element_type=F32)
    for o_ref, (kind, lo, hi, width, scale) in zip(o_refs, outs):
        if kind == "flat":
            o_ref[...] = acc[:, lo:hi].astype(o_ref.dtype)
        elif kind == "heads":
            for hh in range((hi - lo) // width):
                blk = acc[:, lo + hh * width: lo + (hh + 1) * width]
                if scale != 1.0:
                    blk = blk * scale
                o_ref[hh] = blk.astype(o_ref.dtype)
        elif kind == "lanes":
            for hh in range((hi - lo) // LANES):
                o_ref[hh] = acc[:, lo + hh * LANES: lo + (hh + 1) * LANES].astype(o_ref.dtype)
        else:
            for s in range(SUBLANES):
                o_ref[:, s, :] = acc[:, lo + s * LANES: lo + (s + 1) * LANES].astype(o_ref.dtype)


def _proj(h2d, w_pad, col_start, ncols, tn, tm, outs):
    m, k = h2d.shape
    nj = ncols // tn
    assert col_start % tn == 0 and ncols % tn == 0 and m % tm == 0
    j0 = col_start // tn
    out_shapes, out_specs, kouts = [], [], []
    for kind, lo, hi, width, scale, dtype in outs:
        if kind == "flat":
            out_shapes.append(jax.ShapeDtypeStruct((m, nj * (hi - lo)), dtype))
            out_specs.append(pl.BlockSpec((tm, hi - lo), lambda i, j: (i, j)))
        elif kind == "heads":
            nh = (hi - lo) // width
            out_shapes.append(jax.ShapeDtypeStruct((nj * nh, m, width), dtype))
            out_specs.append(pl.BlockSpec((nh, tm, width), lambda i, j: (j, i, 0)))
        elif kind == "lanes":
            nh = (hi - lo) // LANES
            out_shapes.append(jax.ShapeDtypeStruct((nh, m, nj * LANES), dtype))
            out_specs.append(pl.BlockSpec((nh, tm, LANES), lambda i, j: (0, i, j)))
        else:
            assert hi - lo == SUBLANES * LANES
            out_shapes.append(jax.ShapeDtypeStruct((m, nj, SUBLANES, LANES), dtype))
            out_specs.append(pl.BlockSpec((tm, None, SUBLANES, LANES), lambda i, j: (i, j, 0, 0)))
        kouts.append((kind, lo, hi, width, scale))
    return pl.pallas_call(
        functools.partial(_proj_kernel, outs=tuple(kouts), stream_weights=m <= MXU_DIM),
        out_shape=out_shapes,
        grid=(m // tm, nj),
        in_specs=[pl.BlockSpec((tm, k), lambda i, j: (i, 0)),
                  pl.BlockSpec((k, tn), lambda i, j: (0, j0 + j))],
        out_specs=out_specs,
        compiler_params=_cparams(("arbitrary", "arbitrary")),
        name=f"proj_c{col_start}",
    )(h2d, w_pad)


def _kth_largest(load, ntiles, k, axis, shape):
    part = (SUBLANES, shape[1]) if axis == 0 else shape

    def fold(x, op):
        if axis == 0:
            return op(x.reshape(x.shape[0] // SUBLANES, SUBLANES, x.shape[1]), axis=0)
        return op(x, axis=axis, keepdims=True)

    def finish(c, op):
        return op(c, axis=0, keepdims=True) if axis == 0 else c

    def reduce_tiles(fn, init):
        return lax.fori_loop(0, ntiles, lambda i, c: fn(load(i), c), init)

    def count_ge(t):
        c = reduce_tiles(lambda x, c: c + fold(jnp.where(x >= t, 1.0, 0.0), jnp.sum), jnp.zeros(part, F32))
        return finish(c, jnp.sum)

    def max_below(t, strict):
        def fn(x, c):
            keep = (x < t) if strict else (x <= t)
            return jnp.maximum(c, fold(jnp.where(keep, x, -jnp.inf), jnp.max))
        return finish(reduce_tiles(fn, jnp.full(part, -jnp.inf, F32)), jnp.max)

    def minmax(x, c):
        mn, mx = c
        mn = jnp.minimum(mn, fold(jnp.where(x > -jnp.inf, x, jnp.inf), jnp.min))
        mx = jnp.maximum(mx, fold(x, jnp.max))
        return mn, mx

    lo, hi = reduce_tiles(minmax, (jnp.full(part, jnp.inf, F32), jnp.full(part, -jnp.inf, F32)))
    lo, hi = finish(lo, jnp.min), finish(hi, jnp.max)
    kf = float(k)

    def bisect(_, c):
        lo, hi = c
        mid = 0.5 * lo + 0.5 * hi
        ok = count_ge(mid) >= kf
        return jnp.where(ok, mid, lo), jnp.where(ok, hi, mid)

    lo, hi = lax.fori_loop(0, N_BISECT, bisect, (lo, hi))
    v = max_below(hi, strict=False)
    c = count_ge(v)

    def walk(vc):
        v, c = vc
        v = jnp.where(c < kf, max_below(v, strict=True), v)
        return v, count_ge(v)

    v, _ = lax.while_loop(lambda vc: jnp.min(vc[1]) < kf, walk, (v, c))
    return v


def _slope_features(n):
    r = _alibi_slopes(n)
    parts = []
    for _ in range(3):
        p = r.astype(BF16).astype(F32)
        parts.append(p)
        r = r - p
    feat = jnp.stack([float(CHUNK) * p for p in parts] + parts, axis=1)
    return jnp.pad(feat, ((0, 0), (0, LANES - feat.shape[1])))


def _position_features(n_rows):
    pos = lax.broadcasted_iota(jnp.int32, (n_rows, HD), 0)
    lane = lax.broadcasted_iota(jnp.int32, (n_rows, HD), 1)
    feat = jnp.where(lane < 3, pos >> CHUNK_SHIFT, jnp.where(lane < 6, pos & (CHUNK - 1), 0))
    return feat.astype(F32).astype(BF16)


def _attn_a_kernel(slopes_ref, sfeat_ref, qa_ref, k_ref, v_ref, kw_ref, kwq_ref, qidx_ref, ga_ref, o_ref,
                   k16, vt16, ki16, st_scr, mb_scr, acc_scr, thr_scr, sq_scr, *, topk, n_heads):
    qi = pl.program_id(1)
    tq = qa_ref.shape[2]
    tk = tq
    rep = n_heads // KV_A
    nk = qi + 1

    @pl.when(qi == 0)
    def _():
        for g in range(KV_A):
            t = k16.shape[1]
            k16[g, :, :HD] = k_ref[pl.ds(g, t, stride=KV_A), :].astype(BF16)
            k16[g, :, HD:] = _position_features(t)
            for j in range(vt16.shape[1]):
                vt16[g, j] = jnp.transpose(v_ref[pl.ds(j * tk * KV_A + g, tk, stride=KV_A), :]).astype(BF16)
        ki16[...] = kw_ref[0, :, :D_IDX].astype(BF16)

    w_t = jnp.transpose(kwq_ref[0])
    w_sc = w_t[D_IDX:D_IDX + H_IDX, :] * (H_IDX ** -0.5)

    def score_tile(kj, c):
        ki = ki16[pl.ds(pl.multiple_of(kj * tk, tk), tk), :]
        acc = jnp.zeros((tk, tq), F32)
        for h in range(H_IDX):
            acc = acc + jnp.maximum(_dot_nt(ki, qidx_ref[h, 0]), 0.0) * w_sc[h:h + 1, :]
        st_scr[kj] = acc
        return c

    lax.fori_loop(0, nk, score_tile, 0)
    krow = lax.broadcasted_iota(jnp.int32, (tk, tq), 0)
    qcol = lax.broadcasted_iota(jnp.int32, (tk, tq), 1)
    st_scr[qi] = jnp.where((krow >> CHUNK_SHIFT) <= (qcol >> CHUNK_SHIFT), st_scr[qi], -jnp.inf)

    thr_scr[...] = jnp.full((1, tq), -3e38, F32)

    @pl.when(qi * tq + CHUNK >= topk)
    def _():
        thr_scr[...] = _kth_largest(lambda i: st_scr[i], nk, topk, 0, (1, tq))

    thr = thr_scr[...]

    def mask_tile(kj, c):
        mb_scr[kj] = jnp.where(st_scr[kj] >= thr, 0.0, NEG)
        return c

    lax.fori_loop(0, nk, mask_tile, 0)

    ahead = jnp.maximum(krow - qcol, 0).astype(F32)

    for g in range(KV_A):
        q_g = jnp.concatenate(
            [qa_ref[g * rep:(g + 1) * rep, 0].reshape(rep * tq, HD),
             jnp.concatenate([jnp.broadcast_to(sfeat_ref[g * rep + hh:g * rep + hh + 1, :], (tq, LANES))
                              for hh in range(rep)], axis=0).astype(BF16)], axis=1)
        acc_scr[...] = jnp.zeros(acc_scr.shape, F32)

        def logits(kj, dst):
            rows = pl.ds(pl.multiple_of(kj * tk, tk), tk)
            sq_scr[dst] = _dot_nt(k16[g, rows, :], q_g)

        def update(kj, src, carry, diagonal=False):
            m_old, l_old = carry
            mb = mb_scr[kj]
            s = jnp.concatenate(
                [sq_scr[src, :, hh * tq:(hh + 1) * tq]
                 + (mb - 2.0 * slopes_ref[g * rep + hh] * ahead if diagonal else mb) for hh in range(rep)],
                axis=1)
            m_new = jnp.maximum(m_old, jnp.max(s, axis=0, keepdims=True))
            alpha = jnp.exp2(m_old - m_new)
            p = jnp.exp2(s - m_new)
            l_new = alpha * l_old + jnp.sum(p, axis=0, keepdims=True)
            acc_scr[...] = alpha * acc_scr[...] + jnp.dot(vt16[g, kj], p.astype(BF16),
                                                          preferred_element_type=F32)
            return m_new, l_new

        def pair(p, carry):
            logits(2 * p + 1, 1)
            carry = update(2 * p, 0, carry)
            logits(2 * p + 2, 0)
            return update(2 * p + 1, 1, carry)

        def tail_odd(carry):
            logits(qi, 1)
            return update(qi, 1, update(qi - 1, 0, carry), diagonal=True)

        def tail_even(carry):
            return update(qi, 0, carry, diagonal=True)

        logits(0, 0)
        carry = lax.fori_loop(0, qi // 2, pair,
                              (jnp.full((1, rep * tq), NEG, F32), jnp.zeros((1, rep * tq), F32)))
        _, l = lax.cond(qi % 2 == 1, tail_odd, tail_even, carry)
        o_t = acc_scr[...] / l
        for hh in range(rep):
            cols = slice((g * rep + hh) * HD, (g * rep + hh + 1) * HD)
            o = jnp.transpose(o_t[:, hh * tq:(hh + 1) * tq])
            o_ref[0, :, cols] = (o * _silu(ga_ref[0, :, cols].astype(F32))).astype(o_ref.dtype)


def _attn_a_prompt(qa, k, v, kw, qidx, ga, topk, d_mix):
    n_heads, b, t, _ = qa.shape
    tq = min(256, topk)
    assert t % tq == 0 and tq % CHUNK == 0 and tq <= topk <= tq + CHUNK
    nkt = t // tq
    d_a = n_heads * HD
    return pl.pallas_call(
        functools.partial(_attn_a_kernel, topk=topk, n_heads=n_heads),
        out_shape=jax.ShapeDtypeStruct((b, t, d_mix), BF16),
        grid=(b, nkt),
        in_specs=[pl.BlockSpec(memory_space=pltpu.SMEM),
                  pl.BlockSpec((n_heads, LANES), lambda i, j: (0, 0)),
                  pl.BlockSpec((n_heads, 1, tq, HD), lambda i, j: (0, i, j, 0)),
                  pl.BlockSpec((t * KV_A, HD), lambda i, j: (i, 0)),
                  pl.BlockSpec((t * KV_A, HD), lambda i, j: (i, 0)),
                  pl.BlockSpec((1, t, 128), lambda i, j: (i, 0, 0)),
                  pl.BlockSpec((1, tq, 128), lambda i, j: (i, j, 0)),
                  pl.BlockSpec((H_IDX, 1, tq, D_IDX), lambda i, j: (0, i, j, 0)),
                  pl.BlockSpec((1, tq, d_a), lambda i, j: (i, j, 0))],
        out_specs=pl.BlockSpec((1, tq, d_a), lambda i, j: (i, j, 0)),
        scratch_shapes=[pltpu.VMEM((KV_A, t, 2 * HD), BF16),
                        pltpu.VMEM((KV_A, nkt, HD, tq), BF16),
                        pltpu.VMEM((t, D_IDX), BF16),
                        pltpu.VMEM((nkt, tq, tq), F32),
                        pltpu.VMEM((nkt, tq, tq), F32),
                        pltpu.VMEM((HD, n_heads // KV_A * tq), F32),
                        pltpu.VMEM((1, tq), F32),
                        pltpu.VMEM((2, tq, n_heads // KV_A * tq), F32)],
        compiler_params=_cparams(("arbitrary", "arbitrary")),
        name="attn_a_prompt",
    )(_alibi_slopes(n_heads), _slope_features(n_heads), qa, k, v, kw, kw, qidx, ga)


def _lam(lamp_ref):
    s1 = jnp.sum(lamp_ref[0:1, :] * lamp_ref[1:2, :], axis=1, keepdims=True)
    s2 = jnp.sum(lamp_ref[2:3, :] * lamp_ref[3:4, :], axis=1, keepdims=True)
    return jnp.exp(s1) - jnp.exp(s2) + LAM_INIT


def _subln_gate(o, subln, g):
    o = o * lax.rsqrt(jnp.mean(o * o, axis=-1, keepdims=True) + LN_EPS)
    o = o * subln * (1.0 - LAM_INIT)
    return o * _silu(g.astype(F32))


def _attn_b_kernel(slopes_ref, sfeat_ref, lamp_ref, subln_ref, q_ref, k_ref, v_ref, g_ref, mix_ref, o_ref,
                   vt_scr, s_scr, acc_scr, kaug_scr, qaug_scr, *, tq):
    h = pl.program_id(1)
    t = q_ref.shape[2]
    tk = tq
    nq = t // tq
    slope = slopes_ref[h]
    lam = _lam(lamp_ref)

    for j in range(nq):
        vt_scr[j] = jnp.transpose(v_ref[0, 0, j * tk:(j + 1) * tk, :].astype(F32)).astype(BF16)

    pos_feat = _position_features(t)
    slope_feat = jnp.broadcast_to(sfeat_ref[pl.ds(h, 1), :], (t, LANES)).astype(BF16)
    for mi in range(2):
        kaug_scr[mi, :, :HD] = k_ref[mi, 0]
        kaug_scr[mi, :, HD:] = pos_feat
        qaug_scr[mi, :, :HD] = q_ref[mi, 0]
        qaug_scr[mi, :, HD:] = slope_feat

    krow = lax.broadcasted_iota(jnp.int32, (tk, tq), 0)
    qcol = lax.broadcasted_iota(jnp.int32, (tk, tq), 1)
    ahead = jnp.maximum(krow - qcol, 0).astype(F32)
    diag_bias = jnp.where((krow >> CHUNK_SHIFT) <= (qcol >> CHUNK_SHIFT), -2.0 * slope * ahead, NEG)

    pairs = [(qi, j) for qi in range(nq) for j in range(qi + 1)]

    def issue_logits(n):
        qi, j = pairs[n]
        for mi in range(2):
            s_scr[n % 2, mi] = _dot_nt(kaug_scr[mi, j * tk:(j + 1) * tk, :], qaug_scr[mi, qi * tq:(qi + 1) * tq, :])

    issue_logits(0)
    carry = None
    for n, (qi, j) in enumerate(pairs):
        if n + 1 < len(pairs):
            issue_logits(n + 1)
        if j == 0:
            acc_scr[...] = jnp.zeros(acc_scr.shape, F32)
            init = (jnp.full((1, tq), NEG, F32), jnp.zeros((1, tq), F32))
            carry = [init, init]
        for mi in range(2):
            m_old, l_old = carry[mi]
            s = s_scr[n % 2, mi] + diag_bias if j == qi else s_scr[n % 2, mi]
            m_new = jnp.maximum(m_old, jnp.max(s, axis=0, keepdims=True))
            alpha = jnp.exp2(m_old - m_new)
            p = jnp.exp2(s - m_new)
            l_new = alpha * l_old + jnp.sum(p, axis=0, keepdims=True)
            acc_scr[mi] = alpha * acc_scr[mi] + jnp.dot(vt_scr[j], p.astype(BF16), preferred_element_type=F32)
            carry[mi] = (m_new, l_new)
        if j == qi:
            o_t = acc_scr[0] / carry[0][1] - lam * (acc_scr[1] / carry[1][1])
            o_t = o_t * lax.rsqrt(jnp.mean(o_t * o_t, axis=0, keepdims=True) + LN_EPS)
            o = jnp.transpose(o_t) * subln_ref[...] * (1.0 - LAM_INIT)
            rows = slice(qi * tq, (qi + 1) * tq)
            o_ref[0, rows, :] = (o * _silu(g_ref[0, rows, :].astype(F32))).astype(o_ref.dtype)


def _attn_b_prompt(qb, kb16, vb16, gb, lamp, subln, mix):
    _, b, t, _ = qb.shape
    n_heads = vb16.shape[0]
    col0 = (mix.shape[2] - n_heads * 2 * HD) // (2 * HD)
    tq = min(512, t)
    assert t % tq == 0 and tq % CHUNK == 0
    return pl.pallas_call(
        functools.partial(_attn_b_kernel, tq=tq),
        out_shape=jax.ShapeDtypeStruct(mix.shape, mix.dtype),
        grid=(b, n_heads),
        in_specs=[pl.BlockSpec(memory_space=pltpu.SMEM),
                  pl.BlockSpec((n_heads, LANES), lambda i, h: (0, 0)),
                  pl.BlockSpec((4, HD), lambda i, h: (0, 0)),
                  pl.BlockSpec((1, 2 * HD), lambda i, h: (0, 0)),
                  pl.BlockSpec((2, 1, t, HD), lambda i, h: (h, i, 0, 0)),
                  pl.BlockSpec((2, 1, t, HD), lambda i, h: (h, i, 0, 0)),
                  pl.BlockSpec((1, 1, t, 2 * HD), lambda i, h: (h, i, 0, 0)),
                  pl.BlockSpec((1, t, 2 * HD), lambda i, h: (i, 0, h)),
                  pl.BlockSpec(memory_space=pl.ANY)],
        out_specs=pl.BlockSpec((1, t, 2 * HD), lambda i, h: (i, 0, col0 + h)),
        input_output_aliases={8: 0},
        scratch_shapes=[pltpu.VMEM((t // tq, 2 * HD, tq), BF16),
                        pltpu.VMEM((2, 2, tq, tq), F32),
                        pltpu.VMEM((2, 2 * HD, tq), F32),
                        pltpu.VMEM((2, t, 2 * HD), BF16),
                        pltpu.VMEM((2, t, 2 * HD), BF16)],
        compiler_params=_cparams(("arbitrary", "arbitrary")),
        name="attn_b_prompt",
    )(_alibi_slopes(n_heads), _slope_features(n_heads), lamp, subln, qb, kb16, vb16, gb, mix)


def _attn_a_sample_kernel(slopes_ref, qa_ref, kc_ref, vc_ref, kic_ref, kn_ref, vn_ref, kwn_ref, qidx_ref,
                          ga_ref, o_ref, k16, v16, sc_scr, s_scr, p_scr, *, topk, n_heads, cw):
    t = qa_ref.shape[2]
    past = kc_ref.shape[0] // KV_A
    s_pad = sc_scr.shape[1]
    rep = n_heads // KV_A
    kcol = lax.broadcasted_iota(jnp.int32, (t, s_pad), 1)
    qrow = lax.broadcasted_iota(jnp.int32, (t, s_pad), 0)

    qi_all = qidx_ref[:, 0].reshape(H_IDX * t, D_IDX)
    w_q = kwn_ref[0, :, D_IDX:D_IDX + H_IDX] * (H_IDX ** -0.5)

    def scores(ki):
        r = jnp.maximum(_dot_nt(qi_all, ki), 0.0)
        acc = jnp.zeros((t, ki.shape[0]), F32)
        for h in range(H_IDX):
            acc = acc + r[h * t:(h + 1) * t, :] * w_q[:, h:h + 1]
        return acc

    for c0 in range(0, past, cw):
        sc_scr[:, c0:c0 + cw] = scores(kic_ref[0, c0:c0 + cw, :].astype(BF16))
    tail = s_pad - past
    ki_new = jnp.concatenate(
        [kwn_ref[0, :, :D_IDX], jnp.zeros((tail - t, D_IDX), F32)], axis=0).astype(BF16)
    tail_valid = lax.broadcasted_iota(jnp.int32, (t, tail), 1) < t
    sc_scr[:, past:] = jnp.where(tail_valid, scores(ki_new), -jnp.inf)

    thr = _kth_largest(lambda i: sc_scr[...], 1, topk, 1, (t, 1))
    mbias = jnp.where(sc_scr[...] >= thr, 0.0, NEG)
    dist = jnp.abs((past + qrow - kcol).astype(F32))

    for g in range(KV_A):
        k16[:past] = kc_ref[pl.ds(g, past, stride=KV_A), :].astype(BF16)
        v16[:past] = vc_ref[pl.ds(g, past, stride=KV_A), :].astype(BF16)
        zpad = jnp.zeros((tail - t, HD), F32)
        k16[past:] = jnp.concatenate([kn_ref[pl.ds(g, t, stride=KV_A), :], zpad], axis=0).astype(BF16)
        v16[past:] = jnp.concatenate([vn_ref[pl.ds(g, t, stride=KV_A), :], zpad], axis=0).astype(BF16)
        q_g = qa_ref[g * rep:(g + 1) * rep, 0].reshape(rep * t, HD)
        s_scr[...] = _dot_nt(q_g, k16[...])
        inv_l = []
        for hh in range(rep):
            rows = slice(hh * t, (hh + 1) * t)
            s = s_scr[rows, :] - slopes_ref[g * rep + hh] * dist + mbias
            p = jnp.exp2(s - jnp.max(s, axis=1, keepdims=True))
            inv_l.append(1.0 / jnp.sum(p, axis=1, keepdims=True))
            p_scr[rows, :] = p.astype(BF16)
        o_g = jnp.dot(p_scr[...], v16[...], preferred_element_type=F32)
        for hh in range(rep):
            hc = slice((g * rep + hh) * HD, (g * rep + hh + 1) * HD)
            o = o_g[hh * t:(hh + 1) * t, :] * inv_l[hh]
            o_ref[0, :, hc] = (o * _silu(ga_ref[0, :, hc].astype(F32))).astype(o_ref.dtype)


def _attn_a_sample(qa, kc, vc, kic, kn, vn, kwn, qidx, ga, topk, d_mix, cw=512):
    n_heads, b, t, _ = qa.shape
    past = kic.shape[1]
    s_pad = past + 128
    assert t <= 128 and past % cw == 0
    d_a = n_heads * HD
    return pl.pallas_call(
        functools.partial(_attn_a_sample_kernel, topk=topk, n_heads=n_heads, cw=cw),
        out_shape=jax.ShapeDtypeStruct((b, t, d_mix), BF16),
        grid=(b,),
        in_specs=[pl.BlockSpec(memory_space=pltpu.SMEM),
                  pl.BlockSpec((n_heads, 1, t, HD), lambda i: (0, i, 0, 0)),
                  pl.BlockSpec((past * KV_A, HD), lambda i: (i, 0)),
                  pl.BlockSpec((past * KV_A, HD), lambda i: (i, 0)),
                  pl.BlockSpec((1, past, D_IDX), lambda i: (i, 0, 0)),
                  pl.BlockSpec((t * KV_A, HD), lambda i: (i, 0)),
                  pl.BlockSpec((t * KV_A, HD), lambda i: (i, 0)),
                  pl.BlockSpec((1, t, 128), lambda i: (i, 0, 0)),
                  pl.BlockSpec((H_IDX, 1, t, D_IDX), lambda i: (0, i, 0, 0)),
                  pl.BlockSpec((1, t, d_a), lambda i: (i, 0, 0))],
        out_specs=pl.BlockSpec((1, t, d_a), lambda i: (i, 0, 0)),
        scratch_shapes=[pltpu.VMEM((s_pad, HD), BF16),
                        pltpu.VMEM((s_pad, HD), BF16),
                        pltpu.VMEM((t, s_pad), F32),
                        pltpu.VMEM((n_heads // KV_A * t, s_pad), F32),
                        pltpu.VMEM((n_heads // KV_A * t, s_pad), BF16)],
        compiler_params=_cparams(("arbitrary",)),
        name="attn_a_sample",
    )(_alibi_slopes(n_heads), qa, kc, vc, kic, kn, vn, kwn, qidx, ga)


def _attn_b_sample_kernel(slopes_ref, lamp_ref, subln_ref, q_ref, kc_ref, vc_ref, kn_ref, vn_ref, g_ref, mix_ref,
                          o_ref, m_scr, l_scr, acc_scr, *, n_heads, past, v_half_major):
    c = pl.program_id(1)
    t = q_ref.shape[2]
    slots = 2 * n_heads
    pc = kc_ref.shape[0] // slots
    lam = _lam(lamp_ref)

    @pl.when(c == 0)
    def _():
        m_scr[...] = jnp.full(m_scr.shape, NEG, F32)
        l_scr[...] = jnp.zeros(l_scr.shape, F32)
        acc_scr[...] = jnp.zeros(acc_scr.shape, F32)

    def q_blockdiag(h):
        z = jnp.zeros((t, HD), BF16)
        return jnp.concatenate([jnp.concatenate([q_ref[2 * h, 0], z], axis=1),
                                jnp.concatenate([z, q_ref[2 * h + 1, 0]], axis=1)], axis=0)

    def update(h, s, v16):
        m_old = m_scr[h]
        m_new = jnp.maximum(m_old, jnp.max(s, axis=1, keepdims=True))
        alpha = jnp.exp2(m_old - m_new)
        p = jnp.exp2(s - m_new)
        l_scr[h] = alpha * l_scr[h] + jnp.sum(p, axis=1, keepdims=True)
        acc_scr[h] = alpha * acc_scr[h] + jnp.dot(p.astype(BF16), v16, preferred_element_type=F32)
        m_scr[h] = m_new

    qrow = lax.broadcasted_iota(jnp.int32, (t, pc), 0)
    kcol = lax.broadcasted_iota(jnp.int32, (t, pc), 1)
    dist = (past + qrow - kcol).astype(F32) - (c * pc).astype(F32)
    for h in range(n_heads):
        k01 = jnp.concatenate([kc_ref[pl.ds(2 * h, pc, stride=slots), :],
                               kc_ref[pl.ds(2 * h + 1, pc, stride=slots), :]], axis=1).astype(BF16)
        v_h = jnp.concatenate([vc_ref[pl.ds(h, pc, stride=slots), :],
                               vc_ref[pl.ds(n_heads + h, pc, stride=slots), :]], axis=1).astype(BF16)
        bias = -slopes_ref[h] * dist
        update(h, _dot_nt(q_blockdiag(h), k01) + jnp.concatenate([bias, bias], axis=0), v_h)

    @pl.when(c == pl.num_programs(1) - 1)
    def _():
        qr = lax.broadcasted_iota(jnp.int32, (t, LANES), 0)
        kc_ = lax.broadcasted_iota(jnp.int32, (t, LANES), 1)
        dist_new = jnp.abs((qr - kc_).astype(F32))
        zpad = jnp.zeros((LANES - t, 2 * HD), F32)
        for h in range(n_heads):
            cols = slice(h * 2 * HD, (h + 1) * 2 * HD)
            kn = jnp.concatenate([kn_ref[0, :, cols], zpad], axis=0).astype(BF16)
            if v_half_major:
                v_new = jnp.concatenate([vn_ref[0, :, (half * n_heads + h) * HD:(half * n_heads + h + 1) * HD]
                                         for half in range(2)], axis=1)
            else:
                v_new = vn_ref[0, :, cols]
            vn = jnp.concatenate([v_new, zpad], axis=0).astype(BF16)
            bias = jnp.where(kc_ < t, -slopes_ref[h] * dist_new, NEG)
            update(h, _dot_nt(q_blockdiag(h), kn) + jnp.concatenate([bias, bias], axis=0), vn)
            o_all = acc_scr[h] / l_scr[h]
            o = o_all[:t] - lam * o_all[t:]
            o_ref[0, :, cols] = _subln_gate(o, subln_ref[...], g_ref[0, :, cols]).astype(o_ref.dtype)


def _attn_b_sample(qb, kc, vc, kn, vn, gb, lamp, subln, mix, past, v_half_major, pc=1024):
    _, b, t, _ = qb.shape
    d_b = gb.shape[2]
    n_heads = d_b // (2 * HD)
    slots = 2 * n_heads
    pc = min(pc, past)
    assert t <= LANES and past % pc == 0 and (mix.shape[2] - d_b) % d_b == 0
    nc = past // pc
    col0 = (mix.shape[2] - d_b) // d_b
    return pl.pallas_call(
        functools.partial(_attn_b_sample_kernel, n_heads=n_heads, past=past, v_half_major=v_half_major),
        out_shape=jax.ShapeDtypeStruct(mix.shape, mix.dtype),
        grid=(b, nc),
        in_specs=[pl.BlockSpec(memory_space=pltpu.SMEM),
                  pl.BlockSpec((4, HD), lambda i, c: (0, 0)),
                  pl.BlockSpec((1, 2 * HD), lambda i, c: (0, 0)),
                  pl.BlockSpec((slots, 1, t, HD), lambda i, c: (0, i, 0, 0)),
                  pl.BlockSpec((pc * slots, LANES), lambda i, c: (i * nc + c, 0)),
                  pl.BlockSpec((pc * slots, LANES), lambda i, c: (i * nc + c, 0)),
                  pl.BlockSpec((1, t, d_b), lambda i, c: (i, 0, 0)),
                  pl.BlockSpec((1, t, d_b), lambda i, c: (i, 0, 0)),
                  pl.BlockSpec((1, t, d_b), lambda i, c: (i, 0, 0)),
                  pl.BlockSpec(memory_space=pl.ANY)],
        out_specs=pl.BlockSpec((1, t, d_b), lambda i, c: (i, 0, col0)),
        input_output_aliases={9: 0},
        scratch_shapes=[pltpu.VMEM((n_heads, 2 * t, 1), F32),
                        pltpu.VMEM((n_heads, 2 * t, 1), F32),
                        pltpu.VMEM((n_heads, 2 * t, 2 * HD), F32)],
        compiler_params=_cparams(("arbitrary", "arbitrary")),
        name="attn_b_sample",
    )(_alibi_slopes(n_heads), lamp, subln, qb, kc, vc, kn, vn, gb, mix)


def _out_kernel(mix_ref, w_ref, x_ref, gate_ref, lng_ref, lnb_ref, o_ref, *, alpha, nj):
    j = pl.program_id(1)
    tn = x_ref.shape[1]
    r = alpha * x_ref[...] + gate_ref[...] * jnp.dot(mix_ref[...], w_ref[...], preferred_element_type=F32)
    o_ref[:, pl.ds(pl.multiple_of(j * tn, tn), tn)] = r

    @pl.when(j == nj - 1)
    def _():
        d = nj * tn
        tot = jnp.zeros((o_ref.shape[0], 1), F32)
        for c in range(nj):
            tot = tot + jnp.sum(o_ref[:, c * tn:(c + 1) * tn], axis=1, keepdims=True)
        mu = tot / d
        sq = jnp.zeros_like(tot)
        for c in range(nj):
            dv = o_ref[:, c * tn:(c + 1) * tn] - mu
            sq = sq + jnp.sum(dv * dv, axis=1, keepdims=True)
        inv = lax.rsqrt(sq / d + LN_EPS)
        for c in range(nj):
            cols = slice(c * tn, (c + 1) * tn)
            o_ref[:, cols] = (o_ref[:, cols] - mu) * inv * lng_ref[:, cols] + lnb_ref[:, cols]


def _out_proj(mix, w_out, x2d, gate, ln_g, ln_b, alpha, tm, tn, rows_per_gate):
    m, dm = mix.shape
    d = w_out.shape[1]
    assert m % tm == 0 and d % tn == 0
    if rows_per_gate == 1:
        gate_spec = pl.BlockSpec((tm, tn), lambda i, j: (i, j))
    else:
        assert rows_per_gate % tm == 0
        gate = gate.reshape(-1, 1, d)
        gate_spec = pl.BlockSpec((None, 1, tn), lambda i, j: (i * tm // rows_per_gate, 0, j))
    return pl.pallas_call(
        functools.partial(_out_kernel, alpha=alpha, nj=d // tn),
        out_shape=jax.ShapeDtypeStruct((m, d), F32),
        grid=(m // tm, d // tn),
        in_specs=[pl.BlockSpec((tm, dm), lambda i, j: (i, 0), pipeline_mode=pl.Buffered(1)),
                  pl.BlockSpec((dm, tn), lambda i, j: (0, j)),
                  pl.BlockSpec((tm, tn), lambda i, j: (i, j)),
                  gate_spec,
                  pl.BlockSpec((1, d), lambda i, j: (0, 0)),
                  pl.BlockSpec((1, d), lambda i, j: (0, 0))],
        out_specs=pl.BlockSpec((tm, d), lambda i, j: (i, 0)),
        compiler_params=_cparams(("arbitrary", "arbitrary")),
        name="out_proj",
    )(mix, w_out, x2d, gate, ln_g.reshape(1, d), ln_b.reshape(1, d))


def _w_in_layout(d_a, d_b, v_half_major):
    kv = KV_A * HD
    sizes = dict(q_a=d_a, k_a=kv, v_a=kv, g_a=d_a, q_idx=H_IDX * D_IDX, kw=D_IDX + H_IDX, q_b=d_b, k_b=d_b,
                 v_b=d_b, g_b=d_b)
    src, off = {}, 0
    for name in ("q_a", "k_a", "v_a", "g_a", "q_idx", "kw", "q_b", "k_b", "v_b", "g_b"):
        src[name] = off
        off += sizes[name]
    moves, dst = [], 0
    for name in ("k_a", "v_a", "kw"):
        moves.append((dst, src[name], sizes[name]))
        dst += sizes[name]
    zero = (dst, 1024)
    dst = 1024
    for name in ("q_a", "g_a", "q_idx", "q_b", "k_b"):
        moves.append((dst, src[name], sizes[name]))
        dst += sizes[name]
    n_b = d_b // (2 * HD)
    if v_half_major:
        for half in range(2):
            for h in range(n_b):
                moves.append((dst + (half * n_b + h) * HD, src["v_b"] + (h * 2 + half) * HD, HD))
    else:
        moves.append((dst, src["v_b"], d_b))
    dst += d_b
    moves.append((dst, src["g_b"], d_b))
    return moves, zero, dst + d_b


def _repack_kernel(src_ref, *refs):
    o_ref = refs[-1]
    for piece, w_ref in enumerate(refs[:-1]):
        o_ref[:, piece * LANES:(piece + 1) * LANES] = jnp.transpose(w_ref[...]).astype(o_ref.dtype)


ROW_ALIGN = 16


def _pad_w_in(w_in_t, d_a, d_b, v_half_major, per_step=4):
    moves, _, n_out = _w_in_layout(d_a, d_b, v_half_major)
    n_in, k = w_in_t.shape
    src_rows = np.zeros(n_out // LANES, np.int32)
    for dst, src, width in moves:
        for c in range(0, width, LANES):
            assert dst % LANES == 0 and (src + c) % ROW_ALIGN == 0 and src + c + LANES <= n_in
            src_rows[(dst + c) // LANES] = (src + c) // ROW_ALIGN
    return pl.pallas_call(
        _repack_kernel,
        out_shape=jax.ShapeDtypeStruct((k, n_out), BF16),
        grid_spec=pltpu.PrefetchScalarGridSpec(
            num_scalar_prefetch=1,
            grid=(n_out // (per_step * LANES),),
            in_specs=[pl.BlockSpec((pl.Element(LANES), pl.Element(k)),
                                   lambda i, src, p=p: (src[i * per_step + p] * ROW_ALIGN, 0))
                      for p in range(per_step)],
            out_specs=pl.BlockSpec((k, per_step * LANES), lambda i, src: (0, i))),
        compiler_params=_cparams(("arbitrary",)),
        name="repack_w_in",
    )(jnp.asarray(src_rows), *([w_in_t] * per_step))


def _project(x, shift, scale, w_pad, d_a, d_b, tm, want_b16, native):
    kv = KV_A * HD
    h2d, k_a, v_a, kw = _modulate_project(
        x, shift, scale, w_pad, [(0, kv, True), (kv, 2 * kv, True), (2 * kv, 2 * kv + 128, False)],
        tm=min(x.shape[1], 512))
    c = 1024
    wide = tm >= 1024 and d_a % 1024 == 0 and d_b % 1024 == 0
    tn = 1024 if wide else 512
    q_scale = HD ** -0.5 * LOG2E
    (q_a,) = _proj(h2d, w_pad, c, d_a, tn, tm, [("heads", 0, tn, HD, q_scale, BF16)])
    c += d_a
    (g_a,) = _proj(h2d, w_pad, c, d_a, tn, tm, [("flat", 0, tn, 0, 1.0, BF16)])
    c += d_a
    (q_idx,) = _proj(h2d, w_pad, c, H_IDX * D_IDX, tn, tm, [("heads", 0, tn, D_IDX, 1.0, BF16)])
    c += H_IDX * D_IDX
    (q_b,) = _proj(h2d, w_pad, c, d_b, tn, tm, [("heads", 0, tn, HD, q_scale, BF16)])
    c += d_b
    tn_b = tn
    kb_outs = [("flat", 0, tn, 0, 1.0, F32)]
    vb_outs = [("flat", 0, tn, 0, 1.0, F32)]
    if want_b16 and native:
        tn_b = SUBLANES * LANES
        kb_outs = [("native", 0, tn_b, 0, 1.0, F32), ("heads", 0, tn_b, HD, 1.0, BF16)]
        vb_outs = [("native", 0, tn_b, 0, 1.0, F32), ("lanes", 0, tn_b, 0, 1.0, BF16)]
    elif want_b16:
        kb_outs.append(("heads", 0, tn, HD, 1.0, BF16))
        vb_outs.append(("heads", 0, tn, 2 * HD, 1.0, BF16))
    k_b = _proj(h2d, w_pad, c, d_b, tn_b, tm, kb_outs)
    c += d_b
    v_b = _proj(h2d, w_pad, c, d_b, tn_b, tm, vb_outs)
    c += d_b
    (g_b,) = _proj(h2d, w_pad, c, d_b, tn, tm, [("flat", 0, tn, 0, 1.0, BF16)])
    return dict(k_a=k_a, v_a=v_a, kw=kw, q_a=q_a, g_a=g_a, q_idx=q_idx, q_b=q_b, k_b=k_b, v_b=v_b, g_b=g_b)


def _layer(x, mod, past, w_pad, w_out, lamp, subln, ln_g, ln_b, alpha, d_a, d_b, native):
    b, t, d = x.shape
    h_b = d_b // (2 * HD)
    m = b * t
    shift, scale, gate = (mod[:, i * d:(i + 1) * d] for i in range(3))
    prompt = past is None

    p = _project(x, shift.reshape(b, 1, d), scale.reshape(b, 1, d), w_pad, d_a, d_b, tm=min(m, 1024),
                 want_b16=prompt, native=native)

    r3 = lambda a: a.reshape(b, t, a.shape[-1])
    r4 = lambda a: a.reshape(a.shape[0], b, t, a.shape[-1])
    k_a, v_a = p["k_a"], p["v_a"]
    kw, g_a, g_b = r3(p["kw"]), r3(p["g_a"]), r3(p["g_b"])
    q_a, q_idx, q_b = r4(p["q_a"]), r4(p["q_idx"]), r4(p["q_b"])

    if prompt:
        topk = min(TOPK_MAX, t // 4)
        mix = _attn_a_prompt(q_a, k_a, v_a, kw, q_idx, g_a, topk, d_a + d_b)
        mix = _attn_b_prompt(q_b, r4(p["k_b"][1]), r4(p["v_b"][1]), g_b, lamp, subln, mix)
        tm_out, rows_per_gate, gate_rows = min(m, 1024), t, gate
        new_k_b = p["k_b"][0].reshape(1, b, t, h_b, 2, HD)
        if native:
            new_v_b = p["v_b"][0].reshape(b, t, 2, h_b, HD).transpose(0, 1, 3, 2, 4).reshape(1, b, t, h_b, 2 * HD)
        else:
            new_v_b = p["v_b"][0].reshape(1, b, t, h_b, 2 * HD)
    else:
        kc_a, vc_a, kic, kc_b, vc_b = past
        plen = kic.shape[1]
        topk = min(TOPK_MAX, (plen + t) // 4)
        k_b, v_b = r3(p["k_b"][0]), r3(p["v_b"][0])
        rows = lambda a: a.reshape(-1, LANES)
        vc_b_rows = rows(vc_b.reshape(b, plen, h_b, 2, HD).transpose(0, 1, 3, 2, 4))
        mix = _attn_a_sample(q_a, rows(kc_a), rows(vc_a), kic, k_a, v_a, kw, q_idx, g_a, topk, d_a + d_b)
        mix = _attn_b_sample(q_b, rows(kc_b), vc_b_rows, k_b, v_b, g_b, lamp, subln, mix, plen, native)
        tm_out, rows_per_gate = m, 1
        gate_rows = jnp.broadcast_to(gate[:, None, :], (b, t, d)).reshape(m, d)
        new_k_b = k_b.reshape(1, b, t, h_b, 2, HD)
        if native:
            new_v_b = v_b.reshape(b, t, 2, h_b, HD).transpose(0, 1, 3, 2, 4).reshape(1, b, t, h_b, 2 * HD)
        else:
            new_v_b = v_b.reshape(1, b, t, h_b, 2 * HD)

    y = _out_proj(mix.reshape(m, d_a + d_b), w_out, x.reshape(m, d), gate_rows, ln_g, ln_b, alpha,
                  tm=tm_out, tn=512, rows_per_gate=rows_per_gate)
    rows_out = (k_a.reshape(1, b, t, KV_A, HD), v_a.reshape(1, b, t, KV_A, HD), kw[None, :, :, :D_IDX],
                new_k_b, new_v_b)
    return y.reshape(b, t, d), rows_out


def kernel(x_prompt, x_sample, cache_a_k, cache_a_v, cache_a_kidx, cache_b_k, cache_b_v, c_prompt, c_sample,
           w_ada, b_ada, w_in, w_out, lam_q1, lam_k1, lam_q2, lam_k2, subln_g, ln_g, ln_b):
    depth, d, _ = w_ada.shape
    assert depth == 1, "single-layer step"
    d_b = cache_b_v.shape[3] * cache_b_v.shape[4]
    d_a = w_out.shape[1] - d_b
    alpha = (2.0 * depth) ** 0.25
    bp = x_prompt.shape[0]

    mod = _ada(jnp.concatenate([c_prompt, c_sample], axis=0), w_ada[0], b_ada[0])
    native = d_b == 2 * SUBLANES * LANES
    w_pad = _pad_w_in(jnp.transpose(w_in[0]), d_a, d_b, native)
    lamp = jnp.concatenate([lam_q1, lam_k1, lam_q2, lam_k2], axis=0)
    common = (w_pad, w_out[0].astype(BF16), lamp, subln_g, ln_g[0], ln_b[0], alpha, d_a, d_b, native)

    y_p, rows_p = _layer(x_prompt, mod[:bp], None, *common)
    past = (cache_a_k[0], cache_a_v[0], cache_a_kidx[0], cache_b_k[0], cache_b_v[0])
    y_s, rows_s = _layer(x_sample, mod[bp:], past, *common)
    return (y_p, y_s) + rows_p + rows_s
```

```python
import functools
import math

import jax
import jax.numpy as jnp
import numpy as np
from jax import lax
from jax.experimental import pallas as pl
from jax.experimental.pallas import tpu as pltpu

F32 = jnp.float32
BF16 = jnp.bfloat16

HD = 128
CHUNK = 64
CHUNK_SHIFT = 6
KV_A = 2
H_IDX = 16
D_IDX = 64
TOPK_MAX = 256
LN_EPS = 1e-5
LAM_INIT = 0.8 - 0.6 * math.exp(-0.3 * 0)

LOG2E = math.log2(math.e)
NEG = -1e30
N_BISECT = 16

VMEM_LIMIT = 56 * 1024 * 1024
MXU_DIM = 256
LANES = 128
SUBLANES = 8


def _cparams(sem):
    return pltpu.CompilerParams(dimension_semantics=sem, vmem_limit_bytes=VMEM_LIMIT)


def _silu(x):
    return x * jax.nn.sigmoid(x)


def _dot_nt(a, b):
    return lax.dot_general(a, b, (((1,), (1,)), ((), ())), preferred_element_type=F32)


def _alibi_slopes(n):
    return jnp.asarray(2.0 ** (-8.0 * np.arange(1, n + 1) / n), dtype=F32) * LOG2E


def _ada_kernel(c_ref, w_ref, b_ref, o_ref):
    s = _silu(c_ref[...]).astype(BF16)
    o_ref[...] = jnp.dot(s, w_ref[...].astype(BF16), preferred_element_type=F32) + b_ref[...]


def _ada(c, w_ada, b_ada, tn=512):
    n, d = c.shape
    e = w_ada.shape[1]
    return pl.pallas_call(
        _ada_kernel,
        out_shape=jax.ShapeDtypeStruct((n, e), F32),
        grid=(e // tn,),
        in_specs=[pl.BlockSpec((n, d), lambda j: (0, 0)),
                  pl.BlockSpec((d, tn), lambda j: (0, j)),
                  pl.BlockSpec((1, tn), lambda j: (0, j))],
        out_specs=pl.BlockSpec((n, tn), lambda j: (0, j)),
        compiler_params=_cparams(("arbitrary",)),
        name="ada",
    )(c, w_ada, b_ada.reshape(1, e))


def _modproj_kernel(x_ref, shift_ref, scale_ref, w_ref, h_ref, *o_refs, splits):
    h = (x_ref[0] * (1.0 + scale_ref[0]) + shift_ref[0]).astype(h_ref.dtype)
    h_ref[...] = h
    acc = jnp.dot(h, w_ref[...], preferred_element_type=F32)
    tm = acc.shape[0]
    for o_ref, (lo, hi, interleave) in zip(o_refs, splits):
        if interleave:
            for g in range((hi - lo) // LANES):
                o_ref[pl.ds(g, tm, stride=(hi - lo) // LANES), :] = acc[:, lo + g * LANES:lo + (g + 1) * LANES]
        else:
            o_ref[...] = acc[:, lo:hi]


def _modulate_project(x, shift, scale, w_pad, splits, tm):
    b, t, d = x.shape
    n = splits[-1][1]
    nt = t // tm

    def out_shape_block(split):
        lo, hi, interleave = split
        groups = (hi - lo) // LANES if interleave else 1
        return (b * t * groups, (hi - lo) // groups), (tm * groups, (hi - lo) // groups)

    return pl.pallas_call(
        functools.partial(_modproj_kernel, splits=tuple(splits)),
        out_shape=[jax.ShapeDtypeStruct((b * t, d), BF16)]
        + [jax.ShapeDtypeStruct(shape, F32) for shape, _ in map(out_shape_block, splits)],
        grid=(b, nt),
        in_specs=[pl.BlockSpec((1, tm, d), lambda i, j: (i, j, 0)),
                  pl.BlockSpec((1, 1, d), lambda i, j: (i, 0, 0)),
                  pl.BlockSpec((1, 1, d), lambda i, j: (i, 0, 0)),
                  pl.BlockSpec((d, n), lambda i, j: (0, 0))],
        out_specs=[pl.BlockSpec((tm, d), lambda i, j: (i * nt + j, 0))]
        + [pl.BlockSpec(block, lambda i, j: (i * nt + j, 0)) for _, block in map(out_shape_block, splits)],
        compiler_params=_cparams(("arbitrary", "arbitrary")),
        name="modulate_proj",
    )(x, shift, scale, w_pad)


def _proj_kernel(h_ref, w_ref, *o_refs, outs, stream_weights):
    if stream_weights:
        acc = jnp.transpose(_dot_nt(jnp.transpose(w_ref[...]), h_ref[...]))
    else:
        acc = jnp.dot(h_ref[...], w_ref[...], preferred_element_type=F32)
    for o_ref, (kind, lo, hi, width, scale) in zip(o_refs, outs):
        if kind == "flat":
            o_ref[...] = acc[:, lo:hi].astype(o_ref.dtype)
        elif kind == "heads":
            for hh in range((hi - lo) // width):
                blk = acc[:, lo + hh * width: lo + (hh + 1) * width]
                if scale != 1.0:
                    blk = blk * scale
                o_ref[hh] = blk.astype(o_ref.dtype)
        elif kind == "lanes":
            for hh in range((hi - lo) // LANES):
                o_ref[hh] = acc[:, lo + hh * LANES: lo + (hh + 1) * LANES].astype(o_ref.dtype)
        else:
            pieces = jnp.stack([acc[:, lo + s * LANES: lo + (s + 1) * LANES] for s in range(SUBLANES)], axis=0)
            o_ref[...] = jnp.swapaxes(pieces, 0, 1).astype(o_ref.dtype)


def _proj(h2d, w_pad, col_start, ncols, tn, tm, outs):
    m, k = h2d.shape
    nj = ncols // tn
    assert col_start % tn == 0 and ncols % tn == 0 and m % tm == 0
    j0 = col_start // tn
    out_shapes, out_specs, kouts = [], [], []
    for kind, lo, hi, width, scale, dtype in outs:
        if kind == "flat":
            out_shapes.append(jax.ShapeDtypeStruct((m, nj * (hi - lo)), dtype))
            out_specs.append(pl.BlockSpec((tm, hi - lo), lambda i, j: (i, j)))
        elif kind == "heads":
            nh = (hi - lo) // width
            out_shapes.append(jax.ShapeDtypeStruct((nj * nh, m, width), dtype))
            out_specs.append(pl.BlockSpec((nh, tm, width), lambda i, j: (j, i, 0)))
        elif kind == "lanes":
            nh = (hi - lo) // LANES
            out_shapes.append(jax.ShapeDtypeStruct((nh, m, nj * LANES), dtype))
            out_specs.append(pl.BlockSpec((nh, tm, LANES), lambda i, j: (0, i, j)))
        else:
            assert hi - lo == SUBLANES * LANES
            out_shapes.append(jax.ShapeDtypeStruct((m, nj, SUBLANES, LANES), dtype))
            out_specs.append(pl.BlockSpec((tm, None, SUBLANES, LANES), lambda i, j: (i, j, 0, 0)))
        kouts.append((kind, lo, hi, width, scale))
    return pl.pallas_call(
        functools.partial(_proj_kernel, outs=tuple(kouts), stream_weights=m <= MXU_DIM),
        out_shape=out_shapes,
        grid=(m // tm, nj),
        in_specs=[pl.BlockSpec((tm, k), lambda i, j: (i, 0)),
                  pl.BlockSpec((k, tn), lambda i, j: (0, j0 + j))],
        out_specs=out_specs,
        compiler_params=_cparams(("arbitrary", "arbitrary")),
        name=f"proj_c{col_start}",
    )(h2d, w_pad)


def _kth_largest(load, ntiles, k, axis, shape):
    part = (SUBLANES, shape[1]) if axis == 0 else shape

    def fold(x, op):
        if axis == 0:
            return op(x.reshape(x.shape[0] // SUBLANES, SUBLANES, x.shape[1]), axis=0)
        return op(x, axis=axis, keepdims=True)

    def finish(c, op):
        return op(c, axis=0, keepdims=True) if axis == 0 else c

    def reduce_tiles(fn, init):
        return lax.fori_loop(0, ntiles, lambda i, c: fn(load(i), c), init)

    def count_ge(t):
        c = reduce_tiles(lambda x, c: c + fold(jnp.where(x >= t, 1.0, 0.0), jnp.sum), jnp.zeros(part, F32))
        return finish(c, jnp.sum)

    def max_below(t, strict):
        def fn(x, c):
            keep = (x < t) if strict else (x <= t)
            return jnp.maximum(c, fold(jnp.where(keep, x, -jnp.inf), jnp.max))
        return finish(reduce_tiles(fn, jnp.full(part, -jnp.inf, F32)), jnp.max)

    def minmax(x, c):
        mn, mx = c
        mn = jnp.minimum(mn, fold(jnp.where(x > -jnp.inf, x, jnp.inf), jnp.min))
        mx = jnp.maximum(mx, fold(x, jnp.max))
        return mn, mx

    lo, hi = reduce_tiles(minmax, (jnp.full(part, jnp.inf, F32), jnp.full(part, -jnp.inf, F32)))
    lo, hi = finish(lo, jnp.min), finish(hi, jnp.max)
    kf = float(k)

    def bisect(_, c):
        lo, hi = c
        mid = 0.5 * lo + 0.5 * hi
        ok = count_ge(mid) >= kf
        return jnp.where(ok, mid, lo), jnp.where(ok, hi, mid)

    lo, hi = lax.fori_loop(0, N_BISECT, bisect, (lo, hi))
    v = max_below(hi, strict=False)
    c = count_ge(v)

    def walk(vc):
        v, c = vc
        v = jnp.where(c < kf, max_below(v, strict=True), v)
        return v, count_ge(v)

    v, _ = lax.while_loop(lambda vc: jnp.min(vc[1]) < kf, walk, (v, c))
    return v


def _slope_features(n):
    r = _alibi_slopes(n)
    parts = []
    for _ in range(3):
        p = r.astype(BF16).astype(F32)
        parts.append(p)
        r = r - p
    feat = jnp.stack([float(CHUNK) * p for p in parts] + parts, axis=1)
    return jnp.pad(feat, ((0, 0), (0, LANES - feat.shape[1])))


def _position_features(n_rows):
    pos = lax.broadcasted_iota(jnp.int32, (n_rows, HD), 0)
    lane = lax.broadcasted_iota(jnp.int32, (n_rows, HD), 1)
    feat = jnp.where(lane < 3, pos >> CHUNK_SHIFT, jnp.where(lane < 6, pos & (CHUNK - 1), 0))
    return feat.astype(F32).astype(BF16)


def _attn_a_kernel(slopes_ref, sfeat_ref, qa_ref, k_ref, v_ref, kw_ref, kwq_ref, qidx_ref, ga_ref, o_ref,
                   k16, vt16, ki16, st_scr, mb_scr, acc_scr, thr_scr, sq_scr, *, topk, n_heads):
    qi = pl.program_id(1)
    tq = qa_ref.shape[2]
    tk = tq
    rep = n_heads // KV_A
    nk = qi + 1

    @pl.when(qi == 0)
    def _():
        for g in range(KV_A):
            t = k16.shape[1]
            k16[g, :, :HD] = k_ref[pl.ds(g, t, stride=KV_A), :].astype(BF16)
            k16[g, :, HD:] = _position_features(t)
            for j in range(vt16.shape[1]):
                vt16[g, j] = jnp.transpose(v_ref[pl.ds(j * tk * KV_A + g, tk, stride=KV_A), :]).astype(BF16)
        ki16[...] = kw_ref[0, :, :D_IDX].astype(BF16)

    w_t = jnp.transpose(kwq_ref[0])
    w_sc = w_t[D_IDX:D_IDX + H_IDX, :] * (H_IDX ** -0.5)

    def score_tile(kj, c):
        ki = ki16[pl.ds(pl.multiple_of(kj * tk, tk), tk), :]
        acc = jnp.zeros((tk, tq), F32)
        for h in range(H_IDX):
            acc = acc + jnp.maximum(_dot_nt(ki, qidx_ref[h, 0]), 0.0) * w_sc[h:h + 1, :]
        st_scr[kj] = acc
        return c

    lax.fori_loop(0, nk, score_tile, 0)
    krow = lax.broadcasted_iota(jnp.int32, (tk, tq), 0)
    qcol = lax.broadcasted_iota(jnp.int32, (tk, tq), 1)
    st_scr[qi] = jnp.where((krow >> CHUNK_SHIFT) <= (qcol >> CHUNK_SHIFT), st_scr[qi], -jnp.inf)

    thr_scr[...] = jnp.full((1, tq), -3e38, F32)

    @pl.when(qi * tq + CHUNK >= topk)
    def _():
        thr_scr[...] = _kth_largest(lambda i: st_scr[i], nk, topk, 0, (1, tq))

    thr = thr_scr[...]

    def mask_tile(kj, c):
        mb_scr[kj] = jnp.where(st_scr[kj] >= thr, 0.0, NEG)
        return c

    lax.fori_loop(0, nk, mask_tile, 0)

    ahead = jnp.maximum(krow - qcol, 0).astype(F32)

    for g in range(KV_A):
        q_g = jnp.concatenate(
            [qa_ref[g * rep:(g + 1) * rep, 0].reshape(rep * tq, HD),
             jnp.concatenate([jnp.broadcast_to(sfeat_ref[g * rep + hh:g * rep + hh + 1, :], (tq, LANES))
                              for hh in range(rep)], axis=0).astype(BF16)], axis=1)
        acc_scr[...] = jnp.zeros(acc_scr.shape, F32)

        def logits(kj, dst):
            rows = pl.ds(pl.multiple_of(kj * tk, tk), tk)
            sq_scr[dst] = _dot_nt(k16[g, rows, :], q_g)

        def update(kj, src, carry, diagonal=False):
            m_old, l_old = carry
            mb = mb_scr[kj]
            s = jnp.concatenate(
                [sq_scr[src, :, hh * tq:(hh + 1) * tq]
                 + (mb - 2.0 * slopes_ref[g * rep + hh] * ahead if diagonal else mb) for hh in range(rep)],
                axis=1)
            m_new = jnp.maximum(m_old, jnp.max(s, axis=0, keepdims=True))
            alpha = jnp.exp2(m_old - m_new)
            p = jnp.exp2(s - m_new)
            l_new = alpha * l_old + jnp.sum(p, axis=0, keepdims=True)
            acc_scr[...] = alpha * acc_scr[...] + jnp.dot(vt16[g, kj], p.astype(BF16),
                                                          preferred_element_type=F32)
            return m_new, l_new

        def pair(p, carry):
            logits(2 * p + 1, 1)
            carry = update(2 * p, 0, carry)
            logits(2 * p + 2, 0)
            return update(2 * p + 1, 1, carry)

        def tail_odd(carry):
            logits(qi, 1)
            return update(qi, 1, update(qi - 1, 0, carry), diagonal=True)

        def tail_even(carry):
            return update(qi, 0, carry, diagonal=True)

        logits(0, 0)
        carry = lax.fori_loop(0, qi // 2, pair,
                              (jnp.full((1, rep * tq), NEG, F32), jnp.zeros((1, rep * tq), F32)))
        _, l = lax.cond(qi % 2 == 1, tail_odd, tail_even, carry)
        o_t = acc_scr[...] / l
        for hh in range(rep):
            cols = slice((g * rep + hh) * HD, (g * rep + hh + 1) * HD)
            o = jnp.transpose(o_t[:, hh * tq:(hh + 1) * tq])
            o_ref[0, :, cols] = (o * _silu(ga_ref[0, :, cols].astype(F32))).astype(o_ref.dtype)


def _attn_a_prompt(qa, k, v, kw, qidx, ga, topk, d_mix):
    n_heads, b, t, _ = qa.shape
    tq = min(256, topk)
    assert t % tq == 0 and tq % CHUNK == 0 and tq <= topk <= tq + CHUNK
    nkt = t // tq
    d_a = n_heads * HD
    return pl.pallas_call(
        functools.partial(_attn_a_kernel, topk=topk, n_heads=n_heads),
        out_shape=jax.ShapeDtypeStruct((b, t, d_mix), BF16),
        grid=(b, nkt),
        in_specs=[pl.BlockSpec(memory_space=pltpu.SMEM),
                  pl.BlockSpec((n_heads, LANES), lambda i, j: (0, 0)),
                  pl.BlockSpec((n_heads, 1, tq, HD), lambda i, j: (0, i, j, 0)),
                  pl.BlockSpec((t * KV_A, HD), lambda i, j: (i, 0)),
                  pl.BlockSpec((t * KV_A, HD), lambda i, j: (i, 0)),
                  pl.BlockSpec((1, t, 128), lambda i, j: (i, 0, 0)),
                  pl.BlockSpec((1, tq, 128), lambda i, j: (i, j, 0)),
                  pl.BlockSpec((H_IDX, 1, tq, D_IDX), lambda i, j: (0, i, j, 0)),
                  pl.BlockSpec((1, tq, d_a), lambda i, j: (i, j, 0))],
        out_specs=pl.BlockSpec((1, tq, d_a), lambda i, j: (i, j, 0)),
        scratch_shapes=[pltpu.VMEM((KV_A, t, 2 * HD), BF16),
                        pltpu.VMEM((KV_A, nkt, HD, tq), BF16),
                        pltpu.VMEM((t, D_IDX), BF16),
                        pltpu.VMEM((nkt, tq, tq), F32),
                        pltpu.VMEM((nkt, tq, tq), F32),
                        pltpu.VMEM((HD, n_heads // KV_A * tq), F32),
                        pltpu.VMEM((1, tq), F32),
                        pltpu.VMEM((2, tq, n_heads // KV_A * tq), F32)],
        compiler_params=_cparams(("arbitrary", "arbitrary")),
        name="attn_a_prompt",
    )(_alibi_slopes(n_heads), _slope_features(n_heads), qa, k, v, kw, kw, qidx, ga)


def _lam(lamp_ref):
    s1 = jnp.sum(lamp_ref[0:1, :] * lamp_ref[1:2, :], axis=1, keepdims=True)
    s2 = jnp.sum(lamp_ref[2:3, :] * lamp_ref[3:4, :], axis=1, keepdims=True)
    return jnp.exp(s1) - jnp.exp(s2) + LAM_INIT


def _subln_gate(o, subln, g):
    o = o * lax.rsqrt(jnp.mean(o * o, axis=-1, keepdims=True) + LN_EPS)
    o = o * subln * (1.0 - LAM_INIT)
    return o * _silu(g.astype(F32))


def _attn_b_kernel(slopes_ref, sfeat_ref, lamp_ref, subln_ref, q_ref, k_ref, v_ref, g_ref, mix_ref, o_ref,
                   vt_scr, s_scr, acc_scr, kaug_scr, qaug_scr, *, tq):
    h = pl.program_id(1)
    t = q_ref.shape[2]
    tk = tq
    nq = t // tq
    slope = slopes_ref[h]
    lam = _lam(lamp_ref)

    for j in range(nq):
        vt_scr[j] = jnp.transpose(v_ref[0, 0, j * tk:(j + 1) * tk, :].astype(F32)).astype(BF16)

    pos_feat = _position_features(t)
    slope_feat = jnp.broadcast_to(sfeat_ref[pl.ds(h, 1), :], (t, LANES)).astype(BF16)
    for mi in range(2):
        kaug_scr[mi, :, :HD] = k_ref[mi, 0]
        kaug_scr[mi, :, HD:] = pos_feat
        qaug_scr[mi, :, :HD] = q_ref[mi, 0]
        qaug_scr[mi, :, HD:] = slope_feat

    krow = lax.broadcasted_iota(jnp.int32, (tk, tq), 0)
    qcol = lax.broadcasted_iota(jnp.int32, (tk, tq), 1)
    ahead = jnp.maximum(krow - qcol, 0).astype(F32)
    diag_bias = jnp.where((krow >> CHUNK_SHIFT) <= (qcol >> CHUNK_SHIFT), -2.0 * slope * ahead, NEG)

    pairs = [(qi, j) for qi in range(nq) for j in range(qi + 1)]

    def issue_logits(n):
        qi, j = pairs[n]
        for mi in range(2):
            s_scr[n % 2, mi] = _dot_nt(kaug_scr[mi, j * tk:(j + 1) * tk, :], qaug_scr[mi, qi * tq:(qi + 1) * tq, :])

    issue_logits(0)
    carry = None
    for n, (qi, j) in enumerate(pairs):
        if n + 1 < len(pairs):
            issue_logits(n + 1)
        if j == 0:
            acc_scr[...] = jnp.zeros(acc_scr.shape, F32)
            init = (jnp.full((1, tq), NEG, F32), jnp.zeros((1, tq), F32))
            carry = [init, init]
        for mi in range(2):
            m_old, l_old = carry[mi]
            s = s_scr[n % 2, mi] + diag_bias if j == qi else s_scr[n % 2, mi]
            m_new = jnp.maximum(m_old, jnp.max(s, axis=0, keepdims=True))
            alpha = jnp.exp2(m_old - m_new)
            p = jnp.exp2(s - m_new)
            l_new = alpha * l_old + jnp.sum(p, axis=0, keepdims=True)
            acc_scr[mi] = alpha * acc_scr[mi] + jnp.dot(vt_scr[j], p.astype(BF16), preferred_element_type=F32)
            carry[mi] = (m_new, l_new)
        if j == qi:
            o_t = acc_scr[0] / carry[0][1] - lam * (acc_scr[1] / carry[1][1])
            o_t = o_t * lax.rsqrt(jnp.mean(o_t * o_t, axis=0, keepdims=True) + LN_EPS)
            o = jnp.transpose(o_t) * subln_ref[...] * (1.0 - LAM_INIT)
            rows = slice(qi * tq, (qi + 1) * tq)
            o_ref[0, rows, :] = (o * _silu(g_ref[0, rows, :].astype(F32))).astype(o_ref.dtype)


def _attn_b_prompt(qb, kb16, vb16, gb, lamp, subln, mix):
    _, b, t, _ = qb.shape
    n_heads = vb16.shape[0]
    col0 = (mix.shape[2] - n_heads * 2 * HD) // (2 * HD)
    tq = min(512, t)
    assert t % tq == 0 and tq % CHUNK == 0
    return pl.pallas_call(
        functools.partial(_attn_b_kernel, tq=tq),
        out_shape=jax.ShapeDtypeStruct(mix.shape, mix.dtype),
        grid=(b, n_heads),
        in_specs=[pl.BlockSpec(memory_space=pltpu.SMEM),
                  pl.BlockSpec((n_heads, LANES), lambda i, h: (0, 0)),
                  pl.BlockSpec((4, HD), lambda i, h: (0, 0)),
                  pl.BlockSpec((1, 2 * HD), lambda i, h: (0, 0)),
                  pl.BlockSpec((2, 1, t, HD), lambda i, h: (h, i, 0, 0)),
                  pl.BlockSpec((2, 1, t, HD), lambda i, h: (h, i, 0, 0)),
                  pl.BlockSpec((1, 1, t, 2 * HD), lambda i, h: (h, i, 0, 0)),
                  pl.BlockSpec((1, t, 2 * HD), lambda i, h: (i, 0, h)),
                  pl.BlockSpec(memory_space=pl.ANY)],
        out_specs=pl.BlockSpec((1, t, 2 * HD), lambda i, h: (i, 0, col0 + h)),
        input_output_aliases={8: 0},
        scratch_shapes=[pltpu.VMEM((t // tq, 2 * HD, tq), BF16),
                        pltpu.VMEM((2, 2, tq, tq), F32),
                        pltpu.VMEM((2, 2 * HD, tq), F32),
                        pltpu.VMEM((2, t, 2 * HD), BF16),
                        pltpu.VMEM((2, t, 2 * HD), BF16)],
        compiler_params=_cparams(("arbitrary", "arbitrary")),
        name="attn_b_prompt",
    )(_alibi_slopes(n_heads), _slope_features(n_heads), lamp, subln, qb, kb16, vb16, gb, mix)


def _attn_a_sample_kernel(slopes_ref, qa_ref, kc_ref, vc_ref, kic_ref, kn_ref, vn_ref, kwn_ref, qidx_ref,
                          ga_ref, o_ref, k16, v16, sc_scr, s_scr, p_scr, *, topk, n_heads, cw):
    t = qa_ref.shape[2]
    past = kc_ref.shape[0] // KV_A
    s_pad = sc_scr.shape[1]
    rep = n_heads // KV_A
    kcol = lax.broadcasted_iota(jnp.int32, (t, s_pad), 1)
    qrow = lax.broadcasted_iota(jnp.int32, (t, s_pad), 0)

    qi_all = qidx_ref[:, 0].reshape(H_IDX * t, D_IDX)
    w_q = kwn_ref[0, :, D_IDX:D_IDX + H_IDX] * (H_IDX ** -0.5)

    def scores(ki):
        r = jnp.maximum(_dot_nt(qi_all, ki), 0.0)
        acc = jnp.zeros((t, ki.shape[0]), F32)
        for h in range(H_IDX):
            acc = acc + r[h * t:(h + 1) * t, :] * w_q[:, h:h + 1]
        return acc

    for c0 in range(0, past, cw):
        sc_scr[:, c0:c0 + cw] = scores(kic_ref[0, c0:c0 + cw, :].astype(BF16))
    tail = s_pad - past
    ki_new = jnp.concatenate(
        [kwn_ref[0, :, :D_IDX], jnp.zeros((tail - t, D_IDX), F32)], axis=0).astype(BF16)
    tail_valid = lax.broadcasted_iota(jnp.int32, (t, tail), 1) < t
    sc_scr[:, past:] = jnp.where(tail_valid, scores(ki_new), -jnp.inf)

    thr = _kth_largest(lambda i: sc_scr[...], 1, topk, 1, (t, 1))
    mbias = jnp.where(sc_scr[...] >= thr, 0.0, NEG)
    dist = jnp.abs((past + qrow - kcol).astype(F32))

    for g in range(KV_A):
        k16[:past] = kc_ref[pl.ds(g, past, stride=KV_A), :].astype(BF16)
        v16[:past] = vc_ref[pl.ds(g, past, stride=KV_A), :].astype(BF16)
        zpad = jnp.zeros((tail - t, HD), F32)
        k16[past:] = jnp.concatenate([kn_ref[pl.ds(g, t, stride=KV_A), :], zpad], axis=0).astype(BF16)
        v16[past:] = jnp.concatenate([vn_ref[pl.ds(g, t, stride=KV_A), :], zpad], axis=0).astype(BF16)
        q_g = qa_ref[g * rep:(g + 1) * rep, 0].reshape(rep * t, HD)
        s_scr[...] = _dot_nt(q_g, k16[...])
        inv_l = []
        for hh in range(rep):
            rows = slice(hh * t, (hh + 1) * t)
            s = s_scr[rows, :] - slopes_ref[g * rep + hh] * dist + mbias
            p = jnp.exp2(s - jnp.max(s, axis=1, keepdims=True))
            inv_l.append(1.0 / jnp.sum(p, axis=1, keepdims=True))
            p_scr[rows, :] = p.astype(BF16)
        o_g = jnp.dot(p_scr[...], v16[...], preferred_element_type=F32)
        for hh in range(rep):
            hc = slice((g * rep + hh) * HD, (g * rep + hh + 1) * HD)
            o = o_g[hh * t:(hh + 1) * t, :] * inv_l[hh]
            o_ref[0, :, hc] = (o * _silu(ga_ref[0, :, hc].astype(F32))).astype(o_ref.dtype)


def _attn_a_sample(qa, kc, vc, kic, kn, vn, kwn, qidx, ga, topk, d_mix, cw=512):
    n_heads, b, t, _ = qa.shape
    past = kic.shape[1]
    s_pad = past + 128
    assert t <= 128 and past % cw == 0
    d_a = n_heads * HD
    return pl.pallas_call(
        functools.partial(_attn_a_sample_kernel, topk=topk, n_heads=n_heads, cw=cw),
        out_shape=jax.ShapeDtypeStruct((b, t, d_mix), BF16),
        grid=(b,),
        in_specs=[pl.BlockSpec(memory_space=pltpu.SMEM),
                  pl.BlockSpec((n_heads, 1, t, HD), lambda i: (0, i, 0, 0)),
                  pl.BlockSpec((past * KV_A, HD), lambda i: (i, 0)),
                  pl.BlockSpec((past * KV_A, HD), lambda i: (i, 0)),
                  pl.BlockSpec((1, past, D_IDX), lambda i: (i, 0, 0)),
                  pl.BlockSpec((t * KV_A, HD), lambda i: (i, 0)),
                  pl.BlockSpec((t * KV_A, HD), lambda i: (i, 0)),
                  pl.BlockSpec((1, t, 128), lambda i: (i, 0, 0)),
                  pl.BlockSpec((H_IDX, 1, t, D_IDX), lambda i: (0, i, 0, 0)),
                  pl.BlockSpec((1, t, d_a), lambda i: (i, 0, 0))],
        out_specs=pl.BlockSpec((1, t, d_a), lambda i: (i, 0, 0)),
        scratch_shapes=[pltpu.VMEM((s_pad, HD), BF16),
                        pltpu.VMEM((s_pad, HD), BF16),
                        pltpu.VMEM((t, s_pad), F32),
                        pltpu.VMEM((n_heads // KV_A * t, s_pad), F32),
                        pltpu.VMEM((n_heads // KV_A * t, s_pad), BF16)],
        compiler_params=_cparams(("arbitrary",)),
        name="attn_a_sample",
    )(_alibi_slopes(n_heads), qa, kc, vc, kic, kn, vn, kwn, qidx, ga)


def _attn_b_sample_kernel(slopes_ref, lamp_ref, subln_ref, q_ref, kc_ref, vc_ref, kn_ref, vn_ref, g_ref, mix_ref,
                          o_ref, m_scr, l_scr, acc_scr, kd_scr, vd_scr, *, n_heads, past, v_half_major):
    c = pl.program_id(1)
    t = q_ref.shape[2]
    slots = 2 * n_heads
    pc = kc_ref.shape[0] // slots
    lam = _lam(lamp_ref)

    @pl.when(c == 0)
    def _():
        m_scr[...] = jnp.full(m_scr.shape, NEG, F32)
        l_scr[...] = jnp.zeros(l_scr.shape, F32)
        acc_scr[...] = jnp.zeros(acc_scr.shape, F32)

    def q_blockdiag(h):
        z = jnp.zeros((t, HD), BF16)
        return jnp.concatenate([jnp.concatenate([q_ref[2 * h, 0], z], axis=1),
                                jnp.concatenate([z, q_ref[2 * h + 1, 0]], axis=1)], axis=0)

    def update(h, s, v16):
        m_old = m_scr[h]
        m_new = jnp.maximum(m_old, jnp.max(s, axis=1, keepdims=True))
        alpha = jnp.exp2(m_old - m_new)
        p = jnp.exp2(s - m_new)
        l_scr[h] = alpha * l_scr[h] + jnp.sum(p, axis=1, keepdims=True)
        acc_scr[h] = alpha * acc_scr[h] + jnp.dot(p.astype(BF16), v16, preferred_element_type=F32)
        m_scr[h] = m_new

    kd_scr[...] = jnp.swapaxes(kc_ref[...].reshape(pc, slots, LANES), 0, 1).astype(BF16)
    vd_scr[...] = jnp.swapaxes(vc_ref[...].reshape(pc, slots, LANES), 0, 1).astype(BF16)

    qrow = lax.broadcasted_iota(jnp.int32, (t, pc), 0)
    kcol = lax.broadcasted_iota(jnp.int32, (t, pc), 1)
    dist = (past + qrow - kcol).astype(F32) - (c * pc).astype(F32)
    for h in range(n_heads):
        k01 = jnp.concatenate([kd_scr[2 * h], kd_scr[2 * h + 1]], axis=1)
        v_h = jnp.concatenate([vd_scr[h], vd_scr[n_heads + h]], axis=1)
        bias = -slopes_ref[h] * dist
        update(h, _dot_nt(q_blockdiag(h), k01) + jnp.concatenate([bias, bias], axis=0), v_h)

    @pl.when(c == pl.num_programs(1) - 1)
    def _():
        qr = lax.broadcasted_iota(jnp.int32, (t, LANES), 0)
        kc_ = lax.broadcasted_iota(jnp.int32, (t, LANES), 1)
        dist_new = jnp.abs((qr - kc_).astype(F32))
        zpad = jnp.zeros((LANES - t, 2 * HD), F32)
        for h in range(n_heads):
            cols = slice(h * 2 * HD, (h + 1) * 2 * HD)
            kn = jnp.concatenate([kn_ref[0, :, cols], zpad], axis=0).astype(BF16)
            if v_half_major:
                v_new = jnp.concatenate([vn_ref[0, :, (half * n_heads + h) * HD:(half * n_heads + h + 1) * HD]
                                         for half in range(2)], axis=1)
            else:
                v_new = vn_ref[0, :, cols]
            vn = jnp.concatenate([v_new, zpad], axis=0).astype(BF16)
            bias = jnp.where(kc_ < t, -slopes_ref[h] * dist_new, NEG)
            update(h, _dot_nt(q_blockdiag(h), kn) + jnp.concatenate([bias, bias], axis=0), vn)
            o_all = acc_scr[h] / l_scr[h]
            o = o_all[:t] - lam * o_all[t:]
            o_ref[0, :, cols] = _subln_gate(o, subln_ref[...], g_ref[0, :, cols]).astype(o_ref.dtype)


def _attn_b_sample(qb, kc, vc, kn, vn, gb, lamp, subln, mix, past, v_half_major, pc=1024):
    _, b, t, _ = qb.shape
    d_b = gb.shape[2]
    n_heads = d_b // (2 * HD)
    slots = 2 * n_heads
    pc = min(pc, past)
    assert t <= LANES and past % pc == 0 and (mix.shape[2] - d_b) % d_b == 0
    nc = past // pc
    col0 = (mix.shape[2] - d_b) // d_b
    return pl.pallas_call(
        functools.partial(_attn_b_sample_kernel, n_heads=n_heads, past=past, v_half_major=v_half_major),
        out_shape=jax.ShapeDtypeStruct(mix.shape, mix.dtype),
        grid=(b, nc),
        in_specs=[pl.BlockSpec(memory_space=pltpu.SMEM),
                  pl.BlockSpec((4, HD), lambda i, c: (0, 0)),
                  pl.BlockSpec((1, 2 * HD), lambda i, c: (0, 0)),
                  pl.BlockSpec((slots, 1, t, HD), lambda i, c: (0, i, 0, 0)),
                  pl.BlockSpec((pc * slots, LANES), lambda i, c: (i * nc + c, 0)),
                  pl.BlockSpec((pc * slots, LANES), lambda i, c: (i * nc + c, 0)),
                  pl.BlockSpec((1, t, d_b), lambda i, c: (i, 0, 0)),
                  pl.BlockSpec((1, t, d_b), lambda i, c: (i, 0, 0)),
                  pl.BlockSpec((1, t, d_b), lambda i, c: (i, 0, 0)),
                  pl.BlockSpec(memory_space=pl.ANY)],
        out_specs=pl.BlockSpec((1, t, d_b), lambda i, c: (i, 0, col0)),
        input_output_aliases={9: 0},
        scratch_shapes=[pltpu.VMEM((n_heads, 2 * t, 1), F32),
                        pltpu.VMEM((n_heads, 2 * t, 1), F32),
                        pltpu.VMEM((n_heads, 2 * t, 2 * HD), F32),
                        pltpu.VMEM((slots, pc, LANES), BF16),
                        pltpu.VMEM((slots, pc, LANES), BF16)],
        compiler_params=_cparams(("arbitrary", "arbitrary")),
        name="attn_b_sample",
    )(_alibi_slopes(n_heads), lamp, subln, qb, kc, vc, kn, vn, gb, mix)


def _out_kernel(mix_ref, w_ref, x_ref, gate_ref, lng_ref, lnb_ref, o_ref, *, alpha, nj):
    j = pl.program_id(1)
    tn = x_ref.shape[1]
    r = alpha * x_ref[...] + gate_ref[...] * jnp.dot(mix_ref[...], w_ref[...], preferred_element_type=F32)
    o_ref[:, pl.ds(pl.multiple_of(j * tn, tn), tn)] = r

    @pl.when(j == nj - 1)
    def _():
        d = nj * tn
        tot = jnp.zeros((o_ref.shape[0], 1), F32)
        for c in range(nj):
            tot = tot + jnp.sum(o_ref[:, c * tn:(c + 1) * tn], axis=1, keepdims=True)
        mu = tot / d
        sq = jnp.zeros_like(tot)
        for c in range(nj):
            dv = o_ref[:, c * tn:(c + 1) * tn] - mu
            sq = sq + jnp.sum(dv * dv, axis=1, keepdims=True)
        inv = lax.rsqrt(sq / d + LN_EPS)
        for c in range(nj):
            cols = slice(c * tn, (c + 1) * tn)
            o_ref[:, cols] = (o_ref[:, cols] - mu) * inv * lng_ref[:, cols] + lnb_ref[:, cols]


def _out_proj(mix, w_out, x2d, gate, ln_g, ln_b, alpha, tm, tn, rows_per_gate):
    m, dm = mix.shape
    d = w_out.shape[1]
    assert m % tm == 0 and d % tn == 0
    if rows_per_gate == 1:
        gate_spec = pl.BlockSpec((tm, tn), lambda i, j: (i, j))
    else:
        assert rows_per_gate % tm == 0
        gate = gate.reshape(-1, 1, d)
        gate_spec = pl.BlockSpec((None, 1, tn), lambda i, j: (i * tm // rows_per_gate, 0, j))
    return pl.pallas_call(
        functools.partial(_out_kernel, alpha=alpha, nj=d // tn),
        out_shape=jax.ShapeDtypeStruct((m, d), F32),
        grid=(m // tm, d // tn),
        in_specs=[pl.BlockSpec((tm, dm), lambda i, j: (i, 0)),
                  pl.BlockSpec((dm, tn), lambda i, j: (0, j)),
                  pl.BlockSpec((tm, tn), lambda i, j: (i, j)),
                  gate_spec,
                  pl.BlockSpec((1, d), lambda i, j: (0, 0)),
                  pl.BlockSpec((1, d), lambda i, j: (0, 0))],
        out_specs=pl.BlockSpec((tm, d), lambda i, j: (i, 0), pipeline_mode=pl.Buffered(1)),
        compiler_params=_cparams(("arbitrary", "arbitrary")),
        name="out_proj",
    )(mix, w_out, x2d, gate, ln_g.reshape(1, d), ln_b.reshape(1, d))


def _w_in_layout(d_a, d_b, v_half_major):
    kv = KV_A * HD
    sizes = dict(q_a=d_a, k_a=kv, v_a=kv, g_a=d_a, q_idx=H_IDX * D_IDX, kw=D_IDX + H_IDX, q_b=d_b, k_b=d_b,
                 v_b=d_b, g_b=d_b)
    src, off = {}, 0
    for name in ("q_a", "k_a", "v_a", "g_a", "q_idx", "kw", "q_b", "k_b", "v_b", "g_b"):
        src[name] = off
        off += sizes[name]
    moves, dst = [], 0
    for name in ("k_a", "v_a", "kw"):
        moves.append((dst, src[name], sizes[name]))
        dst += sizes[name]
    zero = (dst, 1024)
    dst = 1024
    for name in ("q_a", "g_a", "q_idx", "q_b", "k_b"):
        moves.append((dst, src[name], sizes[name]))
        dst += sizes[name]
    n_b = d_b // (2 * HD)
    if v_half_major:
        for half in range(2):
            for h in range(n_b):
                moves.append((dst + (half * n_b + h) * HD, src["v_b"] + (h * 2 + half) * HD, HD))
    else:
        moves.append((dst, src["v_b"], d_b))
    dst += d_b
    moves.append((dst, src["g_b"], d_b))
    return moves, zero, dst + d_b


def _repack_kernel(src_ref, *refs):
    o_ref = refs[-1]
    for piece, w_ref in enumerate(refs[:-1]):
        o_ref[:, piece * LANES:(piece + 1) * LANES] = jnp.transpose(w_ref[...]).astype(o_ref.dtype)


ROW_ALIGN = 16


def _pad_w_in(w_in_t, d_a, d_b, v_half_major, per_step=4):
    moves, _, n_out = _w_in_layout(d_a, d_b, v_half_major)
    n_in, k = w_in_t.shape
    src_rows = np.zeros(n_out // LANES, np.int32)
    for dst, src, width in moves:
        for c in range(0, width, LANES):
            assert dst % LANES == 0 and (src + c) % ROW_ALIGN == 0 and src + c + LANES <= n_in
            src_rows[(dst + c) // LANES] = (src + c) // ROW_ALIGN
    return pl.pallas_call(
        _repack_kernel,
        out_shape=jax.ShapeDtypeStruct((k, n_out), BF16),
        grid_spec=pltpu.PrefetchScalarGridSpec(
            num_scalar_prefetch=1,
            grid=(n_out // (per_step * LANES),),
            in_specs=[pl.BlockSpec((pl.Element(LANES), pl.Element(k)),
                                   lambda i, src, p=p: (src[i * per_step + p] * ROW_ALIGN, 0))
                      for p in range(per_step)],
            out_specs=pl.BlockSpec((k, per_step * LANES), lambda i, src: (0, i))),
        compiler_params=_cparams(("arbitrary",)),
        name="repack_w_in",
    )(jnp.asarray(src_rows), *([w_in_t] * per_step))


def _project(x, shift, scale, w_pad, d_a, d_b, tm, want_b16, native):
    kv = KV_A * HD
    h2d, k_a, v_a, kw = _modulate_project(
        x, shift, scale, w_pad, [(0, kv, True), (kv, 2 * kv, True), (2 * kv, 2 * kv + 128, False)],
        tm=min(x.shape[1], 512))
    c = 1024
    wide = tm >= 1024 and d_a % 1024 == 0 and d_b % 1024 == 0
    tn = 1024 if wide else 512
    q_scale = HD ** -0.5 * LOG2E
    (q_a,) = _proj(h2d, w_pad, c, d_a, tn, tm, [("heads", 0, tn, HD, q_scale, BF16)])
    c += d_a
    (g_a,) = _proj(h2d, w_pad, c, d_a, tn, tm, [("flat", 0, tn, 0, 1.0, BF16)])
    c += d_a
    (q_idx,) = _proj(h2d, w_pad, c, H_IDX * D_IDX, tn, tm, [("heads", 0, tn, D_IDX, 1.0, BF16)])
    c += H_IDX * D_IDX
    (q_b,) = _proj(h2d, w_pad, c, d_b, tn, tm, [("heads", 0, tn, HD, q_scale, BF16)])
    c += d_b
    tn_b = tn
    kb_outs = [("flat", 0, tn, 0, 1.0, F32)]
    vb_outs = [("flat", 0, tn, 0, 1.0, F32)]
    if want_b16 and native:
        tn_b = SUBLANES * LANES
        kb_outs = [("native", 0, tn_b, 0, 1.0, F32), ("heads", 0, tn_b, HD, 1.0, BF16)]
        vb_outs = [("native", 0, tn_b, 0, 1.0, F32), ("lanes", 0, tn_b, 0, 1.0, BF16)]
    elif want_b16:
        kb_outs.append(("heads", 0, tn, HD, 1.0, BF16))
        vb_outs.append(("heads", 0, tn, 2 * HD, 1.0, BF16))
    k_b = _proj(h2d, w_pad, c, d_b, tn_b, tm, kb_outs)
    c += d_b
    v_b = _proj(h2d, w_pad, c, d_b, tn_b, tm, vb_outs)
    c += d_b
    (g_b,) = _proj(h2d, w_pad, c, d_b, tn, tm, [("flat", 0, tn, 0, 1.0, BF16)])
    return dict(k_a=k_a, v_a=v_a, kw=kw, q_a=q_a, g_a=g_a, q_idx=q_idx, q_b=q_b, k_b=k_b, v_b=v_b, g_b=g_b)


def _layer(x, mod, past, w_pad, w_out, lamp, subln, ln_g, ln_b, alpha, d_a, d_b, native):
    b, t, d = x.shape
    h_b = d_b // (2 * HD)
    m = b * t
    shift, scale, gate = (mod[:, i * d:(i + 1) * d] for i in range(3))
    prompt = past is None

    p = _project(x, shift.reshape(b, 1, d), scale.reshape(b, 1, d), w_pad, d_a, d_b, tm=min(m, 1024),
                 want_b16=prompt, native=native)

    r3 = lambda a: a.reshape(b, t, a.shape[-1])
    r4 = lambda a: a.reshape(a.shape[0], b, t, a.shape[-1])
    k_a, v_a = p["k_a"], p["v_a"]
    kw, g_a, g_b = r3(p["kw"]), r3(p["g_a"]), r3(p["g_b"])
    q_a, q_idx, q_b = r4(p["q_a"]), r4(p["q_idx"]), r4(p["q_b"])

    if prompt:
        topk = min(TOPK_MAX, t // 4)
        mix = _attn_a_prompt(q_a, k_a, v_a, kw, q_idx, g_a, topk, d_a + d_b)
        mix = _attn_b_prompt(q_b, r4(p["k_b"][1]), r4(p["v_b"][1]), g_b, lamp, subln, mix)
        tm_out, rows_per_gate, gate_rows = min(m, 1024), t, gate
        new_k_b = p["k_b"][0].reshape(1, b, t, h_b, 2, HD)
        if native:
            new_v_b = p["v_b"][0].reshape(b, t, 2, h_b, HD).transpose(0, 1, 3, 2, 4).reshape(1, b, t, h_b, 2 * HD)
        else:
            new_v_b = p["v_b"][0].reshape(1, b, t, h_b, 2 * HD)
    else:
        kc_a, vc_a, kic, kc_b, vc_b = past
        plen = kic.shape[1]
        topk = min(TOPK_MAX, (plen + t) // 4)
        k_b, v_b = r3(p["k_b"][0]), r3(p["v_b"][0])
        rows = lambda a: a.reshape(-1, LANES)
        vc_b_rows = rows(vc_b.reshape(b, plen, h_b, 2, HD).transpose(0, 1, 3, 2, 4))
        mix = _attn_a_sample(q_a, rows(kc_a), rows(vc_a), kic, k_a, v_a, kw, q_idx, g_a, topk, d_a + d_b)
        mix = _attn_b_sample(q_b, rows(kc_b), vc_b_rows, k_b, v_b, g_b, lamp, subln, mix, plen, native)
        tm_out, rows_per_gate = m, 1
        gate_rows = jnp.broadcast_to(gate[:, None, :], (b, t, d)).reshape(m, d)
        new_k_b = k_b.reshape(1, b, t, h_b, 2, HD)
        if native:
            new_v_b = v_b.reshape(b, t, 2, h_b, HD).transpose(0, 1, 3, 2, 4).reshape(1, b, t, h_b, 2 * HD)
        else:
            new_v_b = v_b.reshape(1, b, t, h_b, 2 * HD)

    y = _out_proj(mix.reshape(m, d_a + d_b), w_out, x.reshape(m, d), gate_rows, ln_g, ln_b, alpha,
                  tm=tm_out, tn=512, rows_per_gate=rows_per_gate)
    rows_out = (k_a.reshape(1, b, t, KV_A, HD), v_a.reshape(1, b, t, KV_A, HD), kw[None, :, :, :D_IDX],
                new_k_b, new_v_b)
    return y.reshape(b, t, d), rows_out


def kernel(x_prompt, x_sample, cache_a_k, cache_a_v, cache_a_kidx, cache_b_k, cache_b_v, c_prompt, c_sample,
           w_ada, b_ada, w_in, w_out, lam_q1, lam_k1, lam_q2, lam_k2, subln_g, ln_g, ln_b):
    depth, d, _ = w_ada.shape
    assert depth == 1, "single-layer step"
    d_b = cache_b_v.shape[3] * cache_b_v.shape[4]
    d_a = w_out.shape[1] - d_b
    alpha = (2.0 * depth) ** 0.25
    bp = x_prompt.shape[0]

    mod = _ada(jnp.concatenate([c_prompt, c_sample], axis=0), w_ada[0], b_ada[0])
    native = d_b == 2 * SUBLANES * LANES
    w_pad = _pad_w_in(jnp.transpose(w_in[0]), d_a, d_b, native)
    lamp = jnp.concatenate([lam_q1, lam_k1, lam_q2, lam_k2], axis=0)
    common = (w_pad, w_out[0].astype(BF16), lamp, subln_g, ln_g[0], ln_b[0], alpha, d_a, d_b, native)

    y_p, rows_p = _layer(x_prompt, mod[:bp], None, *common)
    past = (cache_a_k[0], cache_a_v[0], cache_a_kidx[0], cache_b_k[0], cache_b_v[0])
    y_s, rows_s = _layer(x_sample, mod[bp:], past, *common)
    return (y_p, y_s) + rows_p + rows_s
```

```python
import functools
import math

import jax
import jax.numpy as jnp
import numpy as np
from jax import lax
from jax.experimental import pallas as pl
from jax.experimental.pallas import tpu as pltpu

F32 = jnp.float32
BF16 = jnp.bfloat16

HD = 128
CHUNK = 64
CHUNK_SHIFT = 6
KV_A = 2
H_IDX = 16
D_IDX = 64
TOPK_MAX = 256
LN_EPS = 1e-5
LAM_INIT = 0.8 - 0.6 * math.exp(-0.3 * 0)

LOG2E = math.log2(math.e)
NEG = -1e30
N_BISECT = 16

VMEM_LIMIT = 56 * 1024 * 1024
MXU_DIM = 256
LANES = 128
SUBLANES = 8


def _cparams(sem):
    return pltpu.CompilerParams(dimension_semantics=sem, vmem_limit_bytes=VMEM_LIMIT)


def _silu(x):
    return x * jax.nn.sigmoid(x)


def _dot_nt(a, b):
    return lax.dot_general(a, b, (((1,), (1,)), ((), ())), preferred_element_type=F32)


def _alibi_slopes(n):
    return jnp.asarray(2.0 ** (-8.0 * np.arange(1, n + 1) / n), dtype=F32) * LOG2E


def _ada_kernel(c_ref, w_ref, b_ref, o_ref):
    s = _silu(c_ref[...]).astype(BF16)
    o_ref[...] = jnp.dot(s, w_ref[...].astype(BF16), preferred_element_type=F32) + b_ref[...]


def _ada(c, w_ada, b_ada, tn=512):
    n, d = c.shape
    e = w_ada.shape[1]
    return pl.pallas_call(
        _ada_kernel,
        out_shape=jax.ShapeDtypeStruct((n, e), F32),
        grid=(e // tn,),
        in_specs=[pl.BlockSpec((n, d), lambda j: (0, 0)),
                  pl.BlockSpec((d, tn), lambda j: (0, j)),
                  pl.BlockSpec((1, tn), lambda j: (0, j))],
        out_specs=pl.BlockSpec((n, tn), lambda j: (0, j)),
        compiler_params=_cparams(("arbitrary",)),
        name="ada",
    )(c, w_ada, b_ada.reshape(1, e))


def _modproj_kernel(x_ref, shift_ref, scale_ref, w_ref, h_ref, *o_refs, splits):
    h = (x_ref[0] * (1.0 + scale_ref[0]) + shift_ref[0]).astype(h_ref.dtype)
    h_ref[...] = h
    acc = jnp.dot(h, w_ref[...], preferred_element_type=F32)
    tm = acc.shape[0]
    for o_ref, (lo, hi, interleave) in zip(o_refs, splits):
        if interleave:
            for g in range((hi - lo) // LANES):
                o_ref[pl.ds(g, tm, stride=(hi - lo) // LANES), :] = acc[:, lo + g * LANES:lo + (g + 1) * LANES]
        else:
            o_ref[...] = acc[:, lo:hi]


def _modulate_project(x, shift, scale, w_pad, splits, tm):
    b, t, d = x.shape
    n = splits[-1][1]
    nt = t // tm

    def out_shape_block(split):
        lo, hi, interleave = split
        groups = (hi - lo) // LANES if interleave else 1
        return (b * t * groups, (hi - lo) // groups), (tm * groups, (hi - lo) // groups)

    return pl.pallas_call(
        functools.partial(_modproj_kernel, splits=tuple(splits)),
        out_shape=[jax.ShapeDtypeStruct((b * t, d), BF16)]
        + [jax.ShapeDtypeStruct(shape, F32) for shape, _ in map(out_shape_block, splits)],
        grid=(b, nt),
        in_specs=[pl.BlockSpec((1, tm, d), lambda i, j: (i, j, 0)),
                  pl.BlockSpec((1, 1, d), lambda i, j: (i, 0, 0)),
                  pl.BlockSpec((1, 1, d), lambda i, j: (i, 0, 0)),
                  pl.BlockSpec((d, n), lambda i, j: (0, 0))],
        out_specs=[pl.BlockSpec((tm, d), lambda i, j: (i * nt + j, 0))]
        + [pl.BlockSpec(block, lambda i, j: (i * nt + j, 0)) for _, block in map(out_shape_block, splits)],
        compiler_params=_cparams(("arbitrary", "arbitrary")),
        name="modulate_proj",
    )(x, shift, scale, w_pad)


def _proj_kernel(h_ref, w_ref, *o_refs, outs, stream_weights):
    if stream_weights:
        acc = jnp.transpose(_dot_nt(jnp.transpose(w_ref[...]), h_ref[...]))
    else:
        acc = jnp.dot(h_ref[...], w_ref[...], preferred_element_type=F32)
    for o_ref, (kind, lo, hi, width, scale) in zip(o_refs, outs):
        if kind == "flat":
            o_ref[...] = acc[:, lo:hi].astype(o_ref.dtype)
        elif kind == "heads":
            for hh in range((hi - lo) // width):
                blk = acc[:, lo + hh * width: lo + (hh + 1) * width]
                if scale != 1.0:
                    blk = blk * scale
                o_ref[hh] = blk.astype(o_ref.dtype)
        elif kind == "lanes":
            for hh in range((hi - lo) // LANES):
                o_ref[hh] = acc[:, lo + hh * LANES: lo + (hh + 1) * LANES].astype(o_ref.dtype)
        else:
            pieces = jnp.stack([acc[:, lo + s * LANES: lo + (s + 1) * LANES] for s in range(SUBLANES)], axis=0)
            o_ref[...] = jnp.swapaxes(pieces, 0, 1).astype(o_ref.dtype)


def _proj(h2d, w_pad, col_start, ncols, tn, tm, outs):
    m, k = h2d.shape
    nj = ncols // tn
    assert col_start % tn == 0 and ncols % tn == 0 and m % tm == 0
    j0 = col_start // tn
    out_shapes, out_specs, kouts = [], [], []
    for kind, lo, hi, width, scale, dtype in outs:
        if kind == "flat":
            out_shapes.append(jax.ShapeDtypeStruct((m, nj * (hi - lo)), dtype))
            out_specs.append(pl.BlockSpec((tm, hi - lo), lambda i, j: (i, j)))
        elif kind == "heads":
            nh = (hi - lo) // width
            out_shapes.append(jax.ShapeDtypeStruct((nj * nh, m, width), dtype))
            out_specs.append(pl.BlockSpec((nh, tm, width), lambda i, j: (j, i, 0)))
        elif kind == "lanes":
            nh = (hi - lo) // LANES
            out_shapes.append(jax.ShapeDtypeStruct((nh, m, nj * LANES), dtype))
            out_specs.append(pl.BlockSpec((nh, tm, LANES), lambda i, j: (0, i, j)))
        else:
            assert hi - lo == SUBLANES * LANES
            out_shapes.append(jax.ShapeDtypeStruct((m, nj, SUBLANES, LANES), dtype))
            out_specs.append(pl.BlockSpec((tm, None, SUBLANES, LANES), lambda i, j: (i, j, 0, 0)))
        kouts.append((kind, lo, hi, width, scale))
    return pl.pallas_call(
        functools.partial(_proj_kernel, outs=tuple(kouts), stream_weights=m <= MXU_DIM),
        out_shape=out_shapes,
        grid=(m // tm, nj),
        in_specs=[pl.BlockSpec((tm, k), lambda i, j: (i, 0)),
                  pl.BlockSpec((k, tn), lambda i, j: (0, j0 + j))],
        out_specs=out_specs,
        compiler_params=_cparams(("arbitrary", "arbitrary")),
        name=f"proj_c{col_start}",
    )(h2d, w_pad)


def _kth_largest(load, ntiles, k, axis, shape):
    part = (SUBLANES, shape[1]) if axis == 0 else shape

    def fold(x, op):
        if axis == 0:
            return op(x.reshape(x.shape[0] // SUBLANES, SUBLANES, x.shape[1]), axis=0)
        return op(x, axis=axis, keepdims=True)

    def finish(c, op):
        return op(c, axis=0, keepdims=True) if axis == 0 else c

    def reduce_tiles(fn, init):
        return lax.fori_loop(0, ntiles, lambda i, c: fn(load(i), c), init)

    def count_ge(t):
        c = reduce_tiles(lambda x, c: c + fold(jnp.where(x >= t, 1.0, 0.0), jnp.sum), jnp.zeros(part, F32))
        return finish(c, jnp.sum)

    def max_below(t, strict):
        def fn(x, c):
            keep = (x < t) if strict else (x <= t)
            return jnp.maximum(c, fold(jnp.where(keep, x, -jnp.inf), jnp.max))
        return finish(reduce_tiles(fn, jnp.full(part, -jnp.inf, F32)), jnp.max)

    def minmax(x, c):
        mn, mx = c
        mn = jnp.minimum(mn, fold(jnp.where(x > -jnp.inf, x, jnp.inf), jnp.min))
        mx = jnp.maximum(mx, fold(x, jnp.max))
        return mn, mx

    lo, hi = reduce_tiles(minmax, (jnp.full(part, jnp.inf, F32), jnp.full(part, -jnp.inf, F32)))
    lo, hi = finish(lo, jnp.min), finish(hi, jnp.max)
    kf = float(k)

    def bisect(_, c):
        lo, hi = c
        mid = 0.5 * lo + 0.5 * hi
        ok = count_ge(mid) >= kf
        return jnp.where(ok, mid, lo), jnp.where(ok, hi, mid)

    lo, hi = lax.fori_loop(0, N_BISECT, bisect, (lo, hi))
    v = max_below(hi, strict=False)
    c = count_ge(v)

    def walk(vc):
        v, c = vc
        v = jnp.where(c < kf, max_below(v, strict=True), v)
        return v, count_ge(v)

    v, _ = lax.while_loop(lambda vc: jnp.min(vc[1]) < kf, walk, (v, c))
    return v


def _slope_features(n):
    r = _alibi_slopes(n)
    parts = []
    for _ in range(3):
        p = r.astype(BF16).astype(F32)
        parts.append(p)
        r = r - p
    feat = jnp.stack([float(CHUNK) * p for p in parts] + parts, axis=1)
    return jnp.pad(feat, ((0, 0), (0, LANES - feat.shape[1])))


def _position_features(n_rows):
    pos = lax.broadcasted_iota(jnp.int32, (n_rows, HD), 0)
    lane = lax.broadcasted_iota(jnp.int32, (n_rows, HD), 1)
    feat = jnp.where(lane < 3, pos >> CHUNK_SHIFT, jnp.where(lane < 6, pos & (CHUNK - 1), 0))
    return feat.astype(F32).astype(BF16)


def _attn_a_kernel(slopes_ref, sfeat_ref, qa_ref, k_ref, v_ref, kw_ref, kwq_ref, qidx_ref, ga_ref, o_ref,
                   k16, vt16, ki16, st_scr, mb_scr, acc_scr, thr_scr, sq_scr, *, topk, n_heads):
    qi = pl.program_id(1)
    tq = qa_ref.shape[2]
    tk = tq
    rep = n_heads // KV_A
    nk = qi + 1

    @pl.when(qi == 0)
    def _():
        for g in range(KV_A):
            t = k16.shape[1]
            k16[g, :, :HD] = k_ref[pl.ds(g, t, stride=KV_A), :].astype(BF16)
            k16[g, :, HD:] = _position_features(t)
            for j in range(vt16.shape[1]):
                vt16[g, j] = jnp.transpose(v_ref[pl.ds(j * tk * KV_A + g, tk, stride=KV_A), :]).astype(BF16)
        ki16[...] = kw_ref[0, :, :D_IDX].astype(BF16)

    w_t = jnp.transpose(kwq_ref[0])
    w_sc = w_t[D_IDX:D_IDX + H_IDX, :] * (H_IDX ** -0.5)

    def score_tile(kj, c):
        ki = ki16[pl.ds(pl.multiple_of(kj * tk, tk), tk), :]
        acc = jnp.zeros((tk, tq), F32)
        for h in range(H_IDX):
            acc = acc + jnp.maximum(_dot_nt(ki, qidx_ref[h, 0]), 0.0) * w_sc[h:h + 1, :]
        st_scr[kj] = acc
        return c

    lax.fori_loop(0, nk, score_tile, 0)
    krow = lax.broadcasted_iota(jnp.int32, (tk, tq), 0)
    qcol = lax.broadcasted_iota(jnp.int32, (tk, tq), 1)
    st_scr[qi] = jnp.where((krow >> CHUNK_SHIFT) <= (qcol >> CHUNK_SHIFT), st_scr[qi], -jnp.inf)

    thr_scr[...] = jnp.full((1, tq), -3e38, F32)

    @pl.when(qi * tq + CHUNK >= topk)
    def _():
        thr_scr[...] = _kth_largest(lambda i: st_scr[i], nk, topk, 0, (1, tq))

    thr = thr_scr[...]

    def mask_tile(kj, c):
        mb_scr[kj] = jnp.where(st_scr[kj] >= thr, 0.0, NEG)
        return c

    lax.fori_loop(0, nk, mask_tile, 0)

    ahead = jnp.maximum(krow - qcol, 0).astype(F32)

    for g in range(KV_A):
        q_g = jnp.concatenate(
            [qa_ref[g * rep:(g + 1) * rep, 0].reshape(rep * tq, HD),
             jnp.concatenate([jnp.broadcast_to(sfeat_ref[g * rep + hh:g * rep + hh + 1, :], (tq, LANES))
                              for hh in range(rep)], axis=0).astype(BF16)], axis=1)
        acc_scr[...] = jnp.zeros(acc_scr.shape, F32)

        def logits(kj, dst):
            rows = pl.ds(pl.multiple_of(kj * tk, tk), tk)
            sq_scr[dst] = _dot_nt(k16[g, rows, :], q_g)

        def update(kj, src, carry, diagonal=False):
            m_old, l_old = carry
            mb = mb_scr[kj]
            s = jnp.concatenate(
                [sq_scr[src, :, hh * tq:(hh + 1) * tq]
                 + (mb - 2.0 * slopes_ref[g * rep + hh] * ahead if diagonal else mb) for hh in range(rep)],
                axis=1)
            m_new = jnp.maximum(m_old, jnp.max(s, axis=0, keepdims=True))
            alpha = jnp.exp2(m_old - m_new)
            p = jnp.exp2(s - m_new)
            l_new = alpha * l_old + jnp.sum(p, axis=0, keepdims=True)
            acc_scr[...] = alpha * acc_scr[...] + jnp.dot(vt16[g, kj], p.astype(BF16),
                                                          preferred_element_type=F32)
            return m_new, l_new

        def pair(p, carry):
            logits(2 * p + 1, 1)
            carry = update(2 * p, 0, carry)
            logits(2 * p + 2, 0)
            return update(2 * p + 1, 1, carry)

        def tail_odd(carry):
            logits(qi, 1)
            return update(qi, 1, update(qi - 1, 0, carry), diagonal=True)

        def tail_even(carry):
            return update(qi, 0, carry, diagonal=True)

        logits(0, 0)
        carry = lax.fori_loop(0, qi // 2, pair,
                              (jnp.full((1, rep * tq), NEG, F32), jnp.zeros((1, rep * tq), F32)))
        _, l = lax.cond(qi % 2 == 1, tail_odd, tail_even, carry)
        o_t = acc_scr[...] / l
        for hh in range(rep):
            cols = slice((g * rep + hh) * HD, (g * rep + hh + 1) * HD)
            o = jnp.transpose(o_t[:, hh * tq:(hh + 1) * tq])
            o_ref[0, :, cols] = (o * _silu(ga_ref[0, :, cols].astype(F32))).astype(o_ref.dtype)


def _attn_a_prompt(qa, k, v, kw, qidx, ga, topk, d_mix):
    n_heads, b, t, _ = qa.shape
    tq = min(256, topk)
    assert t % tq == 0 and tq % CHUNK == 0 and tq <= topk <= tq + CHUNK
    nkt = t // tq
    d_a = n_heads * HD
    return pl.pallas_call(
        functools.partial(_attn_a_kernel, topk=topk, n_heads=n_heads),
        out_shape=jax.ShapeDtypeStruct((b, t, d_mix), BF16),
        grid=(b, nkt),
        in_specs=[pl.BlockSpec(memory_space=pltpu.SMEM),
                  pl.BlockSpec((n_heads, LANES), lambda i, j: (0, 0)),
                  pl.BlockSpec((n_heads, 1, tq, HD), lambda i, j: (0, i, j, 0)),
                  pl.BlockSpec((t * KV_A, HD), lambda i, j: (i, 0)),
                  pl.BlockSpec((t * KV_A, HD), lambda i, j: (i, 0)),
                  pl.BlockSpec((1, t, 128), lambda i, j: (i, 0, 0)),
                  pl.BlockSpec((1, tq, 128), lambda i, j: (i, j, 0)),
                  pl.BlockSpec((H_IDX, 1, tq, D_IDX), lambda i, j: (0, i, j, 0)),
                  pl.BlockSpec((1, tq, d_a), lambda i, j: (i, j, 0))],
        out_specs=pl.BlockSpec((1, tq, d_a), lambda i, j: (i, j, 0)),
        scratch_shapes=[pltpu.VMEM((KV_A, t, 2 * HD), BF16),
                        pltpu.VMEM((KV_A, nkt, HD, tq), BF16),
                        pltpu.VMEM((t, D_IDX), BF16),
                        pltpu.VMEM((nkt, tq, tq), F32),
                        pltpu.VMEM((nkt, tq, tq), F32),
                        pltpu.VMEM((HD, n_heads // KV_A * tq), F32),
                        pltpu.VMEM((1, tq), F32),
                        pltpu.VMEM((2, tq, n_heads // KV_A * tq), F32)],
        compiler_params=_cparams(("arbitrary", "arbitrary")),
        name="attn_a_prompt",
    )(_alibi_slopes(n_heads), _slope_features(n_heads), qa, k, v, kw, kw, qidx, ga)


def _lam(lamp_ref):
    s1 = jnp.sum(lamp_ref[0:1, :] * lamp_ref[1:2, :], axis=1, keepdims=True)
    s2 = jnp.sum(lamp_ref[2:3, :] * lamp_ref[3:4, :], axis=1, keepdims=True)
    return jnp.exp(s1) - jnp.exp(s2) + LAM_INIT


def _subln_gate(o, subln, g):
    o = o * lax.rsqrt(jnp.mean(o * o, axis=-1, keepdims=True) + LN_EPS)
    o = o * subln * (1.0 - LAM_INIT)
    return o * _silu(g.astype(F32))


def _attn_b_kernel(slopes_ref, sfeat_ref, lamp_ref, subln_ref, q_ref, k_ref, v_ref, g_ref, mix_ref, o_ref,
                   vt_scr, s_scr, acc_scr, kaug_scr, qaug_scr, *, tq):
    h = pl.program_id(1)
    t = q_ref.shape[2]
    tk = tq
    nq = t // tq
    slope = slopes_ref[h]
    lam = _lam(lamp_ref)

    for j in range(nq):
        vt_scr[j] = jnp.transpose(v_ref[0, 0, j * tk:(j + 1) * tk, :].astype(F32)).astype(BF16)

    pos_feat = _position_features(t)
    slope_feat = jnp.broadcast_to(sfeat_ref[pl.ds(h, 1), :], (t, LANES)).astype(BF16)
    for mi in range(2):
        kaug_scr[mi, :, :HD] = k_ref[mi, 0]
        kaug_scr[mi, :, HD:] = pos_feat
        qaug_scr[mi, :, :HD] = q_ref[mi, 0]
        qaug_scr[mi, :, HD:] = slope_feat

    krow = lax.broadcasted_iota(jnp.int32, (tk, tq), 0)
    qcol = lax.broadcasted_iota(jnp.int32, (tk, tq), 1)
    ahead = jnp.maximum(krow - qcol, 0).astype(F32)
    diag_bias = jnp.where((krow >> CHUNK_SHIFT) <= (qcol >> CHUNK_SHIFT), -2.0 * slope * ahead, NEG)

    pairs = [(qi, j) for qi in range(nq) for j in range(qi + 1)]

    def issue_logits(n):
        qi, j = pairs[n]
        for mi in range(2):
            s_scr[n % 2, mi] = _dot_nt(kaug_scr[mi, j * tk:(j + 1) * tk, :], qaug_scr[mi, qi * tq:(qi + 1) * tq, :])

    issue_logits(0)
    carry = None
    for n, (qi, j) in enumerate(pairs):
        if n + 1 < len(pairs):
            issue_logits(n + 1)
        if j == 0:
            acc_scr[...] = jnp.zeros(acc_scr.shape, F32)
            init = (jnp.full((1, tq), NEG, F32), jnp.zeros((1, tq), F32))
            carry = [init, init]
        for mi in range(2):
            m_old, l_old = carry[mi]
            s = s_scr[n % 2, mi] + diag_bias if j == qi else s_scr[n % 2, mi]
            m_new = jnp.maximum(m_old, jnp.max(s, axis=0, keepdims=True))
            alpha = jnp.exp2(m_old - m_new)
            p = jnp.exp2(s - m_new)
            l_new = alpha * l_old + jnp.sum(p, axis=0, keepdims=True)
            acc_scr[mi] = alpha * acc_scr[mi] + jnp.dot(vt_scr[j], p.astype(BF16), preferred_element_type=F32)
            carry[mi] = (m_new, l_new)
        if j == qi:
            o_t = acc_scr[0] / carry[0][1] - lam * (acc_scr[1] / carry[1][1])
            o_t = o_t * lax.rsqrt(jnp.mean(o_t * o_t, axis=0, keepdims=True) + LN_EPS)
            o = jnp.transpose(o_t) * subln_ref[...] * (1.0 - LAM_INIT)
            rows = slice(qi * tq, (qi + 1) * tq)
            o_ref[0, rows, :] = (o * _silu(g_ref[0, rows, :].astype(F32))).astype(o_ref.dtype)


def _attn_b_prompt(qb, kb16, vb16, gb, lamp, subln, mix):
    _, b, t, _ = qb.shape
    n_heads = vb16.shape[0]
    col0 = (mix.shape[2] - n_heads * 2 * HD) // (2 * HD)
    tq = min(512, t)
    assert t % tq == 0 and tq % CHUNK == 0
    return pl.pallas_call(
        functools.partial(_attn_b_kernel, tq=tq),
        out_shape=jax.ShapeDtypeStruct(mix.shape, mix.dtype),
        grid=(b, n_heads),
        in_specs=[pl.BlockSpec(memory_space=pltpu.SMEM),
                  pl.BlockSpec((n_heads, LANES), lambda i, h: (0, 0)),
                  pl.BlockSpec((4, HD), lambda i, h: (0, 0)),
                  pl.BlockSpec((1, 2 * HD), lambda i, h: (0, 0)),
                  pl.BlockSpec((2, 1, t, HD), lambda i, h: (h, i, 0, 0)),
                  pl.BlockSpec((2, 1, t, HD), lambda i, h: (h, i, 0, 0)),
                  pl.BlockSpec((1, 1, t, 2 * HD), lambda i, h: (h, i, 0, 0)),
                  pl.BlockSpec((1, t, 2 * HD), lambda i, h: (i, 0, h)),
                  pl.BlockSpec(memory_space=pl.ANY)],
        out_specs=pl.BlockSpec((1, t, 2 * HD), lambda i, h: (i, 0, col0 + h)),
        input_output_aliases={8: 0},
        scratch_shapes=[pltpu.VMEM((t // tq, 2 * HD, tq), BF16),
                        pltpu.VMEM((2, 2, tq, tq), F32),
                        pltpu.VMEM((2, 2 * HD, tq), F32),
                        pltpu.VMEM((2, t, 2 * HD), BF16),
                        pltpu.VMEM((2, t, 2 * HD), BF16)],
        compiler_params=_cparams(("arbitrary", "arbitrary")),
        name="attn_b_prompt",
    )(_alibi_slopes(n_heads), _slope_features(n_heads), lamp, subln, qb, kb16, vb16, gb, mix)


def _attn_a_sample_kernel(slopes_ref, qa_ref, kc_ref, vc_ref, kic_ref, kn_ref, vn_ref, kwn_ref, qidx_ref,
                          ga_ref, o_ref, k16, v16, sc_scr, s_scr, p_scr, *, topk, n_heads, cw):
    t = qa_ref.shape[2]
    past = kc_ref.shape[0] // KV_A
    s_pad = sc_scr.shape[1]
    rep = n_heads // KV_A
    kcol = lax.broadcasted_iota(jnp.int32, (t, s_pad), 1)
    qrow = lax.broadcasted_iota(jnp.int32, (t, s_pad), 0)

    qi_all = qidx_ref[:, 0].reshape(H_IDX * t, D_IDX)
    w_q = kwn_ref[0, :, D_IDX:D_IDX + H_IDX] * (H_IDX ** -0.5)

    def scores(ki):
        r = jnp.maximum(_dot_nt(qi_all, ki), 0.0)
        acc = jnp.zeros((t, ki.shape[0]), F32)
        for h in range(H_IDX):
            acc = acc + r[h * t:(h + 1) * t, :] * w_q[:, h:h + 1]
        return acc

    for c0 in range(0, past, cw):
        sc_scr[:, c0:c0 + cw] = scores(kic_ref[0, c0:c0 + cw, :].astype(BF16))
    tail = s_pad - past
    ki_new = jnp.concatenate(
        [kwn_ref[0, :, :D_IDX], jnp.zeros((tail - t, D_IDX), F32)], axis=0).astype(BF16)
    tail_valid = lax.broadcasted_iota(jnp.int32, (t, tail), 1) < t
    sc_scr[:, past:] = jnp.where(tail_valid, scores(ki_new), -jnp.inf)

    thr = _kth_largest(lambda i: sc_scr[...], 1, topk, 1, (t, 1))
    mbias = jnp.where(sc_scr[...] >= thr, 0.0, NEG)
    dist = jnp.abs((past + qrow - kcol).astype(F32))

    for g in range(KV_A):
        k16[:past] = kc_ref[pl.ds(g, past, stride=KV_A), :].astype(BF16)
        v16[:past] = vc_ref[pl.ds(g, past, stride=KV_A), :].astype(BF16)
        zpad = jnp.zeros((tail - t, HD), F32)
        k16[past:] = jnp.concatenate([kn_ref[pl.ds(g, t, stride=KV_A), :], zpad], axis=0).astype(BF16)
        v16[past:] = jnp.concatenate([vn_ref[pl.ds(g, t, stride=KV_A), :], zpad], axis=0).astype(BF16)
        q_g = qa_ref[g * rep:(g + 1) * rep, 0].reshape(rep * t, HD)
        s_scr[...] = _dot_nt(q_g, k16[...])
        inv_l = []
        for hh in range(rep):
            rows = slice(hh * t, (hh + 1) * t)
            s = s_scr[rows, :] - slopes_ref[g * rep + hh] * dist + mbias
            p = jnp.exp2(s - jnp.max(s, axis=1, keepdims=True))
            inv_l.append(1.0 / jnp.sum(p, axis=1, keepdims=True))
            p_scr[rows, :] = p.astype(BF16)
        o_g = jnp.dot(p_scr[...], v16[...], preferred_element_type=F32)
        for hh in range(rep):
            hc = slice((g * rep + hh) * HD, (g * rep + hh + 1) * HD)
            o = o_g[hh * t:(hh + 1) * t, :] * inv_l[hh]
            o_ref[0, :, hc] = (o * _silu(ga_ref[0, :, hc].astype(F32))).astype(o_ref.dtype)


def _attn_a_sample(qa, kc, vc, kic, kn, vn, kwn, qidx, ga, topk, d_mix, cw=512):
    n_heads, b, t, _ = qa.shape
    past = kic.shape[1]
    s_pad = past + 128
    assert t <= 128 and past % cw == 0
    d_a = n_heads * HD
    return pl.pallas_call(
        functools.partial(_attn_a_sample_kernel, topk=topk, n_heads=n_heads, cw=cw),
        out_shape=jax.ShapeDtypeStruct((b, t, d_mix), BF16),
        grid=(b,),
        in_specs=[pl.BlockSpec(memory_space=pltpu.SMEM),
                  pl.BlockSpec((n_heads, 1, t, HD), lambda i: (0, i, 0, 0)),
                  pl.BlockSpec((past * KV_A, HD), lambda i: (i, 0)),
                  pl.BlockSpec((past * KV_A, HD), lambda i: (i, 0)),
                  pl.BlockSpec((1, past, D_IDX), lambda i: (i, 0, 0)),
                  pl.BlockSpec((t * KV_A, HD), lambda i: (i, 0)),
                  pl.BlockSpec((t * KV_A, HD), lambda i: (i, 0)),
                  pl.BlockSpec((1, t, 128), lambda i: (i, 0, 0)),
                  pl.BlockSpec((H_IDX, 1, t, D_IDX), lambda i: (0, i, 0, 0)),
                  pl.BlockSpec((1, t, d_a), lambda i: (i, 0, 0))],
        out_specs=pl.BlockSpec((1, t, d_a), lambda i: (i, 0, 0)),
        scratch_shapes=[pltpu.VMEM((s_pad, HD), BF16),
                        pltpu.VMEM((s_pad, HD), BF16),
                        pltpu.VMEM((t, s_pad), F32),
                        pltpu.VMEM((n_heads // KV_A * t, s_pad), F32),
                        pltpu.VMEM((n_heads // KV_A * t, s_pad), BF16)],
        compiler_params=_cparams(("arbitrary",)),
        name="attn_a_sample",
    )(_alibi_slopes(n_heads), qa, kc, vc, kic, kn, vn, kwn, qidx, ga)


def _attn_b_sample_kernel(slopes_ref, lamp_ref, subln_ref, q_ref, kc_ref, vc_ref, kn_ref, vn_ref, g_ref, mix_ref,
                          o_ref, m_scr, l_scr, acc_scr, *, n_heads, past, v_half_major):
    c = pl.program_id(1)
    t = q_ref.shape[2]
    slots = 2 * n_heads
    pc = kc_ref.shape[0] // slots
    lam = _lam(lamp_ref)

    @pl.when(c == 0)
    def _():
        m_scr[...] = jnp.full(m_scr.shape, NEG, F32)
        l_scr[...] = jnp.zeros(l_scr.shape, F32)
        acc_scr[...] = jnp.zeros(acc_scr.shape, F32)

    def q_blockdiag(h):
        z = jnp.zeros((t, HD), BF16)
        return jnp.concatenate([jnp.concatenate([q_ref[2 * h, 0], z], axis=1),
                                jnp.concatenate([z, q_ref[2 * h + 1, 0]], axis=1)], axis=0)

    def update(h, s, v16):
        m_old = m_scr[h]
        m_new = jnp.maximum(m_old, jnp.max(s, axis=1, keepdims=True))
        alpha = jnp.exp2(m_old - m_new)
        p = jnp.exp2(s - m_new)
        l_scr[h] = alpha * l_scr[h] + jnp.sum(p, axis=1, keepdims=True)
        acc_scr[h] = alpha * acc_scr[h] + jnp.dot(p.astype(BF16), v16, preferred_element_type=F32)
        m_scr[h] = m_new

    qrow = lax.broadcasted_iota(jnp.int32, (t, pc), 0)
    kcol = lax.broadcasted_iota(jnp.int32, (t, pc), 1)
    dist = (past + qrow - kcol).astype(F32) - (c * pc).astype(F32)
    for h in range(n_heads):
        k01 = jnp.concatenate([kc_ref[pl.ds(2 * h, pc, stride=slots), :],
                               kc_ref[pl.ds(2 * h + 1, pc, stride=slots), :]], axis=1).astype(BF16)
        v_h = jnp.concatenate([vc_ref[pl.ds(h, pc, stride=slots), :],
                               vc_ref[pl.ds(n_heads + h, pc, stride=slots), :]], axis=1).astype(BF16)
        bias = -slopes_ref[h] * dist
        update(h, _dot_nt(q_blockdiag(h), k01) + jnp.concatenate([bias, bias], axis=0), v_h)

    @pl.when(c == pl.num_programs(1) - 1)
    def _():
        qr = lax.broadcasted_iota(jnp.int32, (t, LANES), 0)
        kc_ = lax.broadcasted_iota(jnp.int32, (t, LANES), 1)
        dist_new = jnp.abs((qr - kc_).astype(F32))
        zpad = jnp.zeros((LANES - t, 2 * HD), F32)
        for h in range(n_heads):
            cols = slice(h * 2 * HD, (h + 1) * 2 * HD)
            kn = jnp.concatenate([kn_ref[0, :, cols], zpad], axis=0).astype(BF16)
            if v_half_major:
                v_new = jnp.concatenate([vn_ref[0, :, (half * n_heads + h) * HD:(half * n_heads + h + 1) * HD]
                                         for half in range(2)], axis=1)
            else:
                v_new = vn_ref[0, :, cols]
            vn = jnp.concatenate([v_new, zpad], axis=0).astype(BF16)
            bias = jnp.where(kc_ < t, -slopes_ref[h] * dist_new, NEG)
            update(h, _dot_nt(q_blockdiag(h), kn) + jnp.concatenate([bias, bias], axis=0), vn)
            o_all = acc_scr[h] / l_scr[h]
            o = o_all[:t] - lam * o_all[t:]
            o_ref[0, :, cols] = _subln_gate(o, subln_ref[...], g_ref[0, :, cols]).astype(o_ref.dtype)


def _attn_b_sample(qb, kc, vc, kn, vn, gb, lamp, subln, mix, past, v_half_major, pc=1024):
    _, b, t, _ = qb.shape
    d_b = gb.shape[2]
    n_heads = d_b // (2 * HD)
    slots = 2 * n_heads
    pc = min(pc, past)
    assert t <= LANES and past % pc == 0 and (mix.shape[2] - d_b) % d_b == 0
    nc = past // pc
    col0 = (mix.shape[2] - d_b) // d_b
    return pl.pallas_call(
        functools.partial(_attn_b_sample_kernel, n_heads=n_heads, past=past, v_half_major=v_half_major),
        out_shape=jax.ShapeDtypeStruct(mix.shape, mix.dtype),
        grid=(b, nc),
        in_specs=[pl.BlockSpec(memory_space=pltpu.SMEM),
                  pl.BlockSpec((4, HD), lambda i, c: (0, 0)),
                  pl.BlockSpec((1, 2 * HD), lambda i, c: (0, 0)),
                  pl.BlockSpec((slots, 1, t, HD), lambda i, c: (0, i, 0, 0)),
                  pl.BlockSpec((pc * slots, LANES), lambda i, c: (i * nc + c, 0)),
                  pl.BlockSpec((pc * slots, LANES), lambda i, c: (i * nc + c, 0)),
                  pl.BlockSpec((1, t, d_b), lambda i, c: (i, 0, 0)),
                  pl.BlockSpec((1, t, d_b), lambda i, c: (i, 0, 0)),
                  pl.BlockSpec((1, t, d_b), lambda i, c: (i, 0, 0)),
                  pl.BlockSpec(memory_space=pl.ANY)],
        out_specs=pl.BlockSpec((1, t, d_b), lambda i, c: (i, 0, col0)),
        input_output_aliases={9: 0},
        scratch_shapes=[pltpu.VMEM((n_heads, 2 * t, 1), F32),
                        pltpu.VMEM((n_heads, 2 * t, 1), F32),
                        pltpu.VMEM((n_heads, 2 * t, 2 * HD), F32)],
        compiler_params=_cparams(("arbitrary", "arbitrary")),
        name="attn_b_sample",
    )(_alibi_slopes(n_heads), lamp, subln, qb, kc, vc, kn, vn, gb, mix)


def _out_kernel(mix_ref, w_ref, x_ref, gate_ref, lng_ref, lnb_ref, o_ref, *, alpha, nj):
    j = pl.program_id(1)
    tn = x_ref.shape[1]
    r = alpha * x_ref[...] + gate_ref[...] * jnp.dot(mix_ref[...], w_ref[...], preferred_element_type=F32)
    o_ref[:, pl.ds(pl.multiple_of(j * tn, tn), tn)] = r

    @pl.when(j == nj - 1)
    def _():
        d = nj * tn
        tot = jnp.zeros((o_ref.shape[0], 1), F32)
        for c in range(nj):
            tot = tot + jnp.sum(o_ref[:, c * tn:(c + 1) * tn], axis=1, keepdims=True)
        mu = tot / d
        sq = jnp.zeros_like(tot)
        for c in range(nj):
            dv = o_ref[:, c * tn:(c + 1) * tn] - mu
            sq = sq + jnp.sum(dv * dv, axis=1, keepdims=True)
        inv = lax.rsqrt(sq / d + LN_EPS)
        for c in range(nj):
            cols = slice(c * tn, (c + 1) * tn)
            o_ref[:, cols] = (o_ref[:, cols] - mu) * inv * lng_ref[:, cols] + lnb_ref[:, cols]


def _out_proj(mix, w_out, x2d, gate, ln_g, ln_b, alpha, tm, tn, rows_per_gate):
    m, dm = mix.shape
    d = w_out.shape[1]
    assert m % tm == 0 and d % tn == 0
    if rows_per_gate == 1:
        gate_spec = pl.BlockSpec((tm, tn), lambda i, j: (i, j))
    else:
        assert rows_per_gate % tm == 0
        gate = gate.reshape(-1, 1, d)
        gate_spec = pl.BlockSpec((None, 1, tn), lambda i, j: (i * tm // rows_per_gate, 0, j))
    return pl.pallas_call(
        functools.partial(_out_kernel, alpha=alpha, nj=d // tn),
        out_shape=jax.ShapeDtypeStruct((m, d), F32),
        grid=(m // tm, d // tn),
        in_specs=[pl.BlockSpec((tm, dm), lambda i, j: (i, 0)),
                  pl.BlockSpec((dm, tn), lambda i, j: (0, j)),
                  pl.BlockSpec((tm, tn), lambda i, j: (i, j)),
                  gate_spec,
                  pl.BlockSpec((1, d), lambda i, j: (0, 0)),
                  pl.BlockSpec((1, d), lambda i, j: (0, 0))],
        out_specs=pl.BlockSpec((tm, d), lambda i, j: (i, 0), pipeline_mode=pl.Buffered(1)),
        compiler_params=_cparams(("arbitrary", "arbitrary")),
        name="out_proj",
    )(mix, w_out, x2d, gate, ln_g.reshape(1, d), ln_b.reshape(1, d))


def _w_in_layout(d_a, d_b, v_half_major):
    kv = KV_A * HD
    sizes = dict(q_a=d_a, k_a=kv, v_a=kv, g_a=d_a, q_idx=H_IDX * D_IDX, kw=D_IDX + H_IDX, q_b=d_b, k_b=d_b,
                 v_b=d_b, g_b=d_b)
    src, off = {}, 0
    for name in ("q_a", "k_a", "v_a", "g_a", "q_idx", "kw", "q_b", "k_b", "v_b", "g_b"):
        src[name] = off
        off += sizes[name]
    moves, dst = [], 0
    for name in ("k_a", "v_a", "kw"):
        moves.append((dst, src[name], sizes[name]))
        dst += sizes[name]
    zero = (dst, 1024)
    dst = 1024
    for name in ("q_a", "g_a", "q_idx", "q_b", "k_b"):
        moves.append((dst, src[name], sizes[name]))
        dst += sizes[name]
    n_b = d_b // (2 * HD)
    if v_half_major:
        for half in range(2):
            for h in range(n_b):
                moves.append((dst + (half * n_b + h) * HD, src["v_b"] + (h * 2 + half) * HD, HD))
    else:
        moves.append((dst, src["v_b"], d_b))
    dst += d_b
    moves.append((dst, src["g_b"], d_b))
    return moves, zero, dst + d_b


def _repack_kernel(src_ref, *refs):
    o_ref = refs[-1]
    for piece, w_ref in enumerate(refs[:-1]):
        o_ref[:, piece * LANES:(piece + 1) * LANES] = jnp.transpose(w_ref[...]).astype(o_ref.dtype)


ROW_ALIGN = 16


def _pad_w_in(w_in_t, d_a, d_b, v_half_major, per_step=4):
    moves, _, n_out = _w_in_layout(d_a, d_b, v_half_major)
    n_in, k = w_in_t.shape
    src_rows = np.zeros(n_out // LANES, np.int32)
    for dst, src, width in moves:
        for c in range(0, width, LANES):
            assert dst % LANES == 0 and (src + c) % ROW_ALIGN == 0 and src + c + LANES <= n_in
            src_rows[(dst + c) // LANES] = (src + c) // ROW_ALIGN
    return pl.pallas_call(
        _repack_kernel,
        out_shape=jax.ShapeDtypeStruct((k, n_out), BF16),
        grid_spec=pltpu.PrefetchScalarGridSpec(
            num_scalar_prefetch=1,
            grid=(n_out // (per_step * LANES),),
            in_specs=[pl.BlockSpec((pl.Element(LANES), pl.Element(k)),
                                   lambda i, src, p=p: (src[i * per_step + p] * ROW_ALIGN, 0))
                      for p in range(per_step)],
            out_specs=pl.BlockSpec((k, per_step * LANES), lambda i, src: (0, i))),
        compiler_params=_cparams(("arbitrary",)),
        name="repack_w_in",
    )(jnp.asarray(src_rows), *([w_in_t] * per_step))


def _project(x, shift, scale, w_pad, d_a, d_b, tm, want_b16, native):
    kv = KV_A * HD
    h2d, k_a, v_a, kw = _modulate_project(
        x, shift, scale, w_pad, [(0, kv, True), (kv, 2 * kv, True), (2 * kv, 2 * kv + 128, False)],
        tm=min(x.shape[1], 512))
    c = 1024
    wide = tm >= 1024 and d_a % 1024 == 0 and d_b % 1024 == 0
    tn = 1024 if wide else 512
    q_scale = HD ** -0.5 * LOG2E
    (q_a,) = _proj(h2d, w_pad, c, d_a, tn, tm, [("heads", 0, tn, HD, q_scale, BF16)])
    c += d_a
    (g_a,) = _proj(h2d, w_pad, c, d_a, tn, tm, [("flat", 0, tn, 0, 1.0, BF16)])
    c += d_a
    (q_idx,) = _proj(h2d, w_pad, c, H_IDX * D_IDX, tn, tm, [("heads", 0, tn, D_IDX, 1.0, BF16)])
    c += H_IDX * D_IDX
    (q_b,) = _proj(h2d, w_pad, c, d_b, tn, tm, [("heads", 0, tn, HD, q_scale, BF16)])
    c += d_b
    tn_b = tn
    kb_outs = [("flat", 0, tn, 0, 1.0, F32)]
    vb_outs = [("flat", 0, tn, 0, 1.0, F32)]
    if want_b16 and native:
        tn_b = SUBLANES * LANES
        kb_outs = [("native", 0, tn_b, 0, 1.0, F32), ("heads", 0, tn_b, HD, 1.0, BF16)]
        vb_outs = [("native", 0, tn_b, 0, 1.0, F32), ("lanes", 0, tn_b, 0, 1.0, BF16)]
    elif want_b16:
        kb_outs.append(("heads", 0, tn, HD, 1.0, BF16))
        vb_outs.append(("heads", 0, tn, 2 * HD, 1.0, BF16))
    k_b = _proj(h2d, w_pad, c, d_b, tn_b, tm, kb_outs)
    c += d_b
    v_b = _proj(h2d, w_pad, c, d_b, tn_b, tm, vb_outs)
    c += d_b
    (g_b,) = _proj(h2d, w_pad, c, d_b, tn, tm, [("flat", 0, tn, 0, 1.0, BF16)])
    return dict(k_a=k_a, v_a=v_a, kw=kw, q_a=q_a, g_a=g_a, q_idx=q_idx, q_b=q_b, k_b=k_b, v_b=v_b, g_b=g_b)


def _layer(x, mod, past, w_pad, w_out, lamp, subln, ln_g, ln_b, alpha, d_a, d_b, native):
    b, t, d = x.shape
    h_b = d_b // (2 * HD)
    m = b * t
    shift, scale, gate = (mod[:, i * d:(i + 1) * d] for i in range(3))
    prompt = past is None

    p = _project(x, shift.reshape(b, 1, d), scale.reshape(b, 1, d), w_pad, d_a, d_b, tm=min(m, 1024),
                 want_b16=prompt, native=native)

    r3 = lambda a: a.reshape(b, t, a.shape[-1])
    r4 = lambda a: a.reshape(a.shape[0], b, t, a.shape[-1])
    k_a, v_a = p["k_a"], p["v_a"]
    kw, g_a, g_b = r3(p["kw"]), r3(p["g_a"]), r3(p["g_b"])
    q_a, q_idx, q_b = r4(p["q_a"]), r4(p["q_idx"]), r4(p["q_b"])

    if prompt:
        topk = min(TOPK_MAX, t // 4)
        mix = _attn_a_prompt(q_a, k_a, v_a, kw, q_idx, g_a, topk, d_a + d_b)
        mix = _attn_b_prompt(q_b, r4(p["k_b"][1]), r4(p["v_b"][1]), g_b, lamp, subln, mix)
        tm_out, rows_per_gate, gate_rows = min(m, 1024), t, gate
        new_k_b = p["k_b"][0].reshape(1, b, t, h_b, 2, HD)
        if native:
            new_v_b = p["v_b"][0].reshape(b, t, 2, h_b, HD).transpose(0, 1, 3, 2, 4).reshape(1, b, t, h_b, 2 * HD)
        else:
            new_v_b = p["v_b"][0].reshape(1, b, t, h_b, 2 * HD)
    else:
        kc_a, vc_a, kic, kc_b, vc_b = past
        plen = kic.shape[1]
        topk = min(TOPK_MAX, (plen + t) // 4)
        k_b, v_b = r3(p["k_b"][0]), r3(p["v_b"][0])
        rows = lambda a: a.reshape(-1, LANES)
        vc_b_rows = rows(vc_b.reshape(b, plen, h_b, 2, HD).transpose(0, 1, 3, 2, 4))
        mix = _attn_a_sample(q_a, rows(kc_a), rows(vc_a), kic, k_a, v_a, kw, q_idx, g_a, topk, d_a + d_b)
        mix = _attn_b_sample(q_b, rows(kc_b), vc_b_rows, k_b, v_b, g_b, lamp, subln, mix, plen, native)
        tm_out, rows_per_gate = m, 1
        gate_rows = jnp.broadcast_to(gate[:, None, :], (b, t, d)).reshape(m, d)
        new_k_b = k_b.reshape(1, b, t, h_b, 2, HD)
        if native:
            new_v_b = v_b.reshape(b, t, 2, h_b, HD).transpose(0, 1, 3, 2, 4).reshape(1, b, t, h_b, 2 * HD)
        else:
            new_v_b = v_b.reshape(1, b, t, h_b, 2 * HD)

    y = _out_proj(mix.reshape(m, d_a + d_b), w_out, x.reshape(m, d), gate_rows, ln_g, ln_b, alpha,
                  tm=tm_out, tn=512, rows_per_gate=rows_per_gate)
    rows_out = (k_a.reshape(1, b, t, KV_A, HD), v_a.reshape(1, b, t, KV_A, HD), kw[None, :, :, :D_IDX],
                new_k_b, new_v_b)
    return y.reshape(b, t, d), rows_out


def kernel(x_prompt, x_sample, cache_a_k, cache_a_v, cache_a_kidx, cache_b_k, cache_b_v, c_prompt, c_sample,
           w_ada, b_ada, w_in, w_out, lam_q1, lam_k1, lam_q2, lam_k2, subln_g, ln_g, ln_b):
    depth, d, _ = w_ada.shape
    assert depth == 1, "single-layer step"
    d_b = cache_b_v.shape[3] * cache_b_v.shape[4]
    d_a = w_out.shape[1] - d_b
    alpha = (2.0 * depth) ** 0.25
    bp = x_prompt.shape[0]

    mod = _ada(jnp.concatenate([c_prompt, c_sample], axis=0), w_ada[0], b_ada[0])
    native = d_b == 2 * SUBLANES * LANES
    w_pad = _pad_w_in(jnp.transpose(w_in[0]), d_a, d_b, native)
    lamp = jnp.concatenate([lam_q1, lam_k1, lam_q2, lam_k2], axis=0)
    common = (w_pad, w_out[0].astype(BF16), lamp, subln_g, ln_g[0], ln_b[0], alpha, d_a, d_b, native)

    y_p, rows_p = _layer(x_prompt, mod[:bp], None, *common)
    past = (cache_a_k[0], cache_a_v[0], cache_a_kidx[0], cache_b_k[0], cache_b_v[0])
    y_s, rows_s = _layer(x_sample, mod[bp:], past, *common)
    return (y_p, y_s) + rows_p + rows_s
```
